```python
import math
import jax
import jax.numpy as jnp
from jax import lax
import numpy as np

D_MODEL = 1024
BATCH = 4
SEQ = 8192
DEPTH = 1
DEC_BATCH = 128
DEC_SEQ = 1
PAST_LEN = 8192
PAGE_SIZE = 128

N_HEADS = 8
N_KV_HEADS = 2
HEAD_DIM = 64
GQA = N_HEADS // N_KV_HEADS
ATT_DIM = N_HEADS * HEAD_DIM
KV_DIM = 2 * N_KV_HEADS * HEAD_DIM
CMP_LEN = 32
CMP_STRIDE = 16
CMP_HID = 128
SEL_BLOCK = 64
SEL_TOPK = 16
WINDOW = 512
Q_BLOCK = 128
SSM_HEADS = 8
SSM_HEAD_DIM = 64
D_INNER = SSM_HEADS * SSM_HEAD_DIM
N_GROUPS = 2
D_STATE = 128
CONV_W = 4
CONV_DIM = D_INNER + 2 * N_GROUPS * D_STATE
SSD_CHUNK = 128
DT_MIN = 0.001
DT_MAX = 0.1
D_FF = 2816
PLE_DIM = 256
MIX_DIM = ATT_DIM + D_INNER
D_IN_PROJ = ATT_DIM + 3 * KV_DIM + 3 * N_HEADS + D_INNER + CONV_DIM + SSM_HEADS
EPS = 1e-6

kernel_name = 'nsa_ssd_macaron_hybrid_step'


def rmsnorm(x, g):
    xf = x.astype(jnp.float32)
    y = xf * lax.rsqrt(jnp.mean(xf * xf, axis=-1, keepdims=True) + EPS)
    return (y * g.astype(jnp.float32)).astype(x.dtype)


def grouped_rmsnorm(y, g):
    yg = y.reshape(y.shape[:-1] + (N_GROUPS, D_INNER // N_GROUPS))
    yg = yg * lax.rsqrt(jnp.mean(yg * yg, axis=-1, keepdims=True) + EPS)
    return yg.reshape(y.shape) * g.astype(jnp.float32)


def masked_softmax(s, mask):
    s = jnp.where(mask, s.astype(jnp.float32), -jnp.inf)
    m = jnp.max(s, axis=-1, keepdims=True)
    m = jnp.where(jnp.isfinite(m), m, 0.0)
    e = jnp.exp(s - m)
    return e / jnp.maximum(jnp.sum(e, axis=-1, keepdims=True), jnp.finfo(jnp.float32).tiny)


def swiglu(x, wg, wu, wd):
    return (jax.nn.silu(x @ wg) * (x @ wu)) @ wd


def split_proj(h, w_in):
    B, T = h.shape[:2]
    u = h @ w_in
    cuts = np.cumsum([ATT_DIM, KV_DIM, KV_DIM, KV_DIM, 3 * N_HEADS, D_INNER, CONV_DIM]).tolist()
    q, kvc, kvs, kvw, g, z, xbc, dt = jnp.split(u, cuts, axis=-1)
    kv_shape = (B, T, 2, N_KV_HEADS, HEAD_DIM)
    return (q.reshape(B, T, N_KV_HEADS, GQA, HEAD_DIM), g.reshape(B, T, N_KV_HEADS, GQA, 3),
            kvc.reshape(kv_shape), kvs.reshape(kv_shape), kvw.reshape(kv_shape), z, xbc, dt)


def cmp_project(blocks, w1, pe):
    r = CMP_LEN // CMP_STRIDE
    w1r = w1.reshape(2, r, CMP_STRIDE, HEAD_DIM, CMP_HID)
    per = pe.reshape(2, r, CMP_STRIDE, HEAD_DIM)
    proj = jnp.einsum('bnschd,crsdf->rbnchf', blocks, w1r)
    return proj + jnp.einsum('crsd,crsdf->rcf', per, w1r)[:, None, None, :, None, :]


def cmp_combine(proj, n_cmp, b1, w2):
    r = proj.shape[0]
    hid = sum(proj[m, :, m:m + n_cmp] for m in range(r))
    hid = jax.nn.silu(hid + b1[:, None, :])
    out = jnp.einsum('bnchf,cfd->bnchd', hid, w2)
    return out[:, :, 0], out[:, :, 1]


def cmp_to_sel(n_cmp, n_blk):
    cs = jnp.arange(n_cmp)[:, None] * CMP_STRIDE
    bs = jnp.arange(n_blk)[None, :] * SEL_BLOCK
    ov = jnp.minimum(cs + CMP_LEN, bs + SEL_BLOCK) - jnp.maximum(cs, bs)
    return jnp.clip(ov, 0, None).astype(jnp.float32) / CMP_LEN


def nsa_branches(q, g, t, kc, vc, sel_fetch, kw, vw, kw_pos, n_blk):
    scale = HEAD_DIM ** -0.5
    B, T = q.shape[:2]
    n_cmp = kc.shape[1]
    c_end = jnp.arange(n_cmp) * CMP_STRIDE + CMP_LEN - 1
    cmask = c_end[None, :] <= t[:, None]
    s = jnp.einsum('bthgd,bchd->bthgc', q, kc) * scale
    p_cmp = masked_softmax(s, cmask[None, :, None, None, :])
    o_cmp = jnp.einsum('bthgc,bchd->bthgd', p_cmp, vc)
    imp = jnp.einsum('bthgc,cj->bthj', p_cmp, cmp_to_sel(n_cmp, n_blk))
    j = jnp.arange(n_blk)[None, :]
    cb = (t // SEL_BLOCK)[:, None]
    forced = (j == 0) | (j == cb) | (j == cb - 1)
    causal = j * SEL_BLOCK <= t[:, None]
    imp = jnp.where(forced[None, :, None, :], jnp.inf, imp)
    imp = jnp.where(causal[None, :, None, :], imp, -jnp.inf)
    _, idx = lax.top_k(imp, min(SEL_TOPK, n_blk))
    n_sel = idx.shape[-1]
    ks, vs = sel_fetch(idx)
    kpos = idx[..., None] * SEL_BLOCK + jnp.arange(SEL_BLOCK)
    smask = (kpos <= t[None, :, None, None, None]).reshape(B, T, N_KV_HEADS, 1, n_sel * SEL_BLOCK)
    s = jnp.einsum('bthgd,bthksd->bthgks', q, ks).reshape(B, T, N_KV_HEADS, GQA, n_sel * SEL_BLOCK) * scale
    p_sel = masked_softmax(s, smask)
    o_sel = jnp.einsum('bthgx,bthxd->bthgd', p_sel,
                       vs.reshape(B, T, N_KV_HEADS, n_sel * SEL_BLOCK, HEAD_DIM))
    dpos = t[:, None] - kw_pos[None, :]
    wmask = (dpos >= 0) & (dpos < WINDOW) & (kw_pos[None, :] >= 0)
    s = jnp.einsum('bthgd,bshd->bthgs', q, kw) * scale
    p_win = masked_softmax(s, wmask[None, :, None, None, :])
    o_win = jnp.einsum('bthgs,bshd->bthgd', p_win, vw)
    gate = jax.nn.sigmoid(g.astype(jnp.float32))
    o = gate[..., 0:1] * o_cmp + gate[..., 1:2] * o_sel + gate[..., 2:3] * o_win
    return o.astype(q.dtype).reshape(B, T, ATT_DIM)


def nsa_prompt(q, g, kvc, kvs, kvw, cmp_w):
    cmp_w1, cmp_pe, cmp_b1, cmp_w2 = cmp_w
    B, L = q.shape[:2]
    blocks = kvc.reshape(B, L // CMP_STRIDE, CMP_STRIDE, 2, N_KV_HEADS, HEAD_DIM)
    kc, vc = cmp_combine(cmp_project(blocks, cmp_w1, cmp_pe), (L - CMP_LEN) // CMP_STRIDE + 1, cmp_b1, cmp_w2)
    n_blk = L // SEL_BLOCK
    kb = kvs[:, :, 0].reshape(B, n_blk, SEL_BLOCK, N_KV_HEADS, HEAD_DIM)
    vb = kvs[:, :, 1].reshape(B, n_blk, SEL_BLOCK, N_KV_HEADS, HEAD_DIM)
    bidx = jnp.arange(B)[:, None, None, None]
    hidx = jnp.arange(N_KV_HEADS)[None, None, :, None]

    def sel_fetch(idx):
        return kb[bidx, idx, :, hidx], vb[bidx, idx, :, hidx]

    kvp = jnp.pad(kvw, ((0, 0), (WINDOW, 0), (0, 0), (0, 0), (0, 0)))

    def block(n):
        start = n * Q_BLOCK
        t = start + jnp.arange(Q_BLOCK)
        qb = lax.dynamic_slice_in_dim(q, start, Q_BLOCK, axis=1)
        gb = lax.dynamic_slice_in_dim(g, start, Q_BLOCK, axis=1)
        kw = lax.dynamic_slice_in_dim(kvp, start, WINDOW + Q_BLOCK, axis=1)
        kw_pos = start - WINDOW + jnp.arange(WINDOW + Q_BLOCK)
        return nsa_branches(qb, gb, t, kc, vc, sel_fetch, kw[:, :, 0], kw[:, :, 1], kw_pos, n_blk)

    o = lax.map(block, jnp.arange(L // Q_BLOCK))
    return jnp.moveaxis(o, 0, 1).reshape(B, L, ATT_DIM)


def nsa_sample(q, g, kvc, kvs, kvw, cache_cmp, cache_slc, win_buf, page_table, layer, cmp_w):
    cmp_w1, cmp_pe, cmp_b1, cmp_w2 = cmp_w
    DB, T = q.shape[:2]
    past = page_table.shape[1] * PAGE_SIZE
    L = past + T
    t = past + jnp.arange(T)
    bidx = jnp.arange(DB)[:, None, None, None]
    hidx = jnp.arange(N_KV_HEADS)[None, None, :, None]
    raw = cache_cmp[page_table, layer]
    past_blocks = raw.reshape(DB, past // CMP_STRIDE, CMP_STRIDE, 2, N_KV_HEADS, HEAD_DIM)
    n_new_c = -(-T // CMP_STRIDE)
    new_blocks = jnp.pad(kvc, ((0, 0), (0, n_new_c * CMP_STRIDE - T), (0, 0), (0, 0), (0, 0)))
    new_blocks = new_blocks.reshape(DB, n_new_c, CMP_STRIDE, 2, N_KV_HEADS, HEAD_DIM)
    proj = jnp.concatenate([cmp_project(past_blocks, cmp_w1, cmp_pe),
                            cmp_project(new_blocks.astype(past_blocks.dtype), cmp_w1, cmp_pe)], axis=2)
    kc, vc = cmp_combine(proj, (L - CMP_LEN) // CMP_STRIDE + 1, cmp_b1, cmp_w2)
    n_blk = -(-L // SEL_BLOCK)
    sub = PAGE_SIZE // SEL_BLOCK
    past_blk = past // SEL_BLOCK
    n_new_b = -(-T // SEL_BLOCK)
    new_sel = jnp.pad(kvs, ((0, 0), (0, n_new_b * SEL_BLOCK - T), (0, 0), (0, 0), (0, 0)))
    new_sel = new_sel.reshape(DB, n_new_b, SEL_BLOCK, 2, N_KV_HEADS, HEAD_DIM)
    pool = cache_slc.reshape(cache_slc.shape[0], cache_slc.shape[1], sub, SEL_BLOCK, 2, N_KV_HEADS, HEAD_DIM)

    def sel_fetch(idx):
        jp = jnp.clip(idx, 0, past_blk - 1)
        page = page_table[bidx, jp // sub]
        from_past = pool[page, layer, jp % sub, :, :, hidx]
        jn = jnp.clip(idx - past_blk, 0, n_new_b - 1)
        from_new = new_sel[bidx, jn, :, :, hidx]
        kv = jnp.where((idx >= past_blk)[..., None, None, None], from_new.astype(from_past.dtype), from_past)
        return kv[..., 0, :], kv[..., 1, :]

    w_buf = win_buf.shape[1]
    kvw_all = jnp.concatenate([win_buf, kvw.astype(win_buf.dtype)], axis=1)
    kw_pos = past - w_buf + jnp.arange(w_buf + T)
    o = nsa_branches(q, g, t, kc, vc, sel_fetch, kvw_all[:, :, 0], kvw_all[:, :, 1], kw_pos, n_blk)
    keep = min(WINDOW, w_buf + T)
    return o, kvw_all[:, w_buf + T - keep:]


def causal_conv(xbc, conv_state, w, b):
    T = xbc.shape[1]
    xp = jnp.concatenate([conv_state.astype(xbc.dtype), xbc], axis=1)
    y = sum(xp[:, k:k + T] * w[k] for k in range(CONV_W)) + b
    return jax.nn.silu(y), xp[:, T:]


def segsum(a):
    n = a.shape[-1]
    cs = jnp.cumsum(a, axis=-1)
    d = cs[..., :, None] - cs[..., None, :]
    return jnp.where(jnp.tril(jnp.ones((n, n), dtype=bool)), d, -jnp.inf)


def ssd(x, dt, a, bm, cm, h0, chunk):
    b, l, h, p = x.shape
    nc = l // chunk
    rep = h // bm.shape[2]
    bh = jnp.repeat(bm, rep, axis=2).reshape(b, nc, chunk, h, -1)
    ch = jnp.repeat(cm, rep, axis=2).reshape(b, nc, chunk, h, -1)
    xd = (x * dt[..., None]).reshape(b, nc, chunk, h, p)
    ad = (dt * a).reshape(b, nc, chunk, h).transpose(0, 3, 1, 2)
    acs = jnp.cumsum(ad, axis=-1)
    cb = jnp.einsum('bclhn,bcshn->bhcls', ch, bh)
    y_diag = jnp.einsum('bhcls,bcshp->bclhp', cb * jnp.exp(segsum(ad)), xd)
    decay = jnp.exp(acs[..., -1:] - acs)
    states = jnp.einsum('bclhn,bhcl,bclhp->bchpn', bh, decay, xd)
    states = jnp.concatenate([h0[:, None], states], axis=1)
    chunk_decay = jnp.exp(segsum(jnp.pad(acs[..., -1], ((0, 0), (0, 0), (1, 0)))))
    new_states = jnp.einsum('bhzc,bchpn->bzhpn', chunk_decay, states)
    y_off = jnp.einsum('bclhn,bchpn,bhcl->bclhp', ch, new_states[:, :-1], jnp.exp(acs))
    return (y_diag + y_off).reshape(b, l, h, p), new_states[:, -1]


def ssm_mix(z, xbc, dt_raw, conv_state, h0, chunk, ssm_w):
    conv_w, conv_b, dt_bias, a_log, d_skip, ssm_norm = ssm_w
    xbc_c, conv_new = causal_conv(xbc, conv_state, conv_w, conv_b)
    B, T = z.shape[:2]
    xs, bm, cm = jnp.split(xbc_c.astype(jnp.float32), [D_INNER, D_INNER + N_GROUPS * D_STATE], axis=-1)
    x = xs.reshape(B, T, SSM_HEADS, SSM_HEAD_DIM)
    bm = bm.reshape(B, T, N_GROUPS, D_STATE)
    cm = cm.reshape(B, T, N_GROUPS, D_STATE)
    dt = jax.nn.softplus(dt_raw.astype(jnp.float32) + dt_bias.astype(jnp.float32))
    a = -jnp.exp(a_log.astype(jnp.float32))
    y, h_final = ssd(x, dt, a, bm, cm, h0.astype(jnp.float32), chunk)
    y = (y + d_skip.astype(jnp.float32)[:, None] * x).reshape(B, T, D_INNER)
    y = grouped_rmsnorm(y * jax.nn.silu(z.astype(jnp.float32)), ssm_norm)
    return y.astype(z.dtype), conv_new, h_final.astype(h0.dtype)


def mixer_prompt(h, w_in, cmp_w, ssm_w, w_out):
    B, L = h.shape[:2]
    q, g, kvc, kvs, kvw, z, xbc, dt = split_proj(h, w_in)
    att = nsa_prompt(q, g, kvc, kvs, kvw, cmp_w)
    conv0 = jnp.zeros((B, CONV_W - 1, CONV_DIM), h.dtype)
    h0 = jnp.zeros((B, SSM_HEADS, SSM_HEAD_DIM, D_STATE), h.dtype)
    ssm, conv_new, h_new = ssm_mix(z, xbc, dt, conv0, h0, min(SSD_CHUNK, L), ssm_w)
    out = jnp.concatenate([att, ssm], axis=-1) @ w_out
    keep = min(WINDOW, L)
    return out, (kvc, kvs, kvw[:, L - keep:], conv_new, h_new)


def mixer_sample(h, cache_cmp, cache_slc, win_buf, conv_state, ssm_state, page_table, layer, w_in, cmp_w, ssm_w, w_out):
    T = h.shape[1]
    q, g, kvc, kvs, kvw, z, xbc, dt = split_proj(h, w_in)
    att, new_win = nsa_sample(q, g, kvc, kvs, kvw, cache_cmp, cache_slc, win_buf, page_table, layer, cmp_w)
    ssm, conv_new, h_new = ssm_mix(z, xbc, dt, conv_state, ssm_state, T, ssm_w)
    out = jnp.concatenate([att, ssm], axis=-1) @ w_out
    return out, (kvc, kvs, new_win, conv_new, h_new)


def ple_add(h, p, g, w_gate, w_proj):
    gate = jax.nn.sigmoid(rmsnorm(h, g) @ w_gate)
    return h + gate * (p @ w_proj)


def setup_inputs(seed: int = 0) -> dict:
    key = jax.random.key(seed)
    keys = iter(jax.random.split(key, 48))
    f32 = jnp.float32

    def normal(shape, scale=1.0):
        return jax.random.normal(next(keys), shape, f32) * scale

    def gain(shape):
        return 1.0 + normal(shape, 0.01)

    n_pages = PAST_LEN // PAGE_SIZE
    n_pool = (5 * DEC_BATCH * n_pages + 3) // 4
    w_buf = min(WINDOW, PAST_LEN)
    page_table = jax.random.permutation(next(keys), n_pool)[:DEC_BATCH * n_pages]
    page_table = page_table.reshape(DEC_BATCH, n_pages).astype(jnp.int32)
    u = jax.random.uniform(next(keys), (DEPTH, SSM_HEADS), f32)
    dt = jnp.exp(u * (math.log(DT_MAX) - math.log(DT_MIN)) + math.log(DT_MIN))
    dt_bias = dt + jnp.log(-jnp.expm1(-dt))
    a_log = jnp.log(jax.random.uniform(next(keys), (DEPTH, SSM_HEADS), f32, 1.0, 16.0))
    kv_page = (n_pool, DEPTH, PAGE_SIZE, 2, N_KV_HEADS, HEAD_DIM)
    return {
        'x_prompt': normal((BATCH, SEQ, D_MODEL)),
        'x_sample': normal((DEC_BATCH, DEC_SEQ, D_MODEL)),
        'cache_cmp_kv': normal(kv_page),
        'cache_slc_kv': normal(kv_page),
        'cache_win_kv': normal((DEC_BATCH, DEPTH, w_buf, 2, N_KV_HEADS, HEAD_DIM)),
        'state_conv': normal((DEC_BATCH, DEPTH, CONV_W - 1, CONV_DIM)),
        'state_ssm': normal((DEC_BATCH, DEPTH, SSM_HEADS, SSM_HEAD_DIM, D_STATE), 0.5),
        'page_table': page_table,
        'p_prompt': normal((DEPTH, BATCH, SEQ, PLE_DIM)),
        'p_sample': normal((DEPTH, DEC_BATCH, DEC_SEQ, PLE_DIM)),
        'ffn1_norm': gain((DEPTH, D_MODEL)),
        'ffn1_w_gate': normal((DEPTH, D_MODEL, D_FF), D_MODEL ** -0.5),
        'ffn1_w_up': normal((DEPTH, D_MODEL, D_FF), D_MODEL ** -0.5),
        'ffn1_w_down': normal((DEPTH, D_FF, D_MODEL), D_FF ** -0.5),
        'mix_norm': gain((DEPTH, D_MODEL)),
        'w_in': normal((DEPTH, D_MODEL, D_IN_PROJ), D_MODEL ** -0.5),
        'cmp_w1': normal((DEPTH, 2, CMP_LEN, HEAD_DIM, CMP_HID), (CMP_LEN * HEAD_DIM) ** -0.5),
        'cmp_pe': normal((DEPTH, 2, CMP_LEN, HEAD_DIM), 0.1),
        'cmp_b1': normal((DEPTH, 2, CMP_HID), 0.01),
        'cmp_w2': normal((DEPTH, 2, CMP_HID, HEAD_DIM), CMP_HID ** -0.5),
        'conv_w': normal((DEPTH, CONV_W, CONV_DIM), CONV_W ** -0.5),
        'conv_b': normal((DEPTH, CONV_DIM), 0.01),
        'dt_bias': dt_bias,
        'a_log': a_log,
        'd_skip': gain((DEPTH, SSM_HEADS)),
        'ssm_norm': gain((DEPTH, D_INNER)),
        'w_out': normal((DEPTH, MIX_DIM, D_MODEL), MIX_DIM ** -0.5),
        'ffn2_norm': gain((DEPTH, D_MODEL)),
        'ffn2_w_gate': normal((DEPTH, D_MODEL, D_FF), D_MODEL ** -0.5),
        'ffn2_w_up': normal((DEPTH, D_MODEL, D_FF), D_MODEL ** -0.5),
        'ffn2_w_down': normal((DEPTH, D_FF, D_MODEL), D_FF ** -0.5),
        'ple_norm': gain((DEPTH, D_MODEL)),
        'w_ple_gate': normal((DEPTH, D_MODEL, D_MODEL), D_MODEL ** -0.5),
        'w_ple': normal((DEPTH, PLE_DIM, D_MODEL), PLE_DIM ** -0.5),
        'final_norm': gain((D_MODEL,)),
    }


def reference(x_prompt, x_sample, cache_cmp_kv, cache_slc_kv, cache_win_kv, state_conv, state_ssm, page_table,
              p_prompt, p_sample, ffn1_norm, ffn1_w_gate, ffn1_w_up, ffn1_w_down, mix_norm, w_in,
              cmp_w1, cmp_pe, cmp_b1, cmp_w2, conv_w, conv_b, dt_bias, a_log, d_skip, ssm_norm, w_out,
              ffn2_norm, ffn2_w_gate, ffn2_w_up, ffn2_w_down, ple_norm, w_ple_gate, w_ple, final_norm):
    hp, hs = x_prompt, x_sample
    st_p = ([], [], [], [], [])
    st_s = ([], [], [], [], [])
    for i in range(DEPTH):
        cmp_w = (cmp_w1[i], cmp_pe[i], cmp_b1[i], cmp_w2[i])
        ssm_w = (conv_w[i], conv_b[i], dt_bias[i], a_log[i], d_skip[i], ssm_norm[i])
        hp = hp + 0.5 * swiglu(rmsnorm(hp, ffn1_norm[i]), ffn1_w_gate[i], ffn1_w_up[i], ffn1_w_down[i])
        hs = hs + 0.5 * swiglu(rmsnorm(hs, ffn1_norm[i]), ffn1_w_gate[i], ffn1_w_up[i], ffn1_w_down[i])
        mp, new_p = mixer_prompt(rmsnorm(hp, mix_norm[i]), w_in[i], cmp_w, ssm_w, w_out[i])
        ms, new_s = mixer_sample(rmsnorm(hs, mix_norm[i]), cache_cmp_kv, cache_slc_kv, cache_win_kv[:, i],
                                 state_conv[:, i], state_ssm[:, i], page_table, i, w_in[i], cmp_w, ssm_w, w_out[i])
        hp = hp + mp
        hs = hs + ms
        for lst, arr in zip(st_p, new_p):
            lst.append(arr)
        for lst, arr in zip(st_s, new_s):
            lst.append(arr)
        hp = hp + 0.5 * swiglu(rmsnorm(hp, ffn2_norm[i]), ffn2_w_gate[i], ffn2_w_up[i], ffn2_w_down[i])
        hs = hs + 0.5 * swiglu(rmsnorm(hs, ffn2_norm[i]), ffn2_w_gate[i], ffn2_w_up[i], ffn2_w_down[i])
        hp = ple_add(hp, p_prompt[i], ple_norm[i], w_ple_gate[i], w_ple[i])
        hs = ple_add(hs, p_sample[i], ple_norm[i], w_ple_gate[i], w_ple[i])
    y_prompt = rmsnorm(hp, final_norm)
    y_sample = rmsnorm(hs, final_norm)
    new_cmp_kv_prompt = jnp.stack(st_p[0], axis=1)
    new_slc_kv_prompt = jnp.stack(st_p[1], axis=1)
    new_win_kv_prompt = jnp.stack(st_p[2], axis=1)
    new_conv_prompt = jnp.stack(st_p[3], axis=1)
    new_ssm_prompt = jnp.stack(st_p[4], axis=1)
    new_cmp_kv_sample = jnp.stack(st_s[0], axis=1)
    new_slc_kv_sample = jnp.stack(st_s[1], axis=1)
    new_win_kv_sample = jnp.stack(st_s[2], axis=1)
    new_conv_sample = jnp.stack(st_s[3], axis=1)
    new_ssm_sample = jnp.stack(st_s[4], axis=1)
    return (y_prompt, y_sample, new_cmp_kv_prompt, new_slc_kv_prompt, new_win_kv_prompt, new_conv_prompt,
            new_ssm_prompt, new_cmp_kv_sample, new_slc_kv_sample, new_win_kv_sample, new_conv_sample,
            new_ssm_sample)
```

```python
import functools

import jax
import jax.numpy as jnp
import numpy as np
from jax import lax
from jax.experimental import pallas as pl
from jax.experimental.pallas import tpu as pltpu

F32 = jnp.float32
BF16 = jnp.bfloat16

D_MODEL = 1024
N_HEADS = 8
N_KV_HEADS = 2
HEAD_DIM = 64
GQA = N_HEADS // N_KV_HEADS
ATT_DIM = N_HEADS * HEAD_DIM
KV_DIM = 2 * N_KV_HEADS * HEAD_DIM
CMP_LEN = 32
CMP_STRIDE = 16
CMP_HID = 128
SEL_BLOCK = 64
SEL_TOPK = 16
WINDOW = 512
Q_BLOCK = 128
SSM_HEADS = 8
SSM_HEAD_DIM = 64
D_INNER = SSM_HEADS * SSM_HEAD_DIM
N_GROUPS = 2
D_STATE = 128
CONV_W = 4
CONV_DIM = D_INNER + 2 * N_GROUPS * D_STATE
SSD_CHUNK = 128
D_FF = 2816
PLE_DIM = 256
PAGE_SIZE = 128
EPS = 1e-6

LANES = 128
FF_CHUNK = 256
N_FF_CHUNKS = D_FF // FF_CHUNK
SMALL_W = LANES
N_GATE = 3 * N_HEADS
IN_PROJ_PAD = ATT_DIM + 3 * KV_DIM + D_INNER + CONV_DIM + SMALL_W
VMEM_LIMIT = 56 * 1024 * 1024
NEG_BIG = -1e30
IMP_FORCED = 3e38
IMP_BLOCKED = -1e38
IMP_TAKEN = -3e38


def _dot(a, b):
    return jnp.dot(a, b, preferred_element_type=F32)


def _dot_nt(a, b):
    return lax.dot_general(a, b, (((1,), (1,)), ((), ())), preferred_element_type=F32)


def _rms(x, g):
    return x * lax.rsqrt(jnp.mean(x * x, axis=-1, keepdims=True) + EPS) * g


def _silu(x):
    return x * jax.nn.sigmoid(x)


def _const_spec(shape):
    nd = len(shape)
    return pl.BlockSpec(shape, lambda *_: (0,) * nd, pipeline_mode=pl.Buffered(1))


def _row_spec(tm, width):
    return pl.BlockSpec((tm, width), lambda i: (i, 0))


def _swiglu_half_step(x, g_ref, wg_ref, wu_ref, wd_ref, acc_ref):
    xn = _rms(x, g_ref[...]).astype(BF16)
    acc_ref[...] = jnp.zeros_like(acc_ref)

    def body(c, carry):
        a = _dot(xn, wg_ref[c])
        b = _dot(xn, wu_ref[c])
        hm = (_silu(a) * b).astype(BF16)
        acc_ref[...] += _dot(hm, wd_ref[c])
        return carry

    lax.fori_loop(0, N_FF_CHUNKS, body, 0)
    return x + 0.5 * acc_ref[...]


_IN_SEGS = (ATT_DIM, KV_DIM, KV_DIM, KV_DIM, D_INNER, CONV_DIM, SMALL_W)


def _ffn_inproj_kernel(x_ref, g1_ref, wg_ref, wu_ref, wd_ref, gm_ref, win_ref,
                       h_ref, q_ref, kvc_ref, kvs_ref, kvw_ref, z_ref, xbc_ref, small_ref, acc_ref):
    h = _swiglu_half_step(x_ref[...], g1_ref, wg_ref, wu_ref, wd_ref, acc_ref)
    h_ref[...] = h
    hn = _rms(h, gm_ref[...]).astype(BF16)
    off = 0
    for out_ref, width in zip((q_ref, kvc_ref, kvs_ref, kvw_ref, z_ref, xbc_ref, small_ref), _IN_SEGS):
        out_ref[...] = _dot(hn, win_ref[:, off:off + width])
        off += width


def _ffn_inproj(x, g1, wg, wu, wd, gm, win, tm):
    n = x.shape[0]
    outs = [jax.ShapeDtypeStruct((n, D_MODEL), F32)] + [jax.ShapeDtypeStruct((n, w), F32) for w in _IN_SEGS]
    return pl.pallas_call(
        _ffn_inproj_kernel,
        grid=(n // tm,),
        in_specs=[_row_spec(tm, D_MODEL), _const_spec(g1.shape), _const_spec(wg.shape), _const_spec(wu.shape),
                  _const_spec(wd.shape), _const_spec(gm.shape), _const_spec(win.shape)],
        out_specs=[_row_spec(tm, D_MODEL)] + [_row_spec(tm, w) for w in _IN_SEGS],
        out_shape=outs,
        scratch_shapes=[pltpu.VMEM((tm, D_MODEL), F32)],
        compiler_params=pltpu.CompilerParams(dimension_semantics=("arbitrary",), vmem_limit_bytes=VMEM_LIMIT),
        name="ffn1_inproj",
    )(x, g1, wg, wu, wd, gm, win)


CMP_IN = CMP_STRIDE * KV_DIM
CMP_OUT = 2 * N_KV_HEADS * 2 * CMP_HID


def _cmp_weights(cmp_w1, cmp_pe, cmp_b1, cmp_w2):
    r = CMP_LEN // CMP_STRIDE
    w1r = cmp_w1.reshape(2, r, CMP_STRIDE, HEAD_DIM, CMP_HID)
    per = cmp_pe.reshape(2, r, CMP_STRIDE, HEAD_DIM)
    wbig = jnp.zeros((CMP_STRIDE, 2, N_KV_HEADS, HEAD_DIM, 2, N_KV_HEADS, r, CMP_HID), F32)
    for c in range(2):
        blk = jnp.transpose(w1r[c], (1, 2, 0, 3))
        for h in range(N_KV_HEADS):
            wbig = wbig.at[:, c, h, :, c, h, :, :].set(blk)
    wbig = wbig.reshape(CMP_IN, CMP_OUT).astype(BF16)
    pe_rows = jnp.transpose(per, (1, 2, 0, 3))
    pe_rows = jnp.broadcast_to(pe_rows[:, :, :, None, :], (r, CMP_STRIDE, 2, N_KV_HEADS, HEAD_DIM))
    pe_rows = jnp.concatenate([pe_rows.reshape(r, CMP_IN), jnp.zeros((8 - r, CMP_IN), F32)], axis=0)
    return wbig, pe_rows, cmp_b1, cmp_w2.astype(BF16)


def _cmp_tokens(x_bf, wbig_ref, pe_ref, b1_ref, w2_ref):
    nb = x_bf.shape[0]
    proj = _dot(x_bf, wbig_ref[...])
    pe_proj = _dot(pe_ref[...].astype(BF16), wbig_ref[...])
    outs = []
    for ch in range(2 * N_KV_HEADS):
        c = ch // N_KV_HEADS
        lo = ch * 2 * CMP_HID
        p0 = proj[:, lo:lo + CMP_HID]
        p1 = pltpu.roll(proj[:, lo + CMP_HID:lo + 2 * CMP_HID], nb - 1, 0)
        pe_add = pe_proj[0:1, lo:lo + CMP_HID] + pe_proj[1:2, lo + CMP_HID:lo + 2 * CMP_HID]
        hid = _silu(p0 + p1 + pe_add + b1_ref[c:c + 1, :])
        outs.append(_dot(hid.astype(BF16), w2_ref[c]))
    return jnp.concatenate(outs, axis=1)


def _cmp_prompt_kernel(x_ref, wbig_ref, pe_ref, b1_ref, w2_ref, out_ref):
    out_ref[0] = _cmp_tokens(x_ref[0].astype(BF16), wbig_ref, pe_ref, b1_ref, w2_ref)


def _cmp_prompt(kvc_blocks, wbig, pe_rows, b1, w2):
    b, nb, _ = kvc_blocks.shape
    return pl.pallas_call(
        _cmp_prompt_kernel,
        grid=(b,),
        in_specs=[pl.BlockSpec((1, nb, CMP_IN), lambda i: (i, 0, 0)), _const_spec(wbig.shape),
                  _const_spec(pe_rows.shape), _const_spec(b1.shape), _const_spec(w2.shape)],
        out_specs=pl.BlockSpec((1, nb, KV_DIM), lambda i: (i, 0, 0)),
        out_shape=jax.ShapeDtypeStruct((b, nb, KV_DIM), F32),
        compiler_params=pltpu.CompilerParams(dimension_semantics=("arbitrary",), vmem_limit_bytes=VMEM_LIMIT),
        name="nsa_cmp_prompt",
    )(kvc_blocks, wbig, pe_rows, b1, w2)


SEL_TILE = 256
ROWS = GQA * Q_BLOCK
SEL_LANES = 128
AUG = SEL_LANES + 2 * HEAD_DIM
TINY = float(np.finfo(np.float32).tiny)
SEL_SHIFT = SEL_BLOCK.bit_length() - 1


def _softmax_rows(s):
    m = jnp.max(s, axis=-1, keepdims=True)
    m = jnp.where(m > 0.5 * NEG_BIG, m, 0.0)
    e = jnp.exp(s - m)
    return e / jnp.maximum(jnp.sum(e, axis=-1, keepdims=True), TINY)


def _select_blocks(imp_t, t_lane):
    jj = lax.broadcasted_iota(jnp.int32, imp_t.shape, 0)
    cb = t_lane >> SEL_SHIFT
    forced = (jj == 0) | (jj == cb) | (jj == cb - 1)
    causal = (jj << SEL_SHIFT) <= t_lane
    x = jnp.where(forced, IMP_FORCED, imp_t)
    x = jnp.where(causal, x, IMP_BLOCKED)
    sel, _ = _take_top(x, SEL_TOPK)
    return sel


def _take_top(x, k):
    nj = x.shape[0]
    jf = lax.broadcasted_iota(jnp.int32, x.shape, 0).astype(F32)
    sel = jnp.zeros_like(x)
    picks = []
    for _ in range(k):
        m = jnp.max(x, axis=0, keepdims=True)
        first = jnp.min(jnp.where(x == m, jf, float(nj)), axis=0, keepdims=True)
        hit = jf == first
        sel = jnp.where(hit, 1.0, sel)
        x = jnp.where(hit, IMP_TAKEN, x)
        picks.append(first)
    return sel, picks


def _nsa_prompt_kernel(q_ref, small_ref, kcv_ref, kaug_ref, vaug_ref, kwin_ref, mselt_ref, out_ref,
                       lhs_ref, m_ref, acc_ref):
    n = pl.program_id(1)
    t0 = n * Q_BLOCK
    nb = kcv_ref.shape[1]
    row_t = t0 + (lax.broadcasted_iota(jnp.int32, (ROWS, 1), 0) & (Q_BLOCK - 1))
    lane_t = t0 + lax.broadcasted_iota(jnp.int32, (1, Q_BLOCK), 1)
    gates = jax.nn.sigmoid(small_ref[0][:, :N_GATE])
    scale = HEAD_DIM ** -0.5
    lhs_ref[:, SEL_LANES + HEAD_DIM:] = jnp.zeros((ROWS, HEAD_DIM), BF16)
    n_tiles = (n + 2) // 2

    for h in range(N_KV_HEADS):
        for g in range(GQA):
            col = (h * GQA + g) * HEAD_DIM
            lhs_ref[g * Q_BLOCK:(g + 1) * Q_BLOCK, SEL_LANES:SEL_LANES + HEAD_DIM] = (
                q_ref[0, :, col:col + HEAD_DIM] * scale).astype(BF16)
        qr = lhs_ref[:, SEL_LANES:SEL_LANES + HEAD_DIM]

        kc = kcv_ref[0, :, h * HEAD_DIM:(h + 1) * HEAD_DIM].astype(BF16)
        vc = kcv_ref[0, :, (N_KV_HEADS + h) * HEAD_DIM:(N_KV_HEADS + h + 1) * HEAD_DIM].astype(BF16)
        s = _dot_nt(qr, kc)
        c_end = lax.broadcasted_iota(jnp.int32, (1, nb), 1) * CMP_STRIDE + (CMP_LEN - 1)
        p = _softmax_rows(jnp.where(c_end <= row_t, s, NEG_BIG))
        o_cmp = _dot(p.astype(BF16), vc)

        psum = p[0:Q_BLOCK]
        for g in range(1, GQA):
            psum = psum + p[g * Q_BLOCK:(g + 1) * Q_BLOCK]
        p_hi = psum.astype(BF16)
        p_lo = (psum - p_hi.astype(F32)).astype(BF16)
        imp_t = _dot_nt(mselt_ref[...], p_hi) + _dot_nt(mselt_ref[...], p_lo)
        sel = _select_blocks(imp_t, lane_t)
        neg = jnp.where(sel > 0.0, 0.0, NEG_BIG).T.astype(BF16)
        for g in range(GQA):
            lhs_ref[g * Q_BLOCK:(g + 1) * Q_BLOCK, 0:SEL_LANES] = neg

        m_ref[...] = jnp.full(m_ref.shape, NEG_BIG, F32)
        acc_ref[...] = jnp.zeros(acc_ref.shape, F32)

        def sel_step(kt, causal_mask, h=h):
            k0 = pl.multiple_of(kt * SEL_TILE, SEL_TILE)
            s = _dot_nt(lhs_ref[...], kaug_ref[0, h, pl.ds(k0, SEL_TILE), :])
            if causal_mask:
                kpos = k0 + lax.broadcasted_iota(jnp.int32, (1, SEL_TILE), 1)
                s = jnp.where(kpos <= row_t, s, NEG_BIG)
            m_old = m_ref[...]
            m_new = jnp.maximum(m_old, jnp.max(s, axis=-1, keepdims=True))
            pe = jnp.exp(s - m_new).astype(BF16)
            acc_ref[...] = jnp.exp(m_old - m_new) * acc_ref[...] + _dot(pe, vaug_ref[0, h, pl.ds(k0, SEL_TILE), :])
            m_ref[...] = m_new

        def sel_body(kt, carry):
            sel_step(kt, False)
            return carry

        lax.fori_loop(0, n_tiles - 1, sel_body, 0)
        sel_step(n_tiles - 1, True)
        acc = acc_ref[...]
        o_sel = acc[:, :HEAD_DIM] / jnp.maximum(acc[:, HEAD_DIM:HEAD_DIM + 1], TINY)

        w0 = pl.multiple_of(t0, Q_BLOCK)
        kw = kwin_ref[0, pl.ds(w0, WINDOW + Q_BLOCK), h * HEAD_DIM:(h + 1) * HEAD_DIM]
        vw = kwin_ref[0, pl.ds(w0, WINDOW + Q_BLOCK), (N_KV_HEADS + h) * HEAD_DIM:(N_KV_HEADS + h + 1) * HEAD_DIM]
        s = _dot_nt(qr, kw)
        kpos = t0 - WINDOW + lax.broadcasted_iota(jnp.int32, (1, WINDOW + Q_BLOCK), 1)
        dpos = row_t - kpos
        p = _softmax_rows(jnp.where((dpos >= 0) & (dpos < WINDOW) & (kpos >= 0), s, NEG_BIG))
        o_win = _dot(p.astype(BF16), vw)

        for g in range(GQA):
            hd = h * GQA + g
            r = slice(g * Q_BLOCK, (g + 1) * Q_BLOCK)
            out_ref[0, :, hd * HEAD_DIM:(hd + 1) * HEAD_DIM] = (
                gates[:, 3 * hd:3 * hd + 1] * o_cmp[r] + gates[:, 3 * hd + 1:3 * hd + 2] * o_sel[r]
                + gates[:, 3 * hd + 2:3 * hd + 3] * o_win[r])


def _cmp_to_sel_t(nb, n_cmp, n_blk):
    cs = np.arange(nb)[None, :] * CMP_STRIDE
    bs = np.arange(SEL_LANES)[:, None] * SEL_BLOCK
    ov = np.clip(np.minimum(cs + CMP_LEN, bs + SEL_BLOCK) - np.maximum(cs, bs), 0, None) / CMP_LEN
    ov = ov * (np.arange(nb)[None, :] < n_cmp) * (np.arange(SEL_LANES)[:, None] < n_blk)
    return jnp.asarray(ov, BF16)


def _nsa_prompt(q, small, kcv, kvs, kvw):
    b, l, _ = q.shape
    nb = kcv.shape[1]
    n_blk = l // SEL_BLOCK
    assert l % SEL_TILE == 0 and n_blk <= SEL_LANES and l >= CMP_LEN
    kvs5 = kvs.reshape(b, l, 2, N_KV_HEADS, HEAD_DIM)
    onehot = jnp.asarray(np.arange(l)[:, None] // SEL_BLOCK == np.arange(SEL_LANES)[None, :], BF16)
    k_sel = jnp.transpose(kvs5[:, :, 0], (0, 2, 1, 3)).astype(BF16)
    v_sel = jnp.transpose(kvs5[:, :, 1], (0, 2, 1, 3)).astype(BF16)
    kaug = jnp.concatenate([jnp.broadcast_to(onehot, (b, N_KV_HEADS, l, SEL_LANES)), k_sel,
                            jnp.zeros((b, N_KV_HEADS, l, HEAD_DIM), BF16)], axis=-1)
    vaug = jnp.concatenate([v_sel, jnp.ones((b, N_KV_HEADS, l, 1), BF16),
                            jnp.zeros((b, N_KV_HEADS, l, LANES - HEAD_DIM - 1), BF16)], axis=-1)
    kwin = jnp.pad(kvw.astype(BF16), ((0, 0), (WINDOW, 0), (0, 0)))
    mselt = _cmp_to_sel_t(nb, (l - CMP_LEN) // CMP_STRIDE + 1, n_blk)

    def per_batch(shape):
        nd = len(shape)
        return pl.BlockSpec((1,) + shape[1:], lambda i, j: (i,) + (0,) * (nd - 1), pipeline_mode=pl.Buffered(1))

    return pl.pallas_call(
        _nsa_prompt_kernel,
        grid=(b, l // Q_BLOCK),
        in_specs=[pl.BlockSpec((1, Q_BLOCK, ATT_DIM), lambda i, j: (i, j, 0)),
                  pl.BlockSpec((1, Q_BLOCK, SMALL_W), lambda i, j: (i, j, 0)),
                  per_batch(kcv.shape), per_batch(kaug.shape), per_batch(vaug.shape), per_batch(kwin.shape),
                  pl.BlockSpec(mselt.shape, lambda i, j: (0, 0), pipeline_mode=pl.Buffered(1))],
        out_specs=pl.BlockSpec((1, Q_BLOCK, ATT_DIM), lambda i, j: (i, j, 0)),
        out_shape=jax.ShapeDtypeStruct((b, l, ATT_DIM), F32),
        scratch_shapes=[pltpu.VMEM((ROWS, AUG), BF16), pltpu.VMEM((ROWS, 1), F32), pltpu.VMEM((ROWS, LANES), F32)],
        compiler_params=pltpu.CompilerParams(dimension_semantics=("arbitrary", "arbitrary"),
                                             vmem_limit_bytes=VMEM_LIMIT),
        name="nsa_prompt",
    )(q, small, kcv, kaug, vaug, kwin, mselt)


DT_LANE = N_GATE
HEADS_PER_GROUP = SSM_HEADS // N_GROUPS
GROUP_W = D_INNER // N_GROUPS
TAIL = 8


def _expand_heads(v):
    rows = v.shape[0]
    lane = lax.broadcasted_iota(jnp.int32, (rows, LANES), 1)
    tiles = []
    for j in range(D_INNER // LANES):
        a = jnp.broadcast_to(v[:, DT_LANE + 2 * j:DT_LANE + 2 * j + 1], (rows, LANES))
        b = jnp.broadcast_to(v[:, DT_LANE + 2 * j + 1:DT_LANE + 2 * j + 2], (rows, LANES))
        tiles.append(jnp.where(lane < SSM_HEAD_DIM, a, b))
    return jnp.concatenate(tiles, axis=1)


def _cumsum_rows(x):
    n = x.shape[0]
    row = lax.broadcasted_iota(jnp.int32, x.shape, 0)
    s = 1
    while s < n:
        x = x + jnp.where(row >= s, pltpu.roll(x, s, 0), 0.0)
        s *= 2
    return x


def _grouped_norm_gate(y, z, norm):
    v = y * _silu(z)
    outs = []
    for g in range(N_GROUPS):
        vg = v[:, g * GROUP_W:(g + 1) * GROUP_W]
        outs.append(vg * lax.rsqrt(jnp.mean(vg * vg, axis=-1, keepdims=True) + EPS))
    return jnp.concatenate(outs, axis=1) * norm


def _ssd_prompt_kernel(xbc_ref, z_ref, small_ref, convw_ref, convb_ref, dtb_ref, alog_ref, dskip_ref, norm_ref,
                       y_ref, state_ref, xe_ref, st_ref, yd_ref):
    c = pl.program_id(1)
    lc = xbc_ref.shape[1]

    @pl.when(c == 0)
    def _():
        xe_ref[0:TAIL, :] = jnp.zeros((TAIL, CONV_DIM), F32)
        st_ref[...] = jnp.zeros(st_ref.shape, F32)

    xe_ref[TAIL:TAIL + lc, :] = xbc_ref[0]
    conv = convb_ref[...] + xe_ref[TAIL:TAIL + lc, :] * convw_ref[CONV_W - 1:CONV_W, :]
    for k in range(CONV_W - 1):
        conv = conv + xe_ref[pl.ds(TAIL - (CONV_W - 1) + k, lc), :] * convw_ref[k:k + 1, :]
    xe_ref[0:TAIL, :] = xe_ref[lc:lc + TAIL, :]
    xc = _silu(conv)
    xs = xc[:, :D_INNER]

    dt = jax.nn.softplus(small_ref[0] + dtb_ref[...])
    ad = dt * (-jnp.exp(alog_ref[...]))
    acs = _cumsum_rows(ad)
    acs_t = acs.T
    dt_e = _expand_heads(dt)
    acs_e = _expand_heads(acs)
    last_e = acs_e[lc - 1:lc, :]
    xd = xs * dt_e
    xd_bf = xd.astype(BF16)
    xdd_bf = (xd * jnp.exp(last_e - acs_e)).astype(BF16)
    grow = jnp.exp(acs_e)
    li = lax.broadcasted_iota(jnp.int32, (lc, lc), 0)
    si = lax.broadcasted_iota(jnp.int32, (lc, lc), 1)

    y_off = []
    for g in range(N_GROUPS):
        bm = xc[:, D_INNER + g * D_STATE:D_INNER + (g + 1) * D_STATE]
        cm = xc[:, D_INNER + N_GROUPS * D_STATE + g * D_STATE:D_INNER + N_GROUPS * D_STATE + (g + 1) * D_STATE]
        bm_bf = bm.astype(BF16)
        cm_bf = cm.astype(BF16)
        cb = _dot_nt(cm_bf, bm_bf)
        cols = slice(g * GROUP_W, (g + 1) * GROUP_W)
        st_g = st_ref[:, cols]
        y_off.append(_dot(cm_bf, st_g.astype(BF16)))
        for hh in range(HEADS_PER_GROUP):
            h = g * HEADS_PER_GROUP + hh
            seg = acs[:, DT_LANE + h:DT_LANE + h + 1] - acs_t[DT_LANE + h:DT_LANE + h + 1, :]
            m = jnp.where(li >= si, cb * jnp.exp(seg), 0.0).astype(BF16)
            yd_ref[:, h * SSM_HEAD_DIM:(h + 1) * SSM_HEAD_DIM] = _dot(m, xd_bf[:, h * SSM_HEAD_DIM:(h + 1) * SSM_HEAD_DIM])
        st_ref[:, cols] = st_g * jnp.exp(last_e[:, cols]) + _dot(bm.T.astype(BF16), xdd_bf[:, cols])

    y = yd_ref[...] + jnp.concatenate(y_off, axis=1) * grow + dskip_ref[...] * xs
    y_ref[0] = _grouped_norm_gate(y, z_ref[0], norm_ref[...])

    @pl.when(c == pl.num_programs(1) - 1)
    def _():
        state_ref[0] = st_ref[...].T


def _pad_small(v):
    return jnp.zeros((1, SMALL_W), F32).at[0, DT_LANE:DT_LANE + SSM_HEADS].set(v.astype(F32))


def _ssd_prompt(xbc, z, small, conv_w, conv_b, dt_bias, a_log, d_skip, ssm_norm):
    b, l, _ = xbc.shape
    lc = min(SSD_CHUNK, l)
    params = (conv_w, conv_b.reshape(1, CONV_DIM), _pad_small(dt_bias), _pad_small(a_log),
              jnp.repeat(d_skip.astype(F32), SSM_HEAD_DIM).reshape(1, D_INNER), ssm_norm.reshape(1, D_INNER))
    tile = lambda w: pl.BlockSpec((1, lc, w), lambda i, j: (i, j, 0))
    return pl.pallas_call(
        _ssd_prompt_kernel,
        grid=(b, l // lc),
        in_specs=[tile(CONV_DIM), tile(D_INNER), tile(SMALL_W)] + [_const_spec(p.shape) for p in params],
        out_specs=[tile(D_INNER), pl.BlockSpec((1, D_INNER, D_STATE), lambda i, j: (i, 0, 0))],
        out_shape=[jax.ShapeDtypeStruct((b, l, D_INNER), F32), jax.ShapeDtypeStruct((b, D_INNER, D_STATE), F32)],
        scratch_shapes=[pltpu.VMEM((lc + TAIL, CONV_DIM), F32), pltpu.VMEM((D_STATE, D_INNER), F32),
                        pltpu.VMEM((lc, D_INNER), F32)],
        compiler_params=pltpu.CompilerParams(dimension_semantics=("arbitrary", "arbitrary"),
                                             vmem_limit_bytes=VMEM_LIMIT),
        name="ssd_prompt",
    )(xbc, z, small, *params)


def _out_ffn_kernel(h_ref, att_ref, ssm_ref, p_ref, woa_ref, wos_ref, g2_ref, wg_ref, wu_ref, wd_ref,
                    gp_ref, wpg_ref, wple_ref, gf_ref, y_ref, acc_ref):
    h = h_ref[...] + _dot(att_ref[...].astype(BF16), woa_ref[...]) + _dot(ssm_ref[...].astype(BF16), wos_ref[...])
    h = _swiglu_half_step(h, g2_ref, wg_ref, wu_ref, wd_ref, acc_ref)
    gate = jax.nn.sigmoid(_dot(_rms(h, gp_ref[...]).astype(BF16), wpg_ref[...]))
    h = h + gate * _dot(p_ref[...].astype(BF16), wple_ref[...])
    y_ref[...] = _rms(h, gf_ref[...])


def _out_ffn(h, att, ssm, p, weights, tm):
    n = h.shape[0]
    return pl.pallas_call(
        _out_ffn_kernel,
        grid=(n // tm,),
        in_specs=[_row_spec(tm, D_MODEL), _row_spec(tm, ATT_DIM), _row_spec(tm, D_INNER), _row_spec(tm, PLE_DIM)]
        + [_const_spec(w.shape) for w in weights],
        out_specs=_row_spec(tm, D_MODEL),
        out_shape=jax.ShapeDtypeStruct((n, D_MODEL), F32),
        scratch_shapes=[pltpu.VMEM((tm, D_MODEL), F32)],
        compiler_params=pltpu.CompilerParams(dimension_semantics=("arbitrary",), vmem_limit_bytes=VMEM_LIMIT),
        name="outproj_ffn2_ple",
    )(h, att, ssm, p, *weights)


BLOCKS_PER_PAGE = PAGE_SIZE // CMP_STRIDE
HEAD_ROWS = 8


def _head_rows_of(h):
    row = lax.broadcasted_iota(jnp.int32, (HEAD_ROWS, 1), 0)
    return (row >= h * GQA) & (row < (h + 1) * GQA)


def _cmp_sample_kernel(pt_ref, q_ref, cache_ref, wbig_ref, pe_ref, b1_ref, w2_ref, mselt_ref,
                       ocmp_ref, imp_ref, xbuf_ref, sem_ref, *, n_cmp):
    b = pl.program_id(0)
    nb_total = pl.num_programs(0)
    n_pages = pt_ref.shape[1]
    nbp = n_pages * BLOCKS_PER_PAGE

    def page_copy(bb, slot, i):
        return pltpu.make_async_copy(cache_ref.at[pt_ref[bb, i]],
                                     xbuf_ref.at[slot, pl.ds(i * BLOCKS_PER_PAGE, BLOCKS_PER_PAGE), :],
                                     sem_ref.at[slot])

    def start_all(bb, slot):
        def body(i, carry):
            page_copy(bb, slot, i).start()
            return carry
        lax.fori_loop(0, n_pages, body, 0)

    def wait_all(bb, slot):
        def body(i, carry):
            page_copy(bb, slot, i).wait()
            return carry
        lax.fori_loop(0, n_pages, body, 0)

    slot = b % 2

    @pl.when(b == 0)
    def _():
        start_all(0, 0)

    @pl.when(b + 1 < nb_total)
    def _():
        start_all(b + 1, 1 - slot)

    wait_all(b, slot)

    kcv = _cmp_tokens(xbuf_ref[slot].astype(BF16), wbig_ref, pe_ref, b1_ref, w2_ref)
    q8 = (q_ref[0] * (HEAD_DIM ** -0.5)).astype(BF16)
    valid = lax.broadcasted_iota(jnp.int32, (1, nbp), 1) < n_cmp
    o_cmp = jnp.zeros((HEAD_ROWS, HEAD_DIM), F32)
    psum = jnp.zeros((HEAD_ROWS, nbp), F32)
    row = lax.broadcasted_iota(jnp.int32, (HEAD_ROWS, 1), 0)
    for h in range(N_KV_HEADS):
        kc = kcv[:, h * HEAD_DIM:(h + 1) * HEAD_DIM].astype(BF16)
        vc = kcv[:, (N_KV_HEADS + h) * HEAD_DIM:(N_KV_HEADS + h + 1) * HEAD_DIM].astype(BF16)
        p = _softmax_rows(jnp.where(valid, _dot_nt(q8, kc), NEG_BIG))
        mine = _head_rows_of(h)
        o_cmp = jnp.where(mine, _dot(p.astype(BF16), vc), o_cmp)
        ph = jnp.sum(jnp.where(mine, p, 0.0), axis=0, keepdims=True)
        psum = jnp.where(row == h, ph, psum)
    ocmp_ref[0] = o_cmp
    p_hi = psum.astype(BF16)
    p_lo = (psum - p_hi.astype(F32)).astype(BF16)
    imp_ref[0] = _dot_nt(p_hi, mselt_ref[...]) + _dot_nt(p_lo, mselt_ref[...])


def _cmp_sample(page_table, q8, cache_blocks, wbig, pe_rows, b1, w2, mselt, n_cmp):
    db, n_pages = page_table.shape
    nbp = n_pages * BLOCKS_PER_PAGE
    consts = (wbig, pe_rows, b1, w2, mselt)
    grid_spec = pltpu.PrefetchScalarGridSpec(
        num_scalar_prefetch=1,
        grid=(db,),
        in_specs=[pl.BlockSpec((1, HEAD_ROWS, HEAD_DIM), lambda i, pt: (i, 0, 0)),
                  pl.BlockSpec(memory_space=pl.ANY)]
        + [pl.BlockSpec(c.shape, lambda i, pt, nd=c.ndim: (0,) * nd, pipeline_mode=pl.Buffered(1)) for c in consts],
        out_specs=[pl.BlockSpec((1, HEAD_ROWS, HEAD_DIM), lambda i, pt: (i, 0, 0)),
                   pl.BlockSpec((1, HEAD_ROWS, SEL_LANES), lambda i, pt: (i, 0, 0))],
        scratch_shapes=[pltpu.VMEM((2, nbp, CMP_IN), F32), pltpu.SemaphoreType.DMA((2,))],
    )
    return pl.pallas_call(
        functools.partial(_cmp_sample_kernel, n_cmp=n_cmp),
        grid_spec=grid_spec,
        out_shape=[jax.ShapeDtypeStruct((db, HEAD_ROWS, HEAD_DIM), F32),
                   jax.ShapeDtypeStruct((db, HEAD_ROWS, SEL_LANES), F32)],
        compiler_params=pltpu.CompilerParams(dimension_semantics=("arbitrary",), vmem_limit_bytes=VMEM_LIMIT),
        name="nsa_cmp_sample",
    )(page_table, q8, cache_blocks, *consts)


N_PAST_PICKS = SEL_TOPK - 1


def _topk_sample_kernel(imp_ref, idx_ref, *, past_blk):
    x = imp_ref[...].T
    jj = lax.broadcasted_iota(jnp.int32, x.shape, 0)
    x = jnp.where((jj == 0) | (jj == past_blk - 1), IMP_FORCED, x)
    x = jnp.where(jj < past_blk, x, IMP_BLOCKED)
    _, picks = _take_top(x, N_PAST_PICKS)
    picks = picks + [jnp.zeros_like(picks[0])] * (idx_ref.shape[0] - N_PAST_PICKS)
    idx_ref[...] = jnp.concatenate(picks, axis=0).astype(jnp.int32)


def _topk_sample(imp2, past_blk):
    nq = imp2.shape[0]
    assert N_PAST_PICKS <= past_blk <= SEL_LANES
    return pl.pallas_call(
        functools.partial(_topk_sample_kernel, past_blk=past_blk),
        out_shape=jax.ShapeDtypeStruct((SEL_TOPK, nq), jnp.int32),
        name="nsa_topk_sample",
    )(imp2)


SUB_PER_PAGE = PAGE_SIZE // SEL_BLOCK
SUB_SHIFT = SUB_PER_PAGE.bit_length() - 1
N_SEL_ROWS = N_PAST_PICKS * SEL_BLOCK


def _one_token_attention(q8, keys, values, valid):
    p = _softmax_rows(jnp.where(valid, _dot_nt(q8, keys.astype(BF16)), NEG_BIG))
    return _dot(p.astype(BF16), values.astype(BF16))


def _sel_win_sample_kernel(idx_ref, pt_ref, q_ref, gate_ref, ocmp_ref, kvs_ref, kvw_ref, win_ref, slc_ref,
                           att_ref, newwin_ref, gbuf_ref, sem_ref):
    b = pl.program_id(0)
    nb_total = pl.num_programs(0)
    w_buf = win_ref.shape[1]

    def block_copy(bb, slot, h, k):
        j = idx_ref[k, bb * N_KV_HEADS + h]
        page = pt_ref[bb, j >> SUB_SHIFT]
        return pltpu.make_async_copy(slc_ref.at[page * SUB_PER_PAGE + (j & (SUB_PER_PAGE - 1))],
                                     gbuf_ref.at[slot, h, pl.ds(k * SEL_BLOCK, SEL_BLOCK), :],
                                     sem_ref.at[slot])

    def start_all(bb, slot):
        for h in range(N_KV_HEADS):
            for k in range(N_PAST_PICKS):
                block_copy(bb, slot, h, k).start()

    def wait_all(bb, slot):
        for h in range(N_KV_HEADS):
            for k in range(N_PAST_PICKS):
                block_copy(bb, slot, h, k).wait()

    slot = b % 2

    @pl.when(b == 0)
    def _():
        start_all(0, 0)

    @pl.when(b + 1 < nb_total)
    def _():
        start_all(b + 1, 1 - slot)

    wait_all(b, slot)

    q8 = (q_ref[0] * (HEAD_DIM ** -0.5)).astype(BF16)
    pad = jnp.zeros((HEAD_ROWS - 1, KV_DIM), F32)
    sel_new = jnp.concatenate([kvs_ref[0], pad], axis=0)
    win_new = jnp.concatenate([kvw_ref[0], pad], axis=0)
    win_all = jnp.concatenate([win_ref[0], win_new], axis=0)
    sel_col = lax.broadcasted_iota(jnp.int32, (1, N_SEL_ROWS + HEAD_ROWS), 1)
    win_col = lax.broadcasted_iota(jnp.int32, (1, w_buf + HEAD_ROWS), 1)
    sel_valid = sel_col <= N_SEL_ROWS
    win_valid = (win_col <= w_buf) & (w_buf - win_col < WINDOW)
    o_sel = jnp.zeros((HEAD_ROWS, HEAD_DIM), F32)
    o_win = jnp.zeros((HEAD_ROWS, HEAD_DIM), F32)
    for h in range(N_KV_HEADS):
        kcols = slice(h * HEAD_DIM, (h + 1) * HEAD_DIM)
        vcols = slice((N_KV_HEADS + h) * HEAD_DIM, (N_KV_HEADS + h + 1) * HEAD_DIM)
        mine = _head_rows_of(h)
        sel_all = jnp.concatenate([gbuf_ref[slot, h], sel_new], axis=0)
        o_sel = jnp.where(mine, _one_token_attention(q8, sel_all[:, kcols], sel_all[:, vcols], sel_valid), o_sel)
        o_win = jnp.where(mine, _one_token_attention(q8, win_all[:, kcols], win_all[:, vcols], win_valid), o_win)
    gates = jax.nn.sigmoid(gate_ref[0])
    att_ref[0] = gates[:, 0:1] * ocmp_ref[0] + gates[:, 1:2] * o_sel + gates[:, 2:3] * o_win
    newwin_ref[0, 0:w_buf - 1, :] = win_ref[0, 1:w_buf, :]
    newwin_ref[0, w_buf - 1:w_buf, :] = kvw_ref[0]


def _sel_win_sample(idx, page_table, q8, gate8, o_cmp, kvs_new, kvw_new, win_buf, slc_blocks):
    db = page_table.shape[0]
    w_buf = win_buf.shape[1]
    assert w_buf == WINDOW
    per_tok = lambda shape: pl.BlockSpec((1,) + shape[1:], lambda i, idx, pt: (i, 0, 0))
    grid_spec = pltpu.PrefetchScalarGridSpec(
        num_scalar_prefetch=2,
        grid=(db,),
        in_specs=[per_tok(q8.shape), per_tok(gate8.shape), per_tok(o_cmp.shape), per_tok(kvs_new.shape),
                  per_tok(kvw_new.shape), per_tok(win_buf.shape), pl.BlockSpec(memory_space=pl.ANY)],
        out_specs=[per_tok(o_cmp.shape), per_tok(win_buf.shape)],
        scratch_shapes=[pltpu.VMEM((2, N_KV_HEADS, N_SEL_ROWS, KV_DIM), F32), pltpu.SemaphoreType.DMA((2,))],
    )
    return pl.pallas_call(
        _sel_win_sample_kernel,
        grid_spec=grid_spec,
        out_shape=[jax.ShapeDtypeStruct(o_cmp.shape, F32), jax.ShapeDtypeStruct(win_buf.shape, F32)],
        compiler_params=pltpu.CompilerParams(dimension_semantics=("arbitrary",), vmem_limit_bytes=VMEM_LIMIT),
        name="nsa_sel_win_sample",
    )(idx, page_table, q8, gate8, o_cmp, kvs_new, kvw_new, win_buf, slc_blocks)


SSD_BT = 8


def _ssd_sample_kernel(xbc_ref, cst_ref, z_ref, small_ref, h0_ref, convw_ref, convb_ref, dtb_ref, alog_ref,
                       dskip_ref, norm_ref, y_ref, hnew_ref, ys_ref):
    conv = convb_ref[...] + xbc_ref[...] * convw_ref[CONV_W - 1:CONV_W, :]
    for k in range(CONV_W - 1):
        conv = conv + cst_ref[k] * convw_ref[k:k + 1, :]
    xc = _silu(conv)
    xs = xc[:, :D_INNER]
    dt = jax.nn.softplus(small_ref[...] + dtb_ref[...])
    dt_e = _expand_heads(dt)
    decay_e = jnp.exp(dt_e * _expand_heads(-jnp.exp(alog_ref[...])))
    xd = xs * dt_e
    fill = jnp.zeros((LANES - SSD_BT, D_INNER), F32)
    xd_t = jnp.concatenate([xd, fill], axis=0).T
    decay_t = jnp.concatenate([decay_e, fill], axis=0).T
    lane = lax.broadcasted_iota(jnp.int32, (1, D_INNER), 1)
    for i in range(SSD_BT):
        bsel = jnp.concatenate(
            [jnp.broadcast_to(xc[i:i + 1, D_INNER + g * D_STATE:D_INNER + (g + 1) * D_STATE], (GROUP_W, D_STATE))
             for g in range(N_GROUPS)], axis=0)
        h0 = h0_ref[i].reshape(D_INNER, D_STATE)
        hn = decay_t[:, i:i + 1] * h0 + xd_t[:, i:i + 1] * bsel
        hnew_ref[i] = hn.reshape(SSM_HEADS, SSM_HEAD_DIM, D_STATE)
        c8 = jnp.concatenate(
            [xc[i:i + 1, D_INNER + (N_GROUPS + g) * D_STATE:D_INNER + (N_GROUPS + g + 1) * D_STATE]
             for g in range(N_GROUPS)] + [jnp.zeros((HEAD_ROWS - N_GROUPS, D_STATE), F32)], axis=0)
        y8 = _dot_nt(c8.astype(BF16), hn.astype(BF16))
        ys_ref[i:i + 1, :] = jnp.where(lane < GROUP_W, y8[0:1], y8[1:2])
    y = ys_ref[...] + dskip_ref[...] * xs
    y_ref[...] = _grouped_norm_gate(y, z_ref[...], norm_ref[...])


def _ssd_sample(xbc, conv_state_t, z, small, h0, conv_w, conv_b, dt_bias, a_log, d_skip, ssm_norm):
    db = xbc.shape[0]
    assert db % SSD_BT == 0 and N_GROUPS == 2
    params = (conv_w, conv_b.reshape(1, CONV_DIM), _pad_small(dt_bias), _pad_small(a_log),
              jnp.repeat(d_skip.astype(F32), SSM_HEAD_DIM).reshape(1, D_INNER), ssm_norm.reshape(1, D_INNER))
    rows = lambda w: pl.BlockSpec((SSD_BT, w), lambda i: (i, 0))
    state_spec = pl.BlockSpec((SSD_BT, SSM_HEADS, SSM_HEAD_DIM, D_STATE), lambda i: (i, 0, 0, 0))
    return pl.pallas_call(
        _ssd_sample_kernel,
        grid=(db // SSD_BT,),
        in_specs=[rows(CONV_DIM), pl.BlockSpec((CONV_W - 1, SSD_BT, CONV_DIM), lambda i: (0, i, 0)), rows(D_INNER),
                  rows(SMALL_W), state_spec] + [_const_spec(p.shape) for p in params],
        out_specs=[rows(D_INNER), state_spec],
        out_shape=[jax.ShapeDtypeStruct((db, D_INNER), F32), jax.ShapeDtypeStruct(h0.shape, F32)],
        scratch_shapes=[pltpu.VMEM((SSD_BT, D_INNER), F32)],
        compiler_params=pltpu.CompilerParams(dimension_semantics=("arbitrary",), vmem_limit_bytes=VMEM_LIMIT),
        name="ssd_sample",
    )(xbc, conv_state_t, z, small, h0, *params)


PROMPT_TM = 512


def _ffn_weights(w_gate, w_up, w_down):
    chunked = lambda w: jnp.transpose(w.astype(BF16).reshape(D_MODEL, N_FF_CHUNKS, FF_CHUNK), (1, 0, 2))
    return chunked(w_gate), chunked(w_up), w_down.astype(BF16).reshape(N_FF_CHUNKS, FF_CHUNK, D_MODEL)


def _in_proj_weight(w_in):
    cuts = np.cumsum([ATT_DIM, KV_DIM, KV_DIM, KV_DIM, N_GATE, D_INNER, CONV_DIM]).tolist()
    q, kvc, kvs, kvw, g, z, xbc, dt = jnp.split(w_in, cuts, axis=-1)
    pad = jnp.zeros((D_MODEL, SMALL_W - N_GATE - SSM_HEADS), w_in.dtype)
    return jnp.concatenate([q, kvc, kvs, kvw, z, xbc, g, dt, pad], axis=-1).astype(BF16)


def kernel(x_prompt, x_sample, cache_cmp_kv, cache_slc_kv, cache_win_kv, state_conv, state_ssm, page_table,
           p_prompt, p_sample, ffn1_norm, ffn1_w_gate, ffn1_w_up, ffn1_w_down, mix_norm, w_in,
           cmp_w1, cmp_pe, cmp_b1, cmp_w2, conv_w, conv_b, dt_bias, a_log, d_skip, ssm_norm, w_out,
           ffn2_norm, ffn2_w_gate, ffn2_w_up, ffn2_w_down, ple_norm, w_ple_gate, w_ple, final_norm):
    b, l, _ = x_prompt.shape
    db, t_new, _ = x_sample.shape
    depth = ffn1_norm.shape[0]
    n_pool = cache_cmp_kv.shape[0]
    n_pages = page_table.shape[1]
    past = n_pages * PAGE_SIZE
    assert depth == 1 and t_new == 1 and past >= WINDOW and (b * l) % PROMPT_TM == 0
    i = 0
    row = lambda v: v.reshape(1, -1).astype(F32)

    ffn1 = _ffn_weights(ffn1_w_gate[i], ffn1_w_up[i], ffn1_w_down[i])
    stage1_w = (row(ffn1_norm[i]),) + ffn1 + (row(mix_norm[i]), _in_proj_weight(w_in[i]))
    cmp_w = _cmp_weights(cmp_w1[i], cmp_pe[i], cmp_b1[i], cmp_w2[i])
    ssm_w = (conv_w[i], conv_b[i], dt_bias[i], a_log[i], d_skip[i], ssm_norm[i])
    w_o = w_out[i].astype(BF16)
    stage3_w = (w_o[:ATT_DIM], w_o[ATT_DIM:], row(ffn2_norm[i])) + _ffn_weights(ffn2_w_gate[i], ffn2_w_up[i], ffn2_w_down[i]) + (
        row(ple_norm[i]), w_ple_gate[i].astype(BF16), w_ple[i].astype(BF16), row(final_norm))
    kv6 = lambda a, lead, rows: a.reshape(lead, 1, rows, 2, N_KV_HEADS, HEAD_DIM)

    h, q, kvc, kvs, kvw, z, xbc, small = _ffn_inproj(x_prompt.reshape(b * l, D_MODEL), *stage1_w, PROMPT_TM)
    per_b = lambda a: a.reshape(b, l, a.shape[-1])
    kcv = _cmp_prompt(kvc.reshape(b, l // CMP_STRIDE, CMP_IN), *cmp_w)
    att = _nsa_prompt(per_b(q), per_b(small), kcv, per_b(kvs), per_b(kvw))
    ssm, ssm_state = _ssd_prompt(per_b(xbc), per_b(z), per_b(small), *ssm_w)
    y_prompt = _out_ffn(h, att.reshape(b * l, ATT_DIM), ssm.reshape(b * l, D_INNER),
                        p_prompt[i].reshape(b * l, PLE_DIM), stage3_w, PROMPT_TM).reshape(b, l, D_MODEL)
    keep = min(WINDOW, l)
    new_cmp_p = kv6(kvc, b, l)
    new_slc_p = kv6(kvs, b, l)
    new_win_p = kv6(per_b(kvw)[:, l - keep:], b, keep)
    new_conv_p = per_b(xbc)[:, l - (CONV_W - 1):].reshape(b, 1, CONV_W - 1, CONV_DIM)
    new_ssm_p = ssm_state.reshape(b, 1, SSM_HEADS, SSM_HEAD_DIM, D_STATE)

    hs, qs, kvc_s, kvs_s, kvw_s, z_s, xbc_s, small_s = _ffn_inproj(x_sample.reshape(db, D_MODEL), *stage1_w, db)
    q8 = qs.reshape(db, N_HEADS, HEAD_DIM)
    n_cmp_s = (past + t_new - CMP_LEN) // CMP_STRIDE + 1
    past_blk = past // SEL_BLOCK
    mselt_s = _cmp_to_sel_t(n_pages * BLOCKS_PER_PAGE, n_cmp_s, past_blk)
    o_cmp, imp = _cmp_sample(page_table, q8, cache_cmp_kv.reshape(n_pool, BLOCKS_PER_PAGE, CMP_IN), *cmp_w,
                             mselt_s, n_cmp_s)
    n_query = db * N_KV_HEADS
    imp2 = jnp.pad(imp[:, :N_KV_HEADS].reshape(n_query, SEL_LANES), ((0, -n_query % LANES), (0, 0)))
    idx = _topk_sample(imp2, past_blk)
    att_s, new_win_s = _sel_win_sample(
        idx, page_table, q8, small_s[:, :N_GATE].reshape(db, N_HEADS, 3), o_cmp,
        kvs_s.reshape(db, 1, KV_DIM), kvw_s.reshape(db, 1, KV_DIM),
        cache_win_kv.reshape(db, cache_win_kv.shape[2], KV_DIM),
        cache_slc_kv.reshape(n_pool * SUB_PER_PAGE, SEL_BLOCK, KV_DIM))
    ssm_s, ssm_state_s = _ssd_sample(xbc_s, jnp.transpose(state_conv[:, i], (1, 0, 2)), z_s, small_s,
                                     state_ssm[:, i], *ssm_w)
    y_sample = _out_ffn(hs, att_s.reshape(db, ATT_DIM), ssm_s, p_sample[i].reshape(db, PLE_DIM),
                        stage3_w, db).reshape(db, 1, D_MODEL)
    new_conv_s = jnp.concatenate([state_conv[:, i, 1:], xbc_s[:, None]], axis=1)[:, None]

    return (y_prompt, y_sample, new_cmp_p, new_slc_p, new_win_p, new_conv_p, new_ssm_p,
            kv6(kvc_s, db, 1), kv6(kvs_s, db, 1), kv6(new_win_s, db, new_win_s.shape[1]), new_conv_s,
            ssm_state_s[:, None])
```

```python
import functools

import jax
import jax.numpy as jnp
import numpy as np
from jax import lax
from jax.experimental import pallas as pl
from jax.experimental.pallas import tpu as pltpu

F32 = jnp.float32
BF16 = jnp.bfloat16

D_MODEL = 1024
N_HEADS = 8
N_KV_HEADS = 2
HEAD_DIM = 64
GQA = N_HEADS // N_KV_HEADS
ATT_DIM = N_HEADS * HEAD_DIM
KV_DIM = 2 * N_KV_HEADS * HEAD_DIM
CMP_LEN = 32
CMP_STRIDE = 16
CMP_HID = 128
SEL_BLOCK = 64
SEL_TOPK = 16
WINDOW = 512
Q_BLOCK = 128
SSM_HEADS = 8
SSM_HEAD_DIM = 64
D_INNER = SSM_HEADS * SSM_HEAD_DIM
N_GROUPS = 2
D_STATE = 128
CONV_W = 4
CONV_DIM = D_INNER + 2 * N_GROUPS * D_STATE
SSD_CHUNK = 128
D_FF = 2816
PLE_DIM = 256
PAGE_SIZE = 128
EPS = 1e-6

LANES = 128
FF_CHUNK = 256
N_FF_CHUNKS = D_FF // FF_CHUNK
SMALL_W = LANES
N_GATE = 3 * N_HEADS
IN_PROJ_PAD = ATT_DIM + 3 * KV_DIM + D_INNER + CONV_DIM + SMALL_W
VMEM_LIMIT = 56 * 1024 * 1024
NEG_BIG = -1e30
IMP_FORCED = 3e38
IMP_BLOCKED = -1e38
IMP_TAKEN = -3e38


def _dot(a, b):
    return jnp.dot(a, b, preferred_element_type=F32)


def _dot_nt(a, b):
    return lax.dot_general(a, b, (((1,), (1,)), ((), ())), preferred_element_type=F32)


def _rms(x, g):
    return x * lax.rsqrt(jnp.mean(x * x, axis=-1, keepdims=True) + EPS) * g


def _silu(x):
    return x * jax.nn.sigmoid(x)


def _const_spec(shape):
    nd = len(shape)
    return pl.BlockSpec(shape, lambda *_: (0,) * nd, pipeline_mode=pl.Buffered(1))


def _row_spec(tm, width):
    return pl.BlockSpec((tm, width), lambda i: (i, 0))


def _swiglu_half_step(x, g_ref, wg_ref, wu_ref, wd_ref, acc_ref):
    xn = _rms(x, g_ref[...]).astype(BF16)
    acc_ref[...] = jnp.zeros_like(acc_ref)

    def body(c, carry):
        a = _dot(xn, wg_ref[c])
        b = _dot(xn, wu_ref[c])
        hm = (_silu(a) * b).astype(BF16)
        acc_ref[...] += _dot(hm, wd_ref[c])
        return carry

    lax.fori_loop(0, N_FF_CHUNKS, body, 0)
    return x + 0.5 * acc_ref[...]


_IN_SEGS = (ATT_DIM, KV_DIM, KV_DIM, KV_DIM, D_INNER, CONV_DIM, SMALL_W)


def _ffn_inproj_kernel(x_ref, g1_ref, wg_ref, wu_ref, wd_ref, gm_ref, win_ref,
                       h_ref, q_ref, kvc_ref, kvs_ref, kvw_ref, z_ref, xbc_ref, small_ref, acc_ref):
    h = _swiglu_half_step(x_ref[...], g1_ref, wg_ref, wu_ref, wd_ref, acc_ref)
    h_ref[...] = h
    hn = _rms(h, gm_ref[...]).astype(BF16)
    off = 0
    for out_ref, width in zip((q_ref, kvc_ref, kvs_ref, kvw_ref, z_ref, xbc_ref, small_ref), _IN_SEGS):
        out_ref[...] = _dot(hn, win_ref[:, off:off + width])
        off += width


def _ffn_inproj(x, g1, wg, wu, wd, gm, win, tm):
    n = x.shape[0]
    outs = [jax.ShapeDtypeStruct((n, D_MODEL), F32)] + [jax.ShapeDtypeStruct((n, w), F32) for w in _IN_SEGS]
    return pl.pallas_call(
        _ffn_inproj_kernel,
        grid=(n // tm,),
        in_specs=[_row_spec(tm, D_MODEL), _const_spec(g1.shape), _const_spec(wg.shape), _const_spec(wu.shape),
                  _const_spec(wd.shape), _const_spec(gm.shape), _const_spec(win.shape)],
        out_specs=[_row_spec(tm, D_MODEL)] + [_row_spec(tm, w) for w in _IN_SEGS],
        out_shape=outs,
        scratch_shapes=[pltpu.VMEM((tm, D_MODEL), F32)],
        compiler_params=pltpu.CompilerParams(dimension_semantics=("arbitrary",), vmem_limit_bytes=VMEM_LIMIT),
        name="ffn1_inproj",
    )(x, g1, wg, wu, wd, gm, win)


HD2 = N_KV_HEADS * HEAD_DIM
CMP_R = CMP_LEN // CMP_STRIDE
CMP_PROJ = N_KV_HEADS * CMP_R * CMP_HID
PE_ROWS = 8


def _cmp_weights(cmp_w1, cmp_pe, cmp_b1, cmp_w2):
    w1r = cmp_w1.reshape(2, CMP_R, CMP_STRIDE, HEAD_DIM, CMP_HID)
    per = cmp_pe.reshape(2, CMP_R, CMP_STRIDE, HEAD_DIM)
    wbd = jnp.zeros((2, CMP_STRIDE, N_KV_HEADS, HEAD_DIM, N_KV_HEADS, CMP_R, CMP_HID), F32)
    blk = jnp.transpose(w1r, (0, 2, 3, 1, 4))
    for h in range(N_KV_HEADS):
        wbd = wbd.at[:, :, h, :, h, :, :].set(blk)
    wbd = wbd.reshape(2, CMP_STRIDE, HD2, CMP_PROJ).astype(BF16)
    pe_rows = jnp.transpose(per, (0, 2, 1, 3))
    pe_rows = jnp.broadcast_to(pe_rows[:, :, :, None, :], (2, CMP_STRIDE, CMP_R, N_KV_HEADS, HEAD_DIM))
    pe_rows = jnp.concatenate([pe_rows.reshape(2, CMP_STRIDE, CMP_R, HD2),
                               jnp.zeros((2, CMP_STRIDE, PE_ROWS - CMP_R, HD2), F32)], axis=2)
    return wbd, pe_rows, cmp_b1, cmp_w2.astype(BF16)


def _cmp_tokens(load_rows, nb, wbd_ref, pe_ref, b1_ref, w2_ref):
    outs = []
    for c in range(2):
        acc = jnp.zeros((nb + PE_ROWS, CMP_PROJ), F32)
        for s in range(CMP_STRIDE):
            lhs = jnp.concatenate([load_rows(c, s), pe_ref[c, s]], axis=0).astype(BF16)
            acc = acc + _dot(lhs, wbd_ref[c, s])
        proj, pe_proj = acc[:nb], acc[nb:]
        for h in range(N_KV_HEADS):
            lo = h * CMP_R * CMP_HID
            p0 = proj[:, lo:lo + CMP_HID]
            p1 = pltpu.roll(proj[:, lo + CMP_HID:lo + 2 * CMP_HID], nb - 1, 0)
            pe_add = pe_proj[0:1, lo:lo + CMP_HID] + pe_proj[1:2, lo + CMP_HID:lo + 2 * CMP_HID]
            hid = _silu(p0 + p1 + pe_add + b1_ref[c:c + 1, :])
            outs.append(_dot(hid.astype(BF16), w2_ref[c]))
    return jnp.concatenate(outs, axis=1)


def _cmp_prompt_kernel(xk_ref, xv_ref, wbd_ref, pe_ref, b1_ref, w2_ref, out_ref, out_t_ref):
    nb = out_ref.shape[1]
    load_rows = lambda c, s: (xk_ref, xv_ref)[c][0, pl.ds(s, nb, stride=CMP_STRIDE), :]
    tokens = _cmp_tokens(load_rows, nb, wbd_ref, pe_ref, b1_ref, w2_ref)
    out_ref[0] = tokens
    out_t_ref[0] = tokens.T


def _cmp_prompt(kvc, wbd, pe_rows, b1, w2):
    b, l, _ = kvc.shape
    nb = l // CMP_STRIDE
    return pl.pallas_call(
        _cmp_prompt_kernel,
        grid=(b,),
        in_specs=[pl.BlockSpec((1, l, HD2), lambda i: (i, 0, 0)), pl.BlockSpec((1, l, HD2), lambda i: (i, 0, 1)),
                  _const_spec(wbd.shape), _const_spec(pe_rows.shape), _const_spec(b1.shape), _const_spec(w2.shape)],
        out_specs=[pl.BlockSpec((1, nb, KV_DIM), lambda i: (i, 0, 0)), pl.BlockSpec((1, KV_DIM, nb), lambda i: (i, 0, 0))],
        out_shape=[jax.ShapeDtypeStruct((b, nb, KV_DIM), F32), jax.ShapeDtypeStruct((b, KV_DIM, nb), F32)],
        compiler_params=pltpu.CompilerParams(dimension_semantics=("arbitrary",), vmem_limit_bytes=VMEM_LIMIT),
        name="nsa_cmp_prompt",
    )(kvc, kvc, wbd, pe_rows, b1, w2)


SEL_TILE = 512
COLS = GQA * Q_BLOCK
SEL_LANES = 128
AUG = SEL_LANES + 2 * HEAD_DIM
TINY = float(np.finfo(np.float32).tiny)
SEL_SHIFT = SEL_BLOCK.bit_length() - 1


def _softmax_rows(s):
    m = jnp.max(s, axis=-1, keepdims=True)
    m = jnp.where(m > 0.5 * NEG_BIG, m, 0.0)
    e = jnp.exp(s - m)
    return e / jnp.maximum(jnp.sum(e, axis=-1, keepdims=True), TINY)


def _select_blocks(imp_t, t_lane):
    jj = lax.broadcasted_iota(jnp.int32, imp_t.shape, 0)
    cb = t_lane >> SEL_SHIFT
    forced = (jj == 0) | (jj == cb) | (jj == cb - 1)
    causal = (jj << SEL_SHIFT) <= t_lane
    x = jnp.where(forced, IMP_FORCED, imp_t)
    x = jnp.where(causal, x, IMP_BLOCKED)
    sel, _ = _take_top(x, SEL_TOPK)
    return sel


def _take_top(x, k):
    nj = x.shape[0]
    jf = lax.broadcasted_iota(jnp.int32, x.shape, 0).astype(F32)
    sel = jnp.zeros_like(x)
    picks = []
    for _ in range(k):
        m = jnp.max(x, axis=0, keepdims=True)
        first = jnp.min(jnp.where(x == m, jf, float(nj)), axis=0, keepdims=True)
        hit = jf == first
        sel = jnp.where(hit, 1.0, sel)
        x = jnp.where(hit, IMP_TAKEN, x)
        picks.append(first)
    return sel, picks


def _softmax_cols(s):
    m = jnp.max(s, axis=0, keepdims=True)
    m = jnp.where(m > 0.5 * NEG_BIG, m, 0.0)
    e = jnp.exp(s - m)
    return e / jnp.maximum(jnp.sum(e, axis=0, keepdims=True), TINY)


def _dot_tn(a, b):
    return lax.dot_general(a, b, (((0,), (0,)), ((), ())), preferred_element_type=F32)


def _nsa_prompt_kernel(q_ref, small_ref, kcv_ref, kcvt_ref, kaug_ref, vaugt_ref, kwin_ref, mselt_ref, out_ref,
                       lt_ref, acc_ref, ot_ref):
    n = pl.program_id(1)
    t0 = n * Q_BLOCK
    nb = kcv_ref.shape[1]
    col_t = t0 + (lax.broadcasted_iota(jnp.int32, (1, COLS), 1) & (Q_BLOCK - 1))
    lane_t = t0 + lax.broadcasted_iota(jnp.int32, (1, Q_BLOCK), 1)
    gates_t = jax.nn.sigmoid(small_ref[0]).T
    q_t = (q_ref[0] * (HEAD_DIM ** -0.5)).T.astype(BF16)
    c_end = lax.broadcasted_iota(jnp.int32, (nb, 1), 0) * CMP_STRIDE + (CMP_LEN - 1)
    o_cmp_t = []

    for h in range(N_KV_HEADS):
        for g in range(GQA):
            lo = (h * GQA + g) * HEAD_DIM
            lt_ref[h, SEL_LANES:SEL_LANES + HEAD_DIM, g * Q_BLOCK:(g + 1) * Q_BLOCK] = q_t[lo:lo + HEAD_DIM, :]
        lt_ref[h, SEL_LANES + HEAD_DIM:, :] = jnp.zeros((HEAD_DIM, COLS), BF16)
        qh_t = lt_ref[h, SEL_LANES:SEL_LANES + HEAD_DIM, :]

        kc = kcv_ref[0, :, h * HEAD_DIM:(h + 1) * HEAD_DIM].astype(BF16)
        vc_t = kcvt_ref[0, (N_KV_HEADS + h) * HEAD_DIM:(N_KV_HEADS + h + 1) * HEAD_DIM, :].astype(BF16)
        p_t = _softmax_cols(jnp.where(c_end <= col_t, _dot(kc, qh_t), NEG_BIG))
        o_cmp_t.append(_dot(vc_t, p_t.astype(BF16)))

        psum = p_t[:, 0:Q_BLOCK]
        for g in range(1, GQA):
            psum = psum + p_t[:, g * Q_BLOCK:(g + 1) * Q_BLOCK]
        p_hi = psum.astype(BF16)
        p_lo = (psum - p_hi.astype(F32)).astype(BF16)
        imp_t = _dot(mselt_ref[...], p_hi) + _dot(mselt_ref[...], p_lo)
        neg = jnp.where(_select_blocks(imp_t, lane_t) > 0.0, 0.0, NEG_BIG).astype(BF16)
        for g in range(GQA):
            lt_ref[h, 0:SEL_LANES, g * Q_BLOCK:(g + 1) * Q_BLOCK] = neg

    acc_ref[...] = jnp.zeros(acc_ref.shape, F32)
    n_tiles = (t0 + Q_BLOCK + SEL_TILE - 1) // SEL_TILE

    def sel_step(kt, ms, causal_mask):
        k0 = pl.multiple_of(kt * SEL_TILE, SEL_TILE)
        out = []
        for h in range(N_KV_HEADS):
            s = _dot(kaug_ref[0, h, pl.ds(k0, SEL_TILE), :], lt_ref[h])
            if causal_mask:
                kpos = k0 + lax.broadcasted_iota(jnp.int32, (SEL_TILE, 1), 0)
                s = jnp.where(kpos <= col_t, s, NEG_BIG)
            m_new = jnp.maximum(ms[h], jnp.max(s, axis=0, keepdims=True))
            pe = jnp.exp(s - m_new).astype(BF16)
            acc_ref[h] = jnp.exp(ms[h] - m_new) * acc_ref[h] + _dot(vaugt_ref[0, h, kt], pe)
            out.append(m_new)
        return tuple(out)

    m_init = (jnp.full((1, COLS), NEG_BIG, F32),) * N_KV_HEADS
    ms = lax.fori_loop(0, n_tiles - 1, lambda kt, ms: sel_step(kt, ms, False), m_init)
    sel_step(n_tiles - 1, ms, True)

    w0 = pl.multiple_of(t0, Q_BLOCK)
    kpos = t0 - WINDOW + lax.broadcasted_iota(jnp.int32, (WINDOW + Q_BLOCK, 1), 0)
    dpos = col_t - kpos
    win_ok = (dpos >= 0) & (dpos < WINDOW) & (kpos >= 0)
    for h in range(N_KV_HEADS):
        acc = acc_ref[h]
        o_sel_t = acc[:HEAD_DIM] / jnp.maximum(acc[HEAD_DIM:HEAD_DIM + 1], TINY)

        kw = kwin_ref[0, pl.ds(w0, WINDOW + Q_BLOCK), h * HEAD_DIM:(h + 1) * HEAD_DIM]
        vw = kwin_ref[0, pl.ds(w0, WINDOW + Q_BLOCK), (N_KV_HEADS + h) * HEAD_DIM:(N_KV_HEADS + h + 1) * HEAD_DIM]
        qh_t = lt_ref[h, SEL_LANES:SEL_LANES + HEAD_DIM, :]
        p_t = _softmax_cols(jnp.where(win_ok, _dot(kw, qh_t), NEG_BIG))
        o_win_t = _dot_tn(vw, p_t.astype(BF16))

        for g in range(GQA):
            hd = h * GQA + g
            cols = slice(g * Q_BLOCK, (g + 1) * Q_BLOCK)
            ot_ref[hd * HEAD_DIM:(hd + 1) * HEAD_DIM, :] = (
                gates_t[3 * hd:3 * hd + 1, :] * o_cmp_t[h][:, cols] + gates_t[3 * hd + 1:3 * hd + 2, :] * o_sel_t[:, cols]
                + gates_t[3 * hd + 2:3 * hd + 3, :] * o_win_t[:, cols])
    out_ref[0] = ot_ref[...].T


def _cmp_to_sel_t(nb, n_cmp, n_blk):
    cs = np.arange(nb)[None, :] * CMP_STRIDE
    bs = np.arange(SEL_LANES)[:, None] * SEL_BLOCK
    ov = np.clip(np.minimum(cs + CMP_LEN, bs + SEL_BLOCK) - np.maximum(cs, bs), 0, None) / CMP_LEN
    ov = ov * (np.arange(nb)[None, :] < n_cmp) * (np.arange(SEL_LANES)[:, None] < n_blk)
    return jnp.asarray(ov, BF16)


def _nsa_prompt(q, small, kcv, kcv_t, kvs, kvw):
    b, l, _ = q.shape
    nb = kcv.shape[1]
    n_blk = l // SEL_BLOCK
    assert l % SEL_TILE == 0 and n_blk <= SEL_LANES and l >= CMP_LEN
    kvs5 = kvs.reshape(b, l, 2, N_KV_HEADS, HEAD_DIM)
    onehot = jnp.asarray(np.arange(l)[:, None] // SEL_BLOCK == np.arange(SEL_LANES)[None, :], BF16)
    k_sel = jnp.transpose(kvs5[:, :, 0], (0, 2, 1, 3)).astype(BF16)
    kaug = jnp.concatenate([jnp.broadcast_to(onehot, (b, N_KV_HEADS, l, SEL_LANES)), k_sel,
                            jnp.zeros((b, N_KV_HEADS, l, HEAD_DIM), BF16)], axis=-1)
    v_t = jnp.transpose(kvs5[:, :, 1].astype(BF16).reshape(b, l // SEL_TILE, SEL_TILE, N_KV_HEADS, HEAD_DIM),
                        (0, 3, 1, 4, 2))
    vaug_t = jnp.concatenate([v_t, jnp.ones((b, N_KV_HEADS, l // SEL_TILE, 1, SEL_TILE), BF16),
                              jnp.zeros((b, N_KV_HEADS, l // SEL_TILE, LANES - HEAD_DIM - 1, SEL_TILE), BF16)], axis=3)
    kwin = jnp.pad(kvw.astype(BF16), ((0, 0), (WINDOW, 0), (0, 0)))
    mselt = _cmp_to_sel_t(nb, (l - CMP_LEN) // CMP_STRIDE + 1, n_blk)

    def per_batch(shape):
        nd = len(shape)
        return pl.BlockSpec((1,) + shape[1:], lambda i, j: (i,) + (0,) * (nd - 1), pipeline_mode=pl.Buffered(1))

    return pl.pallas_call(
        _nsa_prompt_kernel,
        grid=(b, l // Q_BLOCK),
        in_specs=[pl.BlockSpec((1, Q_BLOCK, ATT_DIM), lambda i, j: (i, j, 0)),
                  pl.BlockSpec((1, Q_BLOCK, SMALL_W), lambda i, j: (i, j, 0)),
                  per_batch(kcv.shape), per_batch(kcv_t.shape), per_batch(kaug.shape), per_batch(vaug_t.shape),
                  per_batch(kwin.shape), pl.BlockSpec(mselt.shape, lambda i, j: (0, 0), pipeline_mode=pl.Buffered(1))],
        out_specs=pl.BlockSpec((1, Q_BLOCK, ATT_DIM), lambda i, j: (i, j, 0)),
        out_shape=jax.ShapeDtypeStruct((b, l, ATT_DIM), F32),
        scratch_shapes=[pltpu.VMEM((N_KV_HEADS, AUG, COLS), BF16), pltpu.VMEM((N_KV_HEADS, LANES, COLS), F32),
                        pltpu.VMEM((ATT_DIM, Q_BLOCK), F32)],
        compiler_params=pltpu.CompilerParams(dimension_semantics=("arbitrary", "arbitrary"),
                                             vmem_limit_bytes=VMEM_LIMIT),
        name="nsa_prompt",
    )(q, small, kcv, kcv_t, kaug, vaug_t, kwin, mselt)


DT_LANE = N_GATE
HEADS_PER_GROUP = SSM_HEADS // N_GROUPS
GROUP_W = D_INNER // N_GROUPS
TAIL = 8


def _expand_heads(v):
    rows = v.shape[0]
    lane = lax.broadcasted_iota(jnp.int32, (rows, LANES), 1)
    tiles = []
    for j in range(D_INNER // LANES):
        a = jnp.broadcast_to(v[:, DT_LANE + 2 * j:DT_LANE + 2 * j + 1], (rows, LANES))
        b = jnp.broadcast_to(v[:, DT_LANE + 2 * j + 1:DT_LANE + 2 * j + 2], (rows, LANES))
        tiles.append(jnp.where(lane < SSM_HEAD_DIM, a, b))
    return jnp.concatenate(tiles, axis=1)


def _cumsum_rows(x):
    n = x.shape[0]
    row = lax.broadcasted_iota(jnp.int32, x.shape, 0)
    s = 1
    while s < n:
        x = x + jnp.where(row >= s, pltpu.roll(x, s, 0), 0.0)
        s *= 2
    return x


def _grouped_norm_gate(y, z, norm):
    v = y * _silu(z)
    outs = []
    for g in range(N_GROUPS):
        vg = v[:, g * GROUP_W:(g + 1) * GROUP_W]
        outs.append(vg * lax.rsqrt(jnp.mean(vg * vg, axis=-1, keepdims=True) + EPS))
    return jnp.concatenate(outs, axis=1) * norm


def _ssd_prompt_kernel(xbc_ref, z_ref, small_ref, convw_ref, convb_ref, dtb_ref, alog_ref, dskip_ref, norm_ref,
                       y_ref, state_ref, xe_ref, st_ref, yd_ref):
    c = pl.program_id(1)
    lc = xbc_ref.shape[1]

    @pl.when(c == 0)
    def _():
        xe_ref[0:TAIL, :] = jnp.zeros((TAIL, CONV_DIM), F32)
        st_ref[...] = jnp.zeros(st_ref.shape, F32)

    xe_ref[TAIL:TAIL + lc, :] = xbc_ref[0]
    conv = convb_ref[...] + xe_ref[TAIL:TAIL + lc, :] * convw_ref[CONV_W - 1:CONV_W, :]
    for k in range(CONV_W - 1):
        conv = conv + xe_ref[pl.ds(TAIL - (CONV_W - 1) + k, lc), :] * convw_ref[k:k + 1, :]
    xe_ref[0:TAIL, :] = xe_ref[lc:lc + TAIL, :]
    xc = _silu(conv)
    xs = xc[:, :D_INNER]

    dt = jax.nn.softplus(small_ref[0] + dtb_ref[...])
    ad = dt * (-jnp.exp(alog_ref[...]))
    acs = _cumsum_rows(ad)
    acs_t = acs.T
    dt_e = _expand_heads(dt)
    acs_e = _expand_heads(acs)
    last_e = acs_e[lc - 1:lc, :]
    xd = xs * dt_e
    xd_bf = xd.astype(BF16)
    xdd_bf = (xd * jnp.exp(last_e - acs_e)).astype(BF16)
    grow = jnp.exp(acs_e)
    li = lax.broadcasted_iota(jnp.int32, (lc, lc), 0)
    si = lax.broadcasted_iota(jnp.int32, (lc, lc), 1)

    y_off = []
    for g in range(N_GROUPS):
        bm = xc[:, D_INNER + g * D_STATE:D_INNER + (g + 1) * D_STATE]
        cm = xc[:, D_INNER + N_GROUPS * D_STATE + g * D_STATE:D_INNER + N_GROUPS * D_STATE + (g + 1) * D_STATE]
        bm_bf = bm.astype(BF16)
        cm_bf = cm.astype(BF16)
        cb = _dot_nt(cm_bf, bm_bf)
        cols = slice(g * GROUP_W, (g + 1) * GROUP_W)
        st_g = st_ref[:, cols]
        y_off.append(_dot(cm_bf, st_g.astype(BF16)))
        for hh in range(HEADS_PER_GROUP):
            h = g * HEADS_PER_GROUP + hh
            seg = acs[:, DT_LANE + h:DT_LANE + h + 1] - acs_t[DT_LANE + h:DT_LANE + h + 1, :]
            m = jnp.where(li >= si, cb * jnp.exp(seg), 0.0).astype(BF16)
            yd_ref[:, h * SSM_HEAD_DIM:(h + 1) * SSM_HEAD_DIM] = _dot(m, xd_bf[:, h * SSM_HEAD_DIM:(h + 1) * SSM_HEAD_DIM])
        st_ref[:, cols] = st_g * jnp.exp(last_e[:, cols]) + _dot(bm.T.astype(BF16), xdd_bf[:, cols])

    y = yd_ref[...] + jnp.concatenate(y_off, axis=1) * grow + dskip_ref[...] * xs
    y_ref[0] = _grouped_norm_gate(y, z_ref[0], norm_ref[...])

    @pl.when(c == pl.num_programs(1) - 1)
    def _():
        state_ref[0] = st_ref[...].T


def _pad_small(v):
    return jnp.zeros((1, SMALL_W), F32).at[0, DT_LANE:DT_LANE + SSM_HEADS].set(v.astype(F32))


def _ssd_prompt(xbc, z, small, conv_w, conv_b, dt_bias, a_log, d_skip, ssm_norm):
    b, l, _ = xbc.shape
    lc = min(SSD_CHUNK, l)
    params = (conv_w, conv_b.reshape(1, CONV_DIM), _pad_small(dt_bias), _pad_small(a_log),
              jnp.repeat(d_skip.astype(F32), SSM_HEAD_DIM).reshape(1, D_INNER), ssm_norm.reshape(1, D_INNER))
    tile = lambda w: pl.BlockSpec((1, lc, w), lambda i, j: (i, j, 0))
    return pl.pallas_call(
        _ssd_prompt_kernel,
        grid=(b, l // lc),
        in_specs=[tile(CONV_DIM), tile(D_INNER), tile(SMALL_W)] + [_const_spec(p.shape) for p in params],
        out_specs=[tile(D_INNER), pl.BlockSpec((1, D_INNER, D_STATE), lambda i, j: (i, 0, 0))],
        out_shape=[jax.ShapeDtypeStruct((b, l, D_INNER), F32), jax.ShapeDtypeStruct((b, D_INNER, D_STATE), F32)],
        scratch_shapes=[pltpu.VMEM((lc + TAIL, CONV_DIM), F32), pltpu.VMEM((D_STATE, D_INNER), F32),
                        pltpu.VMEM((lc, D_INNER), F32)],
        compiler_params=pltpu.CompilerParams(dimension_semantics=("arbitrary", "arbitrary"),
                                             vmem_limit_bytes=VMEM_LIMIT),
        name="ssd_prompt",
    )(xbc, z, small, *params)


def _out_ffn_kernel(h_ref, att_ref, ssm_ref, p_ref, woa_ref, wos_ref, g2_ref, wg_ref, wu_ref, wd_ref,
                    gp_ref, wpg_ref, wple_ref, gf_ref, y_ref, acc_ref):
    h = h_ref[...] + _dot(att_ref[...].astype(BF16), woa_ref[...]) + _dot(ssm_ref[...].astype(BF16), wos_ref[...])
    h = _swiglu_half_step(h, g2_ref, wg_ref, wu_ref, wd_ref, acc_ref)
    gate = jax.nn.sigmoid(_dot(_rms(h, gp_ref[...]).astype(BF16), wpg_ref[...]))
    h = h + gate * _dot(p_ref[...].astype(BF16), wple_ref[...])
    y_ref[...] = _rms(h, gf_ref[...])


def _out_ffn(h, att, ssm, p, weights, tm):
    n = h.shape[0]
    return pl.pallas_call(
        _out_ffn_kernel,
        grid=(n // tm,),
        in_specs=[_row_spec(tm, D_MODEL), _row_spec(tm, ATT_DIM), _row_spec(tm, D_INNER), _row_spec(tm, PLE_DIM)]
        + [_const_spec(w.shape) for w in weights],
        out_specs=_row_spec(tm, D_MODEL),
        out_shape=jax.ShapeDtypeStruct((n, D_MODEL), F32),
        scratch_shapes=[pltpu.VMEM((tm, D_MODEL), F32)],
        compiler_params=pltpu.CompilerParams(dimension_semantics=("arbitrary",), vmem_limit_bytes=VMEM_LIMIT),
        name="outproj_ffn2_ple",
    )(h, att, ssm, p, *weights)


BLOCKS_PER_PAGE = PAGE_SIZE // CMP_STRIDE
HEAD_ROWS = 8


def _head_rows_of(h):
    row = lax.broadcasted_iota(jnp.int32, (HEAD_ROWS, 1), 0)
    return (row >= h * GQA) & (row < (h + 1) * GQA)


def _cmp_sample_kernel(pt_ref, q_ref, cache_ref, wbd_ref, pe_ref, b1_ref, w2_ref, mselt_ref,
                       ocmp_ref, imp_ref, xbuf_ref, xrow_ref, sem_ref, *, n_cmp):
    b = pl.program_id(0)
    nb_total = pl.num_programs(0)
    n_pages = pt_ref.shape[1]
    nbp = n_pages * BLOCKS_PER_PAGE

    def page_copy(bb, slot, i):
        return pltpu.make_async_copy(cache_ref.at[pt_ref[bb, i], 0], xbuf_ref.at[slot, i], sem_ref.at[slot])

    def start_all(bb, slot):
        def body(i, carry):
            page_copy(bb, slot, i).start()
            return carry
        lax.fori_loop(0, n_pages, body, 0)

    def wait_all(bb, slot):
        def body(i, carry):
            page_copy(bb, slot, i).wait()
            return carry
        lax.fori_loop(0, n_pages, body, 0)

    slot = b % 2

    @pl.when(b == 0)
    def _():
        start_all(0, 0)

    @pl.when(b + 1 < nb_total)
    def _():
        start_all(b + 1, 1 - slot)

    wait_all(b, slot)

    def load_rows(c, s):
        if s == 0:
            def fill(i, carry):
                tile = xbuf_ref[slot, i, c].reshape(HD2, PAGE_SIZE)
                xrow_ref[pl.ds(pl.multiple_of(i * PAGE_SIZE, PAGE_SIZE), PAGE_SIZE), :] = tile.T
                return carry
            lax.fori_loop(0, n_pages, fill, 0)
        return xrow_ref[pl.ds(s, nbp, stride=CMP_STRIDE), :]

    kcv = _cmp_tokens(load_rows, nbp, wbd_ref, pe_ref, b1_ref, w2_ref)
    q8 = (q_ref[0] * (HEAD_DIM ** -0.5)).astype(BF16)
    valid = lax.broadcasted_iota(jnp.int32, (1, nbp), 1) < n_cmp
    o_cmp = jnp.zeros((HEAD_ROWS, HEAD_DIM), F32)
    psum = jnp.zeros((HEAD_ROWS, nbp), F32)
    row = lax.broadcasted_iota(jnp.int32, (HEAD_ROWS, 1), 0)
    for h in range(N_KV_HEADS):
        kc = kcv[:, h * HEAD_DIM:(h + 1) * HEAD_DIM].astype(BF16)
        vc = kcv[:, (N_KV_HEADS + h) * HEAD_DIM:(N_KV_HEADS + h + 1) * HEAD_DIM].astype(BF16)
        p = _softmax_rows(jnp.where(valid, _dot_nt(q8, kc), NEG_BIG))
        mine = _head_rows_of(h)
        o_cmp = jnp.where(mine, _dot(p.astype(BF16), vc), o_cmp)
        ph = jnp.sum(jnp.where(mine, p, 0.0), axis=0, keepdims=True)
        psum = jnp.where(row == h, ph, psum)
    ocmp_ref[0] = o_cmp
    p_hi = psum.astype(BF16)
    p_lo = (psum - p_hi.astype(F32)).astype(BF16)
    imp_ref[0] = _dot_nt(p_hi, mselt_ref[...]) + _dot_nt(p_lo, mselt_ref[...])


def _cmp_sample(page_table, q8, cache_t, wbd, pe_rows, b1, w2, mselt, n_cmp):
    db, n_pages = page_table.shape
    consts = (wbd, pe_rows, b1, w2, mselt)
    grid_spec = pltpu.PrefetchScalarGridSpec(
        num_scalar_prefetch=1,
        grid=(db,),
        in_specs=[pl.BlockSpec((1, HEAD_ROWS, HEAD_DIM), lambda i, pt: (i, 0, 0)),
                  pl.BlockSpec(memory_space=pl.ANY)]
        + [pl.BlockSpec(c.shape, lambda i, pt, nd=c.ndim: (0,) * nd, pipeline_mode=pl.Buffered(1)) for c in consts],
        out_specs=[pl.BlockSpec((1, HEAD_ROWS, HEAD_DIM), lambda i, pt: (i, 0, 0)),
                   pl.BlockSpec((1, HEAD_ROWS, SEL_LANES), lambda i, pt: (i, 0, 0))],
        scratch_shapes=[pltpu.VMEM((2, n_pages, 2, N_KV_HEADS, HEAD_DIM, PAGE_SIZE), F32),
                        pltpu.VMEM((n_pages * PAGE_SIZE, HD2), F32), pltpu.SemaphoreType.DMA((2,))],
    )
    return pl.pallas_call(
        functools.partial(_cmp_sample_kernel, n_cmp=n_cmp),
        grid_spec=grid_spec,
        out_shape=[jax.ShapeDtypeStruct((db, HEAD_ROWS, HEAD_DIM), F32),
                   jax.ShapeDtypeStruct((db, HEAD_ROWS, SEL_LANES), F32)],
        compiler_params=pltpu.CompilerParams(dimension_semantics=("arbitrary",), vmem_limit_bytes=VMEM_LIMIT),
        name="nsa_cmp_sample",
    )(page_table, q8, cache_t, *consts)


N_PAST_PICKS = SEL_TOPK - 1


def _topk_sample_kernel(imp_ref, idx_ref, *, past_blk):
    x = imp_ref[...].T
    jj = lax.broadcasted_iota(jnp.int32, x.shape, 0)
    x = jnp.where((jj == 0) | (jj == past_blk - 1), IMP_FORCED, x)
    x = jnp.where(jj < past_blk, x, IMP_BLOCKED)
    _, picks = _take_top(x, N_PAST_PICKS)
    picks = picks + [jnp.zeros_like(picks[0])] * (idx_ref.shape[0] - N_PAST_PICKS)
    idx_ref[...] = jnp.concatenate(picks, axis=0).astype(jnp.int32)


def _topk_sample(imp2, past_blk):
    nq = imp2.shape[0]
    assert N_PAST_PICKS <= past_blk <= SEL_LANES
    return pl.pallas_call(
        functools.partial(_topk_sample_kernel, past_blk=past_blk),
        out_shape=jax.ShapeDtypeStruct((SEL_TOPK, nq), jnp.int32),
        name="nsa_topk_sample",
    )(imp2)


SUB_PER_PAGE = PAGE_SIZE // SEL_BLOCK
SUB_SHIFT = SUB_PER_PAGE.bit_length() - 1
SEL_SHIFT_IN_PAGE = SEL_BLOCK.bit_length() - 1
N_SEL_COLS = N_PAST_PICKS * PAGE_SIZE


def _attend_one_token(q8, q8f, kt, vt, valid, k_new, v_new):
    s = jnp.where(valid, _dot(q8, kt.astype(BF16)), NEG_BIG)
    s_new = jnp.sum(q8f * k_new, axis=-1, keepdims=True)
    m = jnp.maximum(jnp.max(s, axis=-1, keepdims=True), s_new)
    e = jnp.exp(s - m)
    e_new = jnp.exp(s_new - m)
    norm = jnp.maximum(jnp.sum(e, axis=-1, keepdims=True) + e_new, TINY)
    return (_dot_nt(e.astype(BF16), vt.astype(BF16)) + e_new * v_new) / norm


def _sel_win_sample_kernel(idx_ref, pt_ref, q_ref, gate_ref, ocmp_ref, kvs_ref, kvw_ref, win_ref, slc_ref,
                           att_ref, newwin_ref, gbuf_ref, sem_ref):
    b = pl.program_id(0)
    nb_total = pl.num_programs(0)
    w_buf = win_ref.shape[-1]

    def block_copies(bb, slot, h, k):
        j = idx_ref[k, bb * N_KV_HEADS + h]
        page = pt_ref[bb, j >> SUB_SHIFT]
        return [pltpu.make_async_copy(slc_ref.at[page, 0, c, h],
                                      gbuf_ref.at[slot, h, c, :, pl.ds(k * PAGE_SIZE, PAGE_SIZE)],
                                      sem_ref.at[slot]) for c in range(2)]

    def start_all(bb, slot):
        for h in range(N_KV_HEADS):
            for k in range(N_PAST_PICKS):
                for cp in block_copies(bb, slot, h, k):
                    cp.start()

    def wait_all(bb, slot):
        for h in range(N_KV_HEADS):
            for k in range(N_PAST_PICKS):
                for cp in block_copies(bb, slot, h, k):
                    cp.wait()

    slot = b % 2

    @pl.when(b == 0)
    def _():
        start_all(0, 0)

    @pl.when(b + 1 < nb_total)
    def _():
        start_all(b + 1, 1 - slot)

    wait_all(b, slot)

    q8f = q_ref[0] * (HEAD_DIM ** -0.5)
    q8 = q8f.astype(BF16)
    page_lane = lax.broadcasted_iota(jnp.int32, (1, PAGE_SIZE), 1)
    win_lane = lax.broadcasted_iota(jnp.int32, (1, w_buf), 1)
    win_valid = w_buf - win_lane < WINDOW
    o_sel = jnp.zeros((HEAD_ROWS, HEAD_DIM), F32)
    o_win = jnp.zeros((HEAD_ROWS, HEAD_DIM), F32)
    for h in range(N_KV_HEADS):
        kcols = slice(h * HEAD_DIM, (h + 1) * HEAD_DIM)
        vcols = slice((N_KV_HEADS + h) * HEAD_DIM, (N_KV_HEADS + h + 1) * HEAD_DIM)
        mine = _head_rows_of(h)
        sel_valid = jnp.concatenate(
            [(page_lane >> SEL_SHIFT_IN_PAGE) == (idx_ref[k, b * N_KV_HEADS + h] & (SUB_PER_PAGE - 1))
             for k in range(N_PAST_PICKS)], axis=1)
        o_sel = jnp.where(mine, _attend_one_token(q8, q8f, gbuf_ref[slot, h, 0], gbuf_ref[slot, h, 1], sel_valid,
                                                  kvs_ref[0][:, kcols], kvs_ref[0][:, vcols]), o_sel)
        o_win = jnp.where(mine, _attend_one_token(q8, q8f, win_ref[0, 0, 0, h], win_ref[0, 0, 1, h], win_valid,
                                                  kvw_ref[0][:, kcols], kvw_ref[0][:, vcols]), o_win)
    gates = jax.nn.sigmoid(gate_ref[0])
    att_ref[0] = gates[:, 0:1] * ocmp_ref[0] + gates[:, 1:2] * o_sel + gates[:, 2:3] * o_win

    new_col = jnp.concatenate([kvw_ref[0], jnp.zeros((LANES - 1, KV_DIM), F32)], axis=0).T
    tile_lane = lax.broadcasted_iota(jnp.int32, (HEAD_DIM, w_buf), 1)
    for c in range(2):
        for h in range(N_KV_HEADS):
            lo = (c * N_KV_HEADS + h) * HEAD_DIM
            newwin_ref[0, 0, c, h] = jnp.where(tile_lane == w_buf - 1, new_col[lo:lo + HEAD_DIM, 0:1],
                                               pltpu.roll(win_ref[0, 0, c, h], w_buf - 1, 1))


def _sel_win_sample(idx, page_table, q8, gate8, o_cmp, kvs_new, kvw_new, win_t, slc_t):
    db = page_table.shape[0]
    w_buf = win_t.shape[-1]
    assert w_buf == WINDOW and win_t.shape[1] == 1
    per_tok = lambda shape: pl.BlockSpec((1,) + shape[1:], lambda i, idx, pt, nd=len(shape): (i,) + (0,) * (nd - 1))
    grid_spec = pltpu.PrefetchScalarGridSpec(
        num_scalar_prefetch=2,
        grid=(db,),
        in_specs=[per_tok(q8.shape), per_tok(gate8.shape), per_tok(o_cmp.shape), per_tok(kvs_new.shape),
                  per_tok(kvw_new.shape), per_tok(win_t.shape), pl.BlockSpec(memory_space=pl.ANY)],
        out_specs=[per_tok(o_cmp.shape), per_tok(win_t.shape)],
        scratch_shapes=[pltpu.VMEM((2, N_KV_HEADS, 2, HEAD_DIM, N_SEL_COLS), F32), pltpu.SemaphoreType.DMA((2,))],
    )
    return pl.pallas_call(
        _sel_win_sample_kernel,
        grid_spec=grid_spec,
        out_shape=[jax.ShapeDtypeStruct(o_cmp.shape, F32), jax.ShapeDtypeStruct(win_t.shape, F32)],
        compiler_params=pltpu.CompilerParams(dimension_semantics=("arbitrary",), vmem_limit_bytes=VMEM_LIMIT),
        name="nsa_sel_win_sample",
    )(idx, page_table, q8, gate8, o_cmp, kvs_new, kvw_new, win_t, slc_t)


SSD_BT = 8


def _ssd_sample_kernel(xbc_ref, cst_ref, z_ref, small_ref, h0_ref, convw_ref, convb_ref, dtb_ref, alog_ref,
                       dskip_ref, norm_ref, y_ref, hnew_ref, ys_ref):
    conv = convb_ref[...] + xbc_ref[...] * convw_ref[CONV_W - 1:CONV_W, :]
    for k in range(CONV_W - 1):
        conv = conv + cst_ref[k] * convw_ref[k:k + 1, :]
    xc = _silu(conv)
    xs = xc[:, :D_INNER]
    dt = jax.nn.softplus(small_ref[...] + dtb_ref[...])
    dt_e = _expand_heads(dt)
    decay_e = jnp.exp(dt_e * _expand_heads(-jnp.exp(alog_ref[...])))
    xd = xs * dt_e
    fill = jnp.zeros((LANES - SSD_BT, D_INNER), F32)
    xd_t = jnp.concatenate([xd, fill], axis=0).T
    decay_t = jnp.concatenate([decay_e, fill], axis=0).T
    lane = lax.broadcasted_iota(jnp.int32, (1, D_INNER), 1)
    for i in range(SSD_BT):
        bsel = jnp.concatenate(
            [jnp.broadcast_to(xc[i:i + 1, D_INNER + g * D_STATE:D_INNER + (g + 1) * D_STATE], (GROUP_W, D_STATE))
             for g in range(N_GROUPS)], axis=0)
        h0 = h0_ref[i].reshape(D_INNER, D_STATE)
        hn = decay_t[:, i:i + 1] * h0 + xd_t[:, i:i + 1] * bsel
        hnew_ref[i] = hn.reshape(SSM_HEADS, SSM_HEAD_DIM, D_STATE)
        c8 = jnp.concatenate(
            [xc[i:i + 1, D_INNER + (N_GROUPS + g) * D_STATE:D_INNER + (N_GROUPS + g + 1) * D_STATE]
             for g in range(N_GROUPS)] + [jnp.zeros((HEAD_ROWS - N_GROUPS, D_STATE), F32)], axis=0)
        y8 = _dot_nt(c8.astype(BF16), hn.astype(BF16))
        ys_ref[i:i + 1, :] = jnp.where(lane < GROUP_W, y8[0:1], y8[1:2])
    y = ys_ref[...] + dskip_ref[...] * xs
    y_ref[...] = _grouped_norm_gate(y, z_ref[...], norm_ref[...])


def _ssd_sample(xbc, conv_state_t, z, small, h0, conv_w, conv_b, dt_bias, a_log, d_skip, ssm_norm):
    db = xbc.shape[0]
    assert db % SSD_BT == 0 and N_GROUPS == 2
    params = (conv_w, conv_b.reshape(1, CONV_DIM), _pad_small(dt_bias), _pad_small(a_log),
              jnp.repeat(d_skip.astype(F32), SSM_HEAD_DIM).reshape(1, D_INNER), ssm_norm.reshape(1, D_INNER))
    rows = lambda w: pl.BlockSpec((SSD_BT, w), lambda i: (i, 0))
    state_spec = pl.BlockSpec((SSD_BT, SSM_HEADS, SSM_HEAD_DIM, D_STATE), lambda i: (i, 0, 0, 0))
    return pl.pallas_call(
        _ssd_sample_kernel,
        grid=(db // SSD_BT,),
        in_specs=[rows(CONV_DIM), pl.BlockSpec((CONV_W - 1, SSD_BT, CONV_DIM), lambda i: (0, i, 0)), rows(D_INNER),
                  rows(SMALL_W), state_spec] + [_const_spec(p.shape) for p in params],
        out_specs=[rows(D_INNER), state_spec],
        out_shape=[jax.ShapeDtypeStruct((db, D_INNER), F32), jax.ShapeDtypeStruct(h0.shape, F32)],
        scratch_shapes=[pltpu.VMEM((SSD_BT, D_INNER), F32)],
        compiler_params=pltpu.CompilerParams(dimension_semantics=("arbitrary",), vmem_limit_bytes=VMEM_LIMIT),
        name="ssd_sample",
    )(xbc, conv_state_t, z, small, h0, *params)


PROMPT_TM = 512


def _ffn_weights(w_gate, w_up, w_down):
    chunked = lambda w: jnp.transpose(w.astype(BF16).reshape(D_MODEL, N_FF_CHUNKS, FF_CHUNK), (1, 0, 2))
    return chunked(w_gate), chunked(w_up), w_down.astype(BF16).reshape(N_FF_CHUNKS, FF_CHUNK, D_MODEL)


def _in_proj_weight(w_in):
    cuts = np.cumsum([ATT_DIM, KV_DIM, KV_DIM, KV_DIM, N_GATE, D_INNER, CONV_DIM]).tolist()
    q, kvc, kvs, kvw, g, z, xbc, dt = jnp.split(w_in, cuts, axis=-1)
    pad = jnp.zeros((D_MODEL, SMALL_W - N_GATE - SSM_HEADS), w_in.dtype)
    return jnp.concatenate([q, kvc, kvs, kvw, z, xbc, g, dt, pad], axis=-1).astype(BF16)


def kernel(x_prompt, x_sample, cache_cmp_kv, cache_slc_kv, cache_win_kv, state_conv, state_ssm, page_table,
           p_prompt, p_sample, ffn1_norm, ffn1_w_gate, ffn1_w_up, ffn1_w_down, mix_norm, w_in,
           cmp_w1, cmp_pe, cmp_b1, cmp_w2, conv_w, conv_b, dt_bias, a_log, d_skip, ssm_norm, w_out,
           ffn2_norm, ffn2_w_gate, ffn2_w_up, ffn2_w_down, ple_norm, w_ple_gate, w_ple, final_norm):
    b, l, _ = x_prompt.shape
    db, t_new, _ = x_sample.shape
    depth = ffn1_norm.shape[0]
    n_pool = cache_cmp_kv.shape[0]
    n_pages = page_table.shape[1]
    past = n_pages * PAGE_SIZE
    assert depth == 1 and t_new == 1 and past >= WINDOW and (b * l) % PROMPT_TM == 0
    i = 0
    row = lambda v: v.reshape(1, -1).astype(F32)

    ffn1 = _ffn_weights(ffn1_w_gate[i], ffn1_w_up[i], ffn1_w_down[i])
    stage1_w = (row(ffn1_norm[i]),) + ffn1 + (row(mix_norm[i]), _in_proj_weight(w_in[i]))
    cmp_w = _cmp_weights(cmp_w1[i], cmp_pe[i], cmp_b1[i], cmp_w2[i])
    ssm_w = (conv_w[i], conv_b[i], dt_bias[i], a_log[i], d_skip[i], ssm_norm[i])
    w_o = w_out[i].astype(BF16)
    stage3_w = (w_o[:ATT_DIM], w_o[ATT_DIM:], row(ffn2_norm[i])) + _ffn_weights(ffn2_w_gate[i], ffn2_w_up[i], ffn2_w_down[i]) + (
        row(ple_norm[i]), w_ple_gate[i].astype(BF16), w_ple[i].astype(BF16), row(final_norm))
    kv6 = lambda a, lead, rows: a.reshape(lead, 1, rows, 2, N_KV_HEADS, HEAD_DIM)

    h, q, kvc, kvs, kvw, z, xbc, small = _ffn_inproj(x_prompt.reshape(b * l, D_MODEL), *stage1_w, PROMPT_TM)
    per_b = lambda a: a.reshape(b, l, a.shape[-1])
    kcv, kcv_t = _cmp_prompt(per_b(kvc), *cmp_w)
    att = _nsa_prompt(per_b(q), per_b(small), kcv, kcv_t, per_b(kvs), per_b(kvw))
    ssm, ssm_state = _ssd_prompt(per_b(xbc), per_b(z), per_b(small), *ssm_w)
    y_prompt = _out_ffn(h, att.reshape(b * l, ATT_DIM), ssm.reshape(b * l, D_INNER),
                        p_prompt[i].reshape(b * l, PLE_DIM), stage3_w, PROMPT_TM).reshape(b, l, D_MODEL)
    keep = min(WINDOW, l)
    new_cmp_p = kv6(kvc, b, l)
    new_slc_p = kv6(kvs, b, l)
    new_win_p = kv6(per_b(kvw)[:, l - keep:], b, keep)
    new_conv_p = per_b(xbc)[:, l - (CONV_W - 1):].reshape(b, 1, CONV_W - 1, CONV_DIM)
    new_ssm_p = ssm_state.reshape(b, 1, SSM_HEADS, SSM_HEAD_DIM, D_STATE)

    hs, qs, kvc_s, kvs_s, kvw_s, z_s, xbc_s, small_s = _ffn_inproj(x_sample.reshape(db, D_MODEL), *stage1_w, db)
    q8 = qs.reshape(db, N_HEADS, HEAD_DIM)
    n_cmp_s = (past + t_new - CMP_LEN) // CMP_STRIDE + 1
    past_blk = past // SEL_BLOCK
    mselt_s = _cmp_to_sel_t(n_pages * BLOCKS_PER_PAGE, n_cmp_s, past_blk)
    row_minor = lambda a: jnp.transpose(a, (0, 1, 3, 4, 5, 2))
    o_cmp, imp = _cmp_sample(page_table, q8, row_minor(cache_cmp_kv), *cmp_w, mselt_s, n_cmp_s)
    n_query = db * N_KV_HEADS
    imp2 = jnp.pad(imp[:, :N_KV_HEADS].reshape(n_query, SEL_LANES), ((0, -n_query % LANES), (0, 0)))
    idx = _topk_sample(imp2, past_blk)
    att_s, new_win_t = _sel_win_sample(
        idx, page_table, q8, small_s[:, :N_GATE].reshape(db, N_HEADS, 3), o_cmp,
        kvs_s.reshape(db, 1, KV_DIM), kvw_s.reshape(db, 1, KV_DIM),
        row_minor(cache_win_kv), row_minor(cache_slc_kv))
    new_win_s = jnp.transpose(new_win_t, (0, 1, 5, 2, 3, 4))
    ssm_s, ssm_state_s = _ssd_sample(xbc_s, jnp.transpose(state_conv[:, i], (1, 0, 2)), z_s, small_s,
                                     state_ssm[:, i], *ssm_w)
    y_sample = _out_ffn(hs, att_s.reshape(db, ATT_DIM), ssm_s, p_sample[i].reshape(db, PLE_DIM),
                        stage3_w, db).reshape(db, 1, D_MODEL)
    new_conv_s = jnp.concatenate([state_conv[:, i, 1:], xbc_s[:, None]], axis=1)[:, None]

    return (y_prompt, y_sample, new_cmp_p, new_slc_p, new_win_p, new_conv_p, new_ssm_p,
            kv6(kvc_s, db, 1), kv6(kvs_s, db, 1), new_win_s, new_conv_s, ssm_state_s[:, None])
```

```python
import functools

import jax
import jax.numpy as jnp
import numpy as np
from jax import lax
from jax.experimental import pallas as pl
from jax.experimental.pallas import tpu as pltpu

F32 = jnp.float32
BF16 = jnp.bfloat16

D_MODEL = 1024
N_HEADS = 8
N_KV_HEADS = 2
HEAD_DIM = 64
GQA = N_HEADS // N_KV_HEADS
ATT_DIM = N_HEADS * HEAD_DIM
KV_DIM = 2 * N_KV_HEADS * HEAD_DIM
CMP_LEN = 32
CMP_STRIDE = 16
CMP_HID = 128
SEL_BLOCK = 64
SEL_TOPK = 16
WINDOW = 512
Q_BLOCK = 128
SSM_HEADS = 8
SSM_HEAD_DIM = 64
D_INNER = SSM_HEADS * SSM_HEAD_DIM
N_GROUPS = 2
D_STATE = 128
CONV_W = 4
CONV_DIM = D_INNER + 2 * N_GROUPS * D_STATE
SSD_CHUNK = 128
D_FF = 2816
PLE_DIM = 256
PAGE_SIZE = 128
EPS = 1e-6

LANES = 128
FF_CHUNK = 256
N_FF_CHUNKS = D_FF // FF_CHUNK
SMALL_W = LANES
N_GATE = 3 * N_HEADS
IN_PROJ_PAD = ATT_DIM + 3 * KV_DIM + D_INNER + CONV_DIM + SMALL_W
VMEM_LIMIT = 56 * 1024 * 1024
NEG_BIG = -1e30
IMP_FORCED = 3e38
IMP_BLOCKED = -1e38
IMP_TAKEN = -3e38


def _dot(a, b):
    return jnp.dot(a, b, preferred_element_type=F32)


def _dot_nt(a, b):
    return lax.dot_general(a, b, (((1,), (1,)), ((), ())), preferred_element_type=F32)


def _rms(x, g):
    return x * lax.rsqrt(jnp.mean(x * x, axis=-1, keepdims=True) + EPS) * g


def _silu(x):
    return x * jax.nn.sigmoid(x)


def _const_spec(shape):
    nd = len(shape)
    return pl.BlockSpec(shape, lambda *_: (0,) * nd, pipeline_mode=pl.Buffered(1))


def _row_spec(tm, width):
    return pl.BlockSpec((tm, width), lambda i: (i, 0))


def _swiglu_half_step(x, g_ref, wg_ref, wu_ref, wd_ref, acc_ref):
    xn = _rms(x, g_ref[...]).astype(BF16)
    acc_ref[...] = jnp.zeros_like(acc_ref)

    def body(c, carry):
        a = _dot(xn, wg_ref[c])
        b = _dot(xn, wu_ref[c])
        hm = (_silu(a) * b).astype(BF16)
        acc_ref[...] += _dot(hm, wd_ref[c])
        return carry

    lax.fori_loop(0, N_FF_CHUNKS, body, 0)
    return x + 0.5 * acc_ref[...]


_IN_SEGS = (ATT_DIM, KV_DIM, KV_DIM, KV_DIM, D_INNER, CONV_DIM, SMALL_W)


def _ffn_inproj_kernel(x_ref, g1_ref, wg_ref, wu_ref, wd_ref, gm_ref, win_ref,
                       h_ref, q_ref, kvc_ref, kvs_ref, kvw_ref, z_ref, xbc_ref, small_ref, acc_ref):
    h = _swiglu_half_step(x_ref[...], g1_ref, wg_ref, wu_ref, wd_ref, acc_ref)
    h_ref[...] = h
    hn = _rms(h, gm_ref[...]).astype(BF16)
    off = 0
    for out_ref, width in zip((q_ref, kvc_ref, kvs_ref, kvw_ref, z_ref, xbc_ref, small_ref), _IN_SEGS):
        out_ref[...] = _dot(hn, win_ref[:, off:off + width])
        off += width


def _ffn_inproj(x, g1, wg, wu, wd, gm, win, tm):
    n = x.shape[0]
    outs = [jax.ShapeDtypeStruct((n, D_MODEL), F32)] + [jax.ShapeDtypeStruct((n, w), F32) for w in _IN_SEGS]
    return pl.pallas_call(
        _ffn_inproj_kernel,
        grid=(n // tm,),
        in_specs=[_row_spec(tm, D_MODEL), _const_spec(g1.shape), _const_spec(wg.shape), _const_spec(wu.shape),
                  _const_spec(wd.shape), _const_spec(gm.shape), _const_spec(win.shape)],
        out_specs=[_row_spec(tm, D_MODEL)] + [_row_spec(tm, w) for w in _IN_SEGS],
        out_shape=outs,
        scratch_shapes=[pltpu.VMEM((tm, D_MODEL), F32)],
        compiler_params=pltpu.CompilerParams(dimension_semantics=("arbitrary",), vmem_limit_bytes=VMEM_LIMIT),
        name="ffn1_inproj",
    )(x, g1, wg, wu, wd, gm, win)


HD2 = N_KV_HEADS * HEAD_DIM
CMP_R = CMP_LEN // CMP_STRIDE
CMP_PROJ = N_KV_HEADS * CMP_R * CMP_HID
PE_ROWS = 16
CMP_K = CMP_STRIDE * HD2


def _cmp_weights(cmp_w1, cmp_pe, cmp_b1, cmp_w2):
    w1r = cmp_w1.reshape(2, CMP_R, CMP_STRIDE, HEAD_DIM, CMP_HID)
    per = cmp_pe.reshape(2, CMP_R, CMP_STRIDE, HEAD_DIM)
    wbd = jnp.zeros((2, CMP_STRIDE, N_KV_HEADS, HEAD_DIM, N_KV_HEADS, CMP_R, CMP_HID), F32)
    blk = jnp.transpose(w1r, (0, 2, 3, 1, 4))
    for h in range(N_KV_HEADS):
        wbd = wbd.at[:, :, h, :, h, :, :].set(blk)
    wbd = wbd.reshape(2, CMP_K, CMP_PROJ).astype(BF16)
    pe_rows = jnp.broadcast_to(per[:, :, :, None, :], (2, CMP_R, CMP_STRIDE, N_KV_HEADS, HEAD_DIM))
    pe_rows = jnp.concatenate([pe_rows.reshape(2, CMP_R, CMP_K), jnp.zeros((2, PE_ROWS - CMP_R, CMP_K), F32)], axis=1)
    return wbd, pe_rows.astype(BF16), cmp_b1, cmp_w2.astype(BF16)


def _cmp_tokens(load_rows, nb, lhs_ref, wbd_ref, pe_ref, b1_ref, w2_ref):
    outs = []
    for c in range(2):
        for s in range(CMP_STRIDE):
            lhs_ref[0:nb, s * HD2:(s + 1) * HD2] = load_rows(c, s).astype(BF16)
        lhs_ref[nb:nb + PE_ROWS, :] = pe_ref[c]
        acc = _dot(lhs_ref[...], wbd_ref[c])
        proj, pe_proj = acc[:nb], acc[nb:]
        for h in range(N_KV_HEADS):
            lo = h * CMP_R * CMP_HID
            p0 = proj[:, lo:lo + CMP_HID]
            p1 = pltpu.roll(proj[:, lo + CMP_HID:lo + 2 * CMP_HID], nb - 1, 0)
            pe_add = pe_proj[0:1, lo:lo + CMP_HID] + pe_proj[1:2, lo + CMP_HID:lo + 2 * CMP_HID]
            hid = _silu(p0 + p1 + pe_add + b1_ref[c:c + 1, :])
            outs.append(_dot(hid.astype(BF16), w2_ref[c]))
    return jnp.concatenate(outs, axis=1)


def _cmp_prompt_kernel(xk_ref, xv_ref, wbd_ref, pe_ref, b1_ref, w2_ref, out_ref, out_t_ref, lhs_ref):
    nb = out_ref.shape[1]
    load_rows = lambda c, s: (xk_ref, xv_ref)[c][0, pl.ds(s, nb, stride=CMP_STRIDE), :]
    tokens = _cmp_tokens(load_rows, nb, lhs_ref, wbd_ref, pe_ref, b1_ref, w2_ref)
    out_ref[0] = tokens
    out_t_ref[0] = tokens.T


def _cmp_prompt(kvc, wbd, pe_rows, b1, w2):
    b, l, _ = kvc.shape
    nb = l // CMP_STRIDE
    return pl.pallas_call(
        _cmp_prompt_kernel,
        grid=(b,),
        in_specs=[pl.BlockSpec((1, l, HD2), lambda i: (i, 0, 0)), pl.BlockSpec((1, l, HD2), lambda i: (i, 0, 1)),
                  _const_spec(wbd.shape), _const_spec(pe_rows.shape), _const_spec(b1.shape), _const_spec(w2.shape)],
        out_specs=[pl.BlockSpec((1, nb, KV_DIM), lambda i: (i, 0, 0)), pl.BlockSpec((1, KV_DIM, nb), lambda i: (i, 0, 0))],
        out_shape=[jax.ShapeDtypeStruct((b, nb, KV_DIM), F32), jax.ShapeDtypeStruct((b, KV_DIM, nb), F32)],
        scratch_shapes=[pltpu.VMEM((nb + PE_ROWS, CMP_K), BF16)],
        compiler_params=pltpu.CompilerParams(dimension_semantics=("arbitrary",), vmem_limit_bytes=VMEM_LIMIT),
        name="nsa_cmp_prompt",
    )(kvc, kvc, wbd, pe_rows, b1, w2)


SEL_TILE = 512
COLS = GQA * Q_BLOCK
SEL_LANES = 128
AUG = SEL_LANES + 2 * HEAD_DIM
TINY = float(np.finfo(np.float32).tiny)
SEL_SHIFT = SEL_BLOCK.bit_length() - 1


def _softmax_rows(s):
    m = jnp.max(s, axis=-1, keepdims=True)
    m = jnp.where(m > 0.5 * NEG_BIG, m, 0.0)
    e = jnp.exp(s - m)
    return e / jnp.maximum(jnp.sum(e, axis=-1, keepdims=True), TINY)


def _select_blocks(imp_t, t_lane):
    jj = lax.broadcasted_iota(jnp.int32, imp_t.shape, 0)
    cb = t_lane >> SEL_SHIFT
    forced = (jj == 0) | (jj == cb) | (jj == cb - 1)
    causal = (jj << SEL_SHIFT) <= t_lane
    x = jnp.where(forced, IMP_FORCED, imp_t)
    x = jnp.where(causal, x, IMP_BLOCKED)
    sel, _ = _take_top(x, SEL_TOPK)
    return sel


def _take_top(x, k):
    nj = x.shape[0]
    jf = lax.broadcasted_iota(jnp.int32, x.shape, 0).astype(F32)
    sel = jnp.zeros_like(x)
    picks = []
    for _ in range(k):
        m = jnp.max(x, axis=0, keepdims=True)
        first = jnp.min(jnp.where(x == m, jf, float(nj)), axis=0, keepdims=True)
        hit = jf == first
        sel = jnp.where(hit, 1.0, sel)
        x = jnp.where(hit, IMP_TAKEN, x)
        picks.append(first)
    return sel, picks


def _softmax_cols(s):
    m = jnp.max(s, axis=0, keepdims=True)
    m = jnp.where(m > 0.5 * NEG_BIG, m, 0.0)
    e = jnp.exp(s - m)
    return e, 1.0 / jnp.maximum(jnp.sum(e, axis=0, keepdims=True), TINY)


def _dot_tn(a, b):
    return lax.dot_general(a, b, (((0,), (0,)), ((), ())), preferred_element_type=F32)


def _nsa_prompt_kernel(q_ref, small_ref, kcv_ref, kcvt_ref, kaug_ref, vaugt_ref, kwin_ref, mselt_ref, out_ref,
                       lt_ref, acc_ref, ot_ref, sa_ref, sb_ref):
    n = pl.program_id(1)
    t0 = n * Q_BLOCK
    nb = kcv_ref.shape[1]
    col_t = t0 + (lax.broadcasted_iota(jnp.int32, (1, COLS), 1) & (Q_BLOCK - 1))
    lane_t = t0 + lax.broadcasted_iota(jnp.int32, (1, Q_BLOCK), 1)
    gates_t = jax.nn.sigmoid(small_ref[0]).T
    q_t = (q_ref[0] * (HEAD_DIM ** -0.5)).T.astype(BF16)
    c_end = lax.broadcasted_iota(jnp.int32, (nb, 1), 0) * CMP_STRIDE + (CMP_LEN - 1)
    o_cmp_t = []

    for h in range(N_KV_HEADS):
        for g in range(GQA):
            lo = (h * GQA + g) * HEAD_DIM
            lt_ref[h, SEL_LANES:SEL_LANES + HEAD_DIM, g * Q_BLOCK:(g + 1) * Q_BLOCK] = q_t[lo:lo + HEAD_DIM, :]
        lt_ref[h, SEL_LANES + HEAD_DIM:, :] = jnp.zeros((HEAD_DIM, COLS), BF16)
        qh_t = lt_ref[h, SEL_LANES:SEL_LANES + HEAD_DIM, :]

        kc = kcv_ref[0, :, h * HEAD_DIM:(h + 1) * HEAD_DIM].astype(BF16)
        vc_t = kcvt_ref[0, (N_KV_HEADS + h) * HEAD_DIM:(N_KV_HEADS + h + 1) * HEAD_DIM, :].astype(BF16)
        e_t, r_t = _softmax_cols(jnp.where(c_end <= col_t, _dot(kc, qh_t), NEG_BIG))
        o_cmp_t.append(_dot(vc_t, e_t.astype(BF16)) * r_t)

        psum = e_t[:, 0:Q_BLOCK] * r_t[:, 0:Q_BLOCK]
        for g in range(1, GQA):
            psum = psum + e_t[:, g * Q_BLOCK:(g + 1) * Q_BLOCK] * r_t[:, g * Q_BLOCK:(g + 1) * Q_BLOCK]
        p_hi = psum.astype(BF16)
        p_lo = (psum - p_hi.astype(F32)).astype(BF16)
        imp_t = _dot(mselt_ref[...], p_hi) + _dot(mselt_ref[...], p_lo)
        neg = jnp.where(_select_blocks(imp_t, lane_t) > 0.0, 0.0, NEG_BIG).astype(BF16)
        for g in range(GQA):
            lt_ref[h, 0:SEL_LANES, g * Q_BLOCK:(g + 1) * Q_BLOCK] = neg

    acc_ref[...] = jnp.zeros(acc_ref.shape, F32)
    n_pairs = (t0 + Q_BLOCK + 2 * SEL_TILE - 1) // (2 * SEL_TILE)

    def scores(s_ref, kt):
        k0 = pl.multiple_of(kt * SEL_TILE, SEL_TILE)
        for h in range(N_KV_HEADS):
            s_ref[h] = _dot(kaug_ref[0, h, pl.ds(k0, SEL_TILE), :], lt_ref[h])

    def consume(s_ref, kt, ms, causal_mask):
        out = []
        for h in range(N_KV_HEADS):
            s = s_ref[h]
            if causal_mask:
                kpos = kt * SEL_TILE + lax.broadcasted_iota(jnp.int32, (SEL_TILE, 1), 0)
                s = jnp.where(kpos <= col_t, s, NEG_BIG)
            m_new = jnp.maximum(ms[h], jnp.max(s, axis=0, keepdims=True))
            pe = jnp.exp(s - m_new).astype(BF16)
            acc_ref[h] = jnp.exp(ms[h] - m_new) * acc_ref[h] + _dot(vaugt_ref[0, h, kt], pe)
            out.append(m_new)
        return tuple(out)

    def pair(i, ms, causal_mask):
        scores(sb_ref, 2 * i + 1)
        ms = consume(sa_ref, 2 * i, ms, causal_mask)
        if not causal_mask:
            scores(sa_ref, 2 * i + 2)
        return consume(sb_ref, 2 * i + 1, ms, causal_mask)

    scores(sa_ref, 0)
    m_init = (jnp.full((1, COLS), NEG_BIG, F32),) * N_KV_HEADS
    ms = lax.fori_loop(0, n_pairs - 1, lambda i, ms: pair(i, ms, False), m_init)
    pair(n_pairs - 1, ms, True)

    w0 = pl.multiple_of(t0, Q_BLOCK)
    kpos = t0 - WINDOW + lax.broadcasted_iota(jnp.int32, (WINDOW + Q_BLOCK, 1), 0)
    dpos = col_t - kpos
    win_ok = (dpos >= 0) & (dpos < WINDOW) & (kpos >= 0)
    for h in range(N_KV_HEADS):
        acc = acc_ref[h]
        o_sel_t = acc[:HEAD_DIM] / jnp.maximum(acc[HEAD_DIM:HEAD_DIM + 1], TINY)

        kw = kwin_ref[0, pl.ds(w0, WINDOW + Q_BLOCK), h * HEAD_DIM:(h + 1) * HEAD_DIM]
        vw = kwin_ref[0, pl.ds(w0, WINDOW + Q_BLOCK), (N_KV_HEADS + h) * HEAD_DIM:(N_KV_HEADS + h + 1) * HEAD_DIM]
        qh_t = lt_ref[h, SEL_LANES:SEL_LANES + HEAD_DIM, :]
        e_t, r_t = _softmax_cols(jnp.where(win_ok, _dot(kw, qh_t), NEG_BIG))
        o_win_t = _dot_tn(vw, e_t.astype(BF16)) * r_t

        for g in range(GQA):
            hd = h * GQA + g
            cols = slice(g * Q_BLOCK, (g + 1) * Q_BLOCK)
            ot_ref[hd * HEAD_DIM:(hd + 1) * HEAD_DIM, :] = (
                gates_t[3 * hd:3 * hd + 1, :] * o_cmp_t[h][:, cols] + gates_t[3 * hd + 1:3 * hd + 2, :] * o_sel_t[:, cols]
                + gates_t[3 * hd + 2:3 * hd + 3, :] * o_win_t[:, cols])
    out_ref[0] = ot_ref[...].T


def _cmp_to_sel_t(nb, n_cmp, n_blk):
    cs = np.arange(nb)[None, :] * CMP_STRIDE
    bs = np.arange(SEL_LANES)[:, None] * SEL_BLOCK
    ov = np.clip(np.minimum(cs + CMP_LEN, bs + SEL_BLOCK) - np.maximum(cs, bs), 0, None) / CMP_LEN
    ov = ov * (np.arange(nb)[None, :] < n_cmp) * (np.arange(SEL_LANES)[:, None] < n_blk)
    return jnp.asarray(ov, BF16)


def _nsa_prompt(q, small, kcv, kcv_t, kvs, kvw):
    b, l, _ = q.shape
    nb = kcv.shape[1]
    n_blk = l // SEL_BLOCK
    assert l % (2 * SEL_TILE) == 0 and n_blk <= SEL_LANES and l >= CMP_LEN
    kvs5 = kvs.reshape(b, l, 2, N_KV_HEADS, HEAD_DIM)
    onehot = jnp.asarray(np.arange(l)[:, None] // SEL_BLOCK == np.arange(SEL_LANES)[None, :], BF16)
    k_sel = jnp.transpose(kvs5[:, :, 0], (0, 2, 1, 3)).astype(BF16)
    kaug = jnp.concatenate([jnp.broadcast_to(onehot, (b, N_KV_HEADS, l, SEL_LANES)), k_sel,
                            jnp.zeros((b, N_KV_HEADS, l, HEAD_DIM), BF16)], axis=-1)
    v_t = jnp.transpose(kvs5[:, :, 1].astype(BF16).reshape(b, l // SEL_TILE, SEL_TILE, N_KV_HEADS, HEAD_DIM),
                        (0, 3, 1, 4, 2))
    vaug_t = jnp.concatenate([v_t, jnp.ones((b, N_KV_HEADS, l // SEL_TILE, 1, SEL_TILE), BF16),
                              jnp.zeros((b, N_KV_HEADS, l // SEL_TILE, LANES - HEAD_DIM - 1, SEL_TILE), BF16)], axis=3)
    kwin = jnp.pad(kvw.astype(BF16), ((0, 0), (WINDOW, 0), (0, 0)))
    mselt = _cmp_to_sel_t(nb, (l - CMP_LEN) // CMP_STRIDE + 1, n_blk)

    def per_batch(shape):
        nd = len(shape)
        return pl.BlockSpec((1,) + shape[1:], lambda i, j: (i,) + (0,) * (nd - 1), pipeline_mode=pl.Buffered(1))

    return pl.pallas_call(
        _nsa_prompt_kernel,
        grid=(b, l // Q_BLOCK),
        in_specs=[pl.BlockSpec((1, Q_BLOCK, ATT_DIM), lambda i, j: (i, j, 0)),
                  pl.BlockSpec((1, Q_BLOCK, SMALL_W), lambda i, j: (i, j, 0)),
                  per_batch(kcv.shape), per_batch(kcv_t.shape), per_batch(kaug.shape), per_batch(vaug_t.shape),
                  per_batch(kwin.shape), pl.BlockSpec(mselt.shape, lambda i, j: (0, 0), pipeline_mode=pl.Buffered(1))],
        out_specs=pl.BlockSpec((1, Q_BLOCK, ATT_DIM), lambda i, j: (i, j, 0)),
        out_shape=jax.ShapeDtypeStruct((b, l, ATT_DIM), F32),
        scratch_shapes=[pltpu.VMEM((N_KV_HEADS, AUG, COLS), BF16), pltpu.VMEM((N_KV_HEADS, LANES, COLS), F32),
                        pltpu.VMEM((ATT_DIM, Q_BLOCK), F32), pltpu.VMEM((N_KV_HEADS, SEL_TILE, COLS), F32),
                        pltpu.VMEM((N_KV_HEADS, SEL_TILE, COLS), F32)],
        compiler_params=pltpu.CompilerParams(dimension_semantics=("arbitrary", "arbitrary"),
                                             vmem_limit_bytes=VMEM_LIMIT),
        name="nsa_prompt",
    )(q, small, kcv, kcv_t, kaug, vaug_t, kwin, mselt)


DT_LANE = N_GATE
HEADS_PER_GROUP = SSM_HEADS // N_GROUPS
GROUP_W = D_INNER // N_GROUPS
TAIL = 8


def _expand_heads(v):
    rows = v.shape[0]
    lane = lax.broadcasted_iota(jnp.int32, (rows, LANES), 1)
    tiles = []
    for j in range(D_INNER // LANES):
        a = jnp.broadcast_to(v[:, DT_LANE + 2 * j:DT_LANE + 2 * j + 1], (rows, LANES))
        b = jnp.broadcast_to(v[:, DT_LANE + 2 * j + 1:DT_LANE + 2 * j + 2], (rows, LANES))
        tiles.append(jnp.where(lane < SSM_HEAD_DIM, a, b))
    return jnp.concatenate(tiles, axis=1)


def _cumsum_rows(x):
    n = x.shape[0]
    row = lax.broadcasted_iota(jnp.int32, x.shape, 0)
    s = 1
    while s < n:
        x = x + jnp.where(row >= s, pltpu.roll(x, s, 0), 0.0)
        s *= 2
    return x


def _grouped_norm_gate(y, z, norm):
    v = y * _silu(z)
    outs = []
    for g in range(N_GROUPS):
        vg = v[:, g * GROUP_W:(g + 1) * GROUP_W]
        outs.append(vg * lax.rsqrt(jnp.mean(vg * vg, axis=-1, keepdims=True) + EPS))
    return jnp.concatenate(outs, axis=1) * norm


def _ssd_prompt_kernel(xbc_ref, z_ref, small_ref, convw_ref, convb_ref, dtb_ref, alog_ref, dskip_ref, norm_ref,
                       y_ref, state_ref, xe_ref, st_ref, yd_ref):
    c = pl.program_id(1)
    lc = xbc_ref.shape[1]

    @pl.when(c == 0)
    def _():
        xe_ref[0:TAIL, :] = jnp.zeros((TAIL, CONV_DIM), F32)
        st_ref[...] = jnp.zeros(st_ref.shape, F32)

    xe_ref[TAIL:TAIL + lc, :] = xbc_ref[0]
    conv = convb_ref[...] + xe_ref[TAIL:TAIL + lc, :] * convw_ref[CONV_W - 1:CONV_W, :]
    for k in range(CONV_W - 1):
        conv = conv + xe_ref[pl.ds(TAIL - (CONV_W - 1) + k, lc), :] * convw_ref[k:k + 1, :]
    xe_ref[0:TAIL, :] = xe_ref[lc:lc + TAIL, :]
    xc = _silu(conv)
    xs = xc[:, :D_INNER]

    dt = jax.nn.softplus(small_ref[0] + dtb_ref[...])
    ad = dt * (-jnp.exp(alog_ref[...]))
    acs = _cumsum_rows(ad)
    acs_t = acs.T
    dt_e = _expand_heads(dt)
    acs_e = _expand_heads(acs)
    last_e = acs_e[lc - 1:lc, :]
    xd = xs * dt_e
    xd_bf = xd.astype(BF16)
    xdd_bf = (xd * jnp.exp(last_e - acs_e)).astype(BF16)
    grow = jnp.exp(acs_e)
    li = lax.broadcasted_iota(jnp.int32, (lc, lc), 0)
    si = lax.broadcasted_iota(jnp.int32, (lc, lc), 1)

    y_off = []
    for g in range(N_GROUPS):
        bm = xc[:, D_INNER + g * D_STATE:D_INNER + (g + 1) * D_STATE]
        cm = xc[:, D_INNER + N_GROUPS * D_STATE + g * D_STATE:D_INNER + N_GROUPS * D_STATE + (g + 1) * D_STATE]
        bm_bf = bm.astype(BF16)
        cm_bf = cm.astype(BF16)
        cb = _dot_nt(cm_bf, bm_bf)
        cols = slice(g * GROUP_W, (g + 1) * GROUP_W)
        st_g = st_ref[:, cols]
        y_off.append(_dot(cm_bf, st_g.astype(BF16)))
        for hh in range(HEADS_PER_GROUP):
            h = g * HEADS_PER_GROUP + hh
            seg = acs[:, DT_LANE + h:DT_LANE + h + 1] - acs_t[DT_LANE + h:DT_LANE + h + 1, :]
            m = jnp.where(li >= si, cb * jnp.exp(seg), 0.0).astype(BF16)
            yd_ref[:, h * SSM_HEAD_DIM:(h + 1) * SSM_HEAD_DIM] = _dot(m, xd_bf[:, h * SSM_HEAD_DIM:(h + 1) * SSM_HEAD_DIM])
        st_ref[:, cols] = st_g * jnp.exp(last_e[:, cols]) + _dot(bm.T.astype(BF16), xdd_bf[:, cols])

    y = yd_ref[...] + jnp.concatenate(y_off, axis=1) * grow + dskip_ref[...] * xs
    y_ref[0] = _grouped_norm_gate(y, z_ref[0], norm_ref[...])

    @pl.when(c == pl.num_programs(1) - 1)
    def _():
        state_ref[0] = st_ref[...].T


def _pad_small(v):
    return jnp.zeros((1, SMALL_W), F32).at[0, DT_LANE:DT_LANE + SSM_HEADS].set(v.astype(F32))


def _ssd_prompt(xbc, z, small, conv_w, conv_b, dt_bias, a_log, d_skip, ssm_norm):
    b, l, _ = xbc.shape
    lc = min(SSD_CHUNK, l)
    params = (conv_w, conv_b.reshape(1, CONV_DIM), _pad_small(dt_bias), _pad_small(a_log),
              jnp.repeat(d_skip.astype(F32), SSM_HEAD_DIM).reshape(1, D_INNER), ssm_norm.reshape(1, D_INNER))
    tile = lambda w: pl.BlockSpec((1, lc, w), lambda i, j: (i, j, 0))
    return pl.pallas_call(
        _ssd_prompt_kernel,
        grid=(b, l // lc),
        in_specs=[tile(CONV_DIM), tile(D_INNER), tile(SMALL_W)] + [_const_spec(p.shape) for p in params],
        out_specs=[tile(D_INNER), pl.BlockSpec((1, D_INNER, D_STATE), lambda i, j: (i, 0, 0))],
        out_shape=[jax.ShapeDtypeStruct((b, l, D_INNER), F32), jax.ShapeDtypeStruct((b, D_INNER, D_STATE), F32)],
        scratch_shapes=[pltpu.VMEM((lc + TAIL, CONV_DIM), F32), pltpu.VMEM((D_STATE, D_INNER), F32),
                        pltpu.VMEM((lc, D_INNER), F32)],
        compiler_params=pltpu.CompilerParams(dimension_semantics=("arbitrary", "arbitrary"),
                                             vmem_limit_bytes=VMEM_LIMIT),
        name="ssd_prompt",
    )(xbc, z, small, *params)


def _out_ffn_kernel(h_ref, att_ref, ssm_ref, p_ref, woa_ref, wos_ref, g2_ref, wg_ref, wu_ref, wd_ref,
                    gp_ref, wpg_ref, wple_ref, gf_ref, y_ref, acc_ref):
    h = h_ref[...] + _dot(att_ref[...].astype(BF16), woa_ref[...]) + _dot(ssm_ref[...].astype(BF16), wos_ref[...])
    h = _swiglu_half_step(h, g2_ref, wg_ref, wu_ref, wd_ref, acc_ref)
    gate = jax.nn.sigmoid(_dot(_rms(h, gp_ref[...]).astype(BF16), wpg_ref[...]))
    h = h + gate * _dot(p_ref[...].astype(BF16), wple_ref[...])
    y_ref[...] = _rms(h, gf_ref[...])


def _out_ffn(h, att, ssm, p, weights, tm):
    n = h.shape[0]
    return pl.pallas_call(
        _out_ffn_kernel,
        grid=(n // tm,),
        in_specs=[_row_spec(tm, D_MODEL), _row_spec(tm, ATT_DIM), _row_spec(tm, D_INNER), _row_spec(tm, PLE_DIM)]
        + [_const_spec(w.shape) for w in weights],
        out_specs=_row_spec(tm, D_MODEL),
        out_shape=jax.ShapeDtypeStruct((n, D_MODEL), F32),
        scratch_shapes=[pltpu.VMEM((tm, D_MODEL), F32)],
        compiler_params=pltpu.CompilerParams(dimension_semantics=("arbitrary",), vmem_limit_bytes=VMEM_LIMIT),
        name="outproj_ffn2_ple",
    )(h, att, ssm, p, *weights)


BLOCKS_PER_PAGE = PAGE_SIZE // CMP_STRIDE
HEAD_ROWS = 8


def _head_rows_of(h):
    row = lax.broadcasted_iota(jnp.int32, (HEAD_ROWS, 1), 0)
    return (row >= h * GQA) & (row < (h + 1) * GQA)


def _cmp_sample_kernel(pt_ref, q_ref, cache_ref, wbd_ref, pe_ref, b1_ref, w2_ref, mselt_ref,
                       ocmp_ref, imp_ref, xbuf_ref, xrow_ref, lhs_ref, sem_ref, *, n_cmp):
    b = pl.program_id(0)
    nb_total = pl.num_programs(0)
    n_pages = pt_ref.shape[1]
    nbp = n_pages * BLOCKS_PER_PAGE

    def page_copy(bb, slot, i):
        return pltpu.make_async_copy(cache_ref.at[pt_ref[bb, i], 0], xbuf_ref.at[slot, i], sem_ref.at[slot])

    def start_all(bb, slot):
        def body(i, carry):
            page_copy(bb, slot, i).start()
            return carry
        lax.fori_loop(0, n_pages, body, 0)

    def wait_all(bb, slot):
        def body(i, carry):
            page_copy(bb, slot, i).wait()
            return carry
        lax.fori_loop(0, n_pages, body, 0)

    slot = b % 2

    @pl.when(b == 0)
    def _():
        start_all(0, 0)

    @pl.when(b + 1 < nb_total)
    def _():
        start_all(b + 1, 1 - slot)

    wait_all(b, slot)

    def load_rows(c, s):
        if s == 0:
            for i in range(n_pages):
                xrow_ref[c, i * PAGE_SIZE:(i + 1) * PAGE_SIZE, :] = xbuf_ref[slot, i, c].reshape(HD2, PAGE_SIZE).T
        return xrow_ref[c, pl.ds(s, nbp, stride=CMP_STRIDE), :]

    kcv = _cmp_tokens(load_rows, nbp, lhs_ref, wbd_ref, pe_ref, b1_ref, w2_ref)
    q8 = (q_ref[0] * (HEAD_DIM ** -0.5)).astype(BF16)
    valid = lax.broadcasted_iota(jnp.int32, (1, nbp), 1) < n_cmp
    o_cmp = jnp.zeros((HEAD_ROWS, HEAD_DIM), F32)
    psum = jnp.zeros((HEAD_ROWS, nbp), F32)
    row = lax.broadcasted_iota(jnp.int32, (HEAD_ROWS, 1), 0)
    for h in range(N_KV_HEADS):
        kc = kcv[:, h * HEAD_DIM:(h + 1) * HEAD_DIM].astype(BF16)
        vc = kcv[:, (N_KV_HEADS + h) * HEAD_DIM:(N_KV_HEADS + h + 1) * HEAD_DIM].astype(BF16)
        p = _softmax_rows(jnp.where(valid, _dot_nt(q8, kc), NEG_BIG))
        mine = _head_rows_of(h)
        o_cmp = jnp.where(mine, _dot(p.astype(BF16), vc), o_cmp)
        ph = jnp.sum(jnp.where(mine, p, 0.0), axis=0, keepdims=True)
        psum = jnp.where(row == h, ph, psum)
    ocmp_ref[0] = o_cmp
    p_hi = psum.astype(BF16)
    p_lo = (psum - p_hi.astype(F32)).astype(BF16)
    imp_ref[0] = _dot_nt(p_hi, mselt_ref[...]) + _dot_nt(p_lo, mselt_ref[...])


def _cmp_sample(page_table, q8, cache_t, wbd, pe_rows, b1, w2, mselt, n_cmp):
    db, n_pages = page_table.shape
    consts = (wbd, pe_rows, b1, w2, mselt)
    grid_spec = pltpu.PrefetchScalarGridSpec(
        num_scalar_prefetch=1,
        grid=(db,),
        in_specs=[pl.BlockSpec((1, HEAD_ROWS, HEAD_DIM), lambda i, pt: (i, 0, 0)),
                  pl.BlockSpec(memory_space=pl.ANY)]
        + [pl.BlockSpec(c.shape, lambda i, pt, nd=c.ndim: (0,) * nd, pipeline_mode=pl.Buffered(1)) for c in consts],
        out_specs=[pl.BlockSpec((1, HEAD_ROWS, HEAD_DIM), lambda i, pt: (i, 0, 0)),
                   pl.BlockSpec((1, HEAD_ROWS, SEL_LANES), lambda i, pt: (i, 0, 0))],
        scratch_shapes=[pltpu.VMEM((2, n_pages, 2, N_KV_HEADS, HEAD_DIM, PAGE_SIZE), F32),
                        pltpu.VMEM((2, n_pages * PAGE_SIZE, HD2), F32),
                        pltpu.VMEM((n_pages * BLOCKS_PER_PAGE + PE_ROWS, CMP_K), BF16), pltpu.SemaphoreType.DMA((2,))],
    )
    return pl.pallas_call(
        functools.partial(_cmp_sample_kernel, n_cmp=n_cmp),
        grid_spec=grid_spec,
        out_shape=[jax.ShapeDtypeStruct((db, HEAD_ROWS, HEAD_DIM), F32),
                   jax.ShapeDtypeStruct((db, HEAD_ROWS, SEL_LANES), F32)],
        compiler_params=pltpu.CompilerParams(dimension_semantics=("arbitrary",), vmem_limit_bytes=VMEM_LIMIT),
        name="nsa_cmp_sample",
    )(page_table, q8, cache_t, *consts)


N_PAST_PICKS = SEL_TOPK - 1


def _topk_sample_kernel(imp_ref, idx_ref, *, past_blk):
    x = imp_ref[...].T
    jj = lax.broadcasted_iota(jnp.int32, x.shape, 0)
    x = jnp.where((jj == 0) | (jj == past_blk - 1), IMP_FORCED, x)
    x = jnp.where(jj < past_blk, x, IMP_BLOCKED)
    _, picks = _take_top(x, N_PAST_PICKS)
    picks = picks + [jnp.zeros_like(picks[0])] * (idx_ref.shape[0] - N_PAST_PICKS)
    idx_ref[...] = jnp.concatenate(picks, axis=0).astype(jnp.int32)


def _topk_sample(imp2, past_blk):
    nq = imp2.shape[0]
    assert N_PAST_PICKS <= past_blk <= SEL_LANES
    return pl.pallas_call(
        functools.partial(_topk_sample_kernel, past_blk=past_blk),
        out_shape=jax.ShapeDtypeStruct((SEL_TOPK, nq), jnp.int32),
        name="nsa_topk_sample",
    )(imp2)


SUB_PER_PAGE = PAGE_SIZE // SEL_BLOCK
SUB_SHIFT = SUB_PER_PAGE.bit_length() - 1
SEL_SHIFT_IN_PAGE = SEL_BLOCK.bit_length() - 1
N_SEL_COLS = N_PAST_PICKS * PAGE_SIZE


def _attend_one_token(q8, q8f, kt, vt, valid, k_new, v_new):
    s = jnp.where(valid, _dot(q8, kt.astype(BF16)), NEG_BIG)
    s_new = jnp.sum(q8f * k_new, axis=-1, keepdims=True)
    m = jnp.maximum(jnp.max(s, axis=-1, keepdims=True), s_new)
    e = jnp.exp(s - m)
    e_new = jnp.exp(s_new - m)
    norm = jnp.maximum(jnp.sum(e, axis=-1, keepdims=True) + e_new, TINY)
    return (_dot_nt(e.astype(BF16), vt.astype(BF16)) + e_new * v_new) / norm


def _sel_win_sample_kernel(idx_ref, pt_ref, q_ref, gate_ref, ocmp_ref, kvs_ref, kvw_ref, win_ref, slc_ref,
                           att_ref, newwin_ref, gbuf_ref, sem_ref):
    b = pl.program_id(0)
    nb_total = pl.num_programs(0)
    w_buf = win_ref.shape[-1]

    def block_copies(bb, slot, h, k):
        j = idx_ref[k, bb * N_KV_HEADS + h]
        page = pt_ref[bb, j >> SUB_SHIFT]
        return [pltpu.make_async_copy(slc_ref.at[page, 0, c, h],
                                      gbuf_ref.at[slot, h, c, :, pl.ds(k * PAGE_SIZE, PAGE_SIZE)],
                                      sem_ref.at[slot]) for c in range(2)]

    def start_all(bb, slot):
        for h in range(N_KV_HEADS):
            for k in range(N_PAST_PICKS):
                for cp in block_copies(bb, slot, h, k):
                    cp.start()

    def wait_all(bb, slot):
        for h in range(N_KV_HEADS):
            for k in range(N_PAST_PICKS):
                for cp in block_copies(bb, slot, h, k):
                    cp.wait()

    slot = b % 2

    @pl.when(b == 0)
    def _():
        start_all(0, 0)

    @pl.when(b + 1 < nb_total)
    def _():
        start_all(b + 1, 1 - slot)

    wait_all(b, slot)

    q8f = q_ref[0] * (HEAD_DIM ** -0.5)
    q8 = q8f.astype(BF16)
    page_lane = lax.broadcasted_iota(jnp.int32, (1, PAGE_SIZE), 1)
    win_lane = lax.broadcasted_iota(jnp.int32, (1, w_buf), 1)
    win_valid = w_buf - win_lane < WINDOW
    o_sel = jnp.zeros((HEAD_ROWS, HEAD_DIM), F32)
    o_win = jnp.zeros((HEAD_ROWS, HEAD_DIM), F32)
    for h in range(N_KV_HEADS):
        kcols = slice(h * HEAD_DIM, (h + 1) * HEAD_DIM)
        vcols = slice((N_KV_HEADS + h) * HEAD_DIM, (N_KV_HEADS + h + 1) * HEAD_DIM)
        mine = _head_rows_of(h)
        sel_valid = jnp.concatenate(
            [(page_lane >> SEL_SHIFT_IN_PAGE) == (idx_ref[k, b * N_KV_HEADS + h] & (SUB_PER_PAGE - 1))
             for k in range(N_PAST_PICKS)], axis=1)
        o_sel = jnp.where(mine, _attend_one_token(q8, q8f, gbuf_ref[slot, h, 0], gbuf_ref[slot, h, 1], sel_valid,
                                                  kvs_ref[0][:, kcols], kvs_ref[0][:, vcols]), o_sel)
        o_win = jnp.where(mine, _attend_one_token(q8, q8f, win_ref[0, 0, 0, h], win_ref[0, 0, 1, h], win_valid,
                                                  kvw_ref[0][:, kcols], kvw_ref[0][:, vcols]), o_win)
    gates = jax.nn.sigmoid(gate_ref[0])
    att_ref[0] = gates[:, 0:1] * ocmp_ref[0] + gates[:, 1:2] * o_sel + gates[:, 2:3] * o_win

    new_col = jnp.concatenate([kvw_ref[0], jnp.zeros((LANES - 1, KV_DIM), F32)], axis=0).T
    tile_lane = lax.broadcasted_iota(jnp.int32, (HEAD_DIM, w_buf), 1)
    for c in range(2):
        for h in range(N_KV_HEADS):
            lo = (c * N_KV_HEADS + h) * HEAD_DIM
            newwin_ref[0, 0, c, h] = jnp.where(tile_lane == w_buf - 1, new_col[lo:lo + HEAD_DIM, 0:1],
                                               pltpu.roll(win_ref[0, 0, c, h], w_buf - 1, 1))


def _sel_win_sample(idx, page_table, q8, gate8, o_cmp, kvs_new, kvw_new, win_t, slc_t):
    db = page_table.shape[0]
    w_buf = win_t.shape[-1]
    assert w_buf == WINDOW and win_t.shape[1] == 1
    per_tok = lambda shape: pl.BlockSpec((1,) + shape[1:], lambda i, idx, pt, nd=len(shape): (i,) + (0,) * (nd - 1))
    grid_spec = pltpu.PrefetchScalarGridSpec(
        num_scalar_prefetch=2,
        grid=(db,),
        in_specs=[per_tok(q8.shape), per_tok(gate8.shape), per_tok(o_cmp.shape), per_tok(kvs_new.shape),
                  per_tok(kvw_new.shape), per_tok(win_t.shape), pl.BlockSpec(memory_space=pl.ANY)],
        out_specs=[per_tok(o_cmp.shape), per_tok(win_t.shape)],
        scratch_shapes=[pltpu.VMEM((2, N_KV_HEADS, 2, HEAD_DIM, N_SEL_COLS), F32), pltpu.SemaphoreType.DMA((2,))],
    )
    return pl.pallas_call(
        _sel_win_sample_kernel,
        grid_spec=grid_spec,
        out_shape=[jax.ShapeDtypeStruct(o_cmp.shape, F32), jax.ShapeDtypeStruct(win_t.shape, F32)],
        compiler_params=pltpu.CompilerParams(dimension_semantics=("arbitrary",), vmem_limit_bytes=VMEM_LIMIT),
        name="nsa_sel_win_sample",
    )(idx, page_table, q8, gate8, o_cmp, kvs_new, kvw_new, win_t, slc_t)


SSD_BT = 8


def _ssd_sample_kernel(xbc_ref, cst_ref, z_ref, small_ref, h0_ref, convw_ref, convb_ref, dtb_ref, alog_ref,
                       dskip_ref, norm_ref, y_ref, hnew_ref, ys_ref):
    conv = convb_ref[...] + xbc_ref[...] * convw_ref[CONV_W - 1:CONV_W, :]
    for k in range(CONV_W - 1):
        conv = conv + cst_ref[k] * convw_ref[k:k + 1, :]
    xc = _silu(conv)
    xs = xc[:, :D_INNER]
    dt = jax.nn.softplus(small_ref[...] + dtb_ref[...])
    dt_e = _expand_heads(dt)
    decay_e = jnp.exp(dt_e * _expand_heads(-jnp.exp(alog_ref[...])))
    xd = xs * dt_e
    fill = jnp.zeros((LANES - SSD_BT, D_INNER), F32)
    xd_t = jnp.concatenate([xd, fill], axis=0).T
    decay_t = jnp.concatenate([decay_e, fill], axis=0).T
    lane = lax.broadcasted_iota(jnp.int32, (1, D_INNER), 1)
    for i in range(SSD_BT):
        bsel = jnp.concatenate(
            [jnp.broadcast_to(xc[i:i + 1, D_INNER + g * D_STATE:D_INNER + (g + 1) * D_STATE], (GROUP_W, D_STATE))
             for g in range(N_GROUPS)], axis=0)
        h0 = h0_ref[i].reshape(D_INNER, D_STATE)
        hn = decay_t[:, i:i + 1] * h0 + xd_t[:, i:i + 1] * bsel
        hnew_ref[i] = hn.reshape(SSM_HEADS, SSM_HEAD_DIM, D_STATE)
        c8 = jnp.concatenate(
            [xc[i:i + 1, D_INNER + (N_GROUPS + g) * D_STATE:D_INNER + (N_GROUPS + g + 1) * D_STATE]
             for g in range(N_GROUPS)] + [jnp.zeros((HEAD_ROWS - N_GROUPS, D_STATE), F32)], axis=0)
        y8 = _dot_nt(c8.astype(BF16), hn.astype(BF16))
        ys_ref[i:i + 1, :] = jnp.where(lane < GROUP_W, y8[0:1], y8[1:2])
    y = ys_ref[...] + dskip_ref[...] * xs
    y_ref[...] = _grouped_norm_gate(y, z_ref[...], norm_ref[...])


def _ssd_sample(xbc, conv_state_t, z, small, h0, conv_w, conv_b, dt_bias, a_log, d_skip, ssm_norm):
    db = xbc.shape[0]
    assert db % SSD_BT == 0 and N_GROUPS == 2
    params = (conv_w, conv_b.reshape(1, CONV_DIM), _pad_small(dt_bias), _pad_small(a_log),
              jnp.repeat(d_skip.astype(F32), SSM_HEAD_DIM).reshape(1, D_INNER), ssm_norm.reshape(1, D_INNER))
    rows = lambda w: pl.BlockSpec((SSD_BT, w), lambda i: (i, 0))
    state_spec = pl.BlockSpec((SSD_BT, SSM_HEADS, SSM_HEAD_DIM, D_STATE), lambda i: (i, 0, 0, 0))
    return pl.pallas_call(
        _ssd_sample_kernel,
        grid=(db // SSD_BT,),
        in_specs=[rows(CONV_DIM), pl.BlockSpec((CONV_W - 1, SSD_BT, CONV_DIM), lambda i: (0, i, 0)), rows(D_INNER),
                  rows(SMALL_W), state_spec] + [_const_spec(p.shape) for p in params],
        out_specs=[rows(D_INNER), state_spec],
        out_shape=[jax.ShapeDtypeStruct((db, D_INNER), F32), jax.ShapeDtypeStruct(h0.shape, F32)],
        scratch_shapes=[pltpu.VMEM((SSD_BT, D_INNER), F32)],
        compiler_params=pltpu.CompilerParams(dimension_semantics=("arbitrary",), vmem_limit_bytes=VMEM_LIMIT),
        name="ssd_sample",
    )(xbc, conv_state_t, z, small, h0, *params)


PROMPT_TM = 512


def _ffn_weights(w_gate, w_up, w_down):
    chunked = lambda w: jnp.transpose(w.astype(BF16).reshape(D_MODEL, N_FF_CHUNKS, FF_CHUNK), (1, 0, 2))
    return chunked(w_gate), chunked(w_up), w_down.astype(BF16).reshape(N_FF_CHUNKS, FF_CHUNK, D_MODEL)


def _in_proj_weight(w_in):
    cuts = np.cumsum([ATT_DIM, KV_DIM, KV_DIM, KV_DIM, N_GATE, D_INNER, CONV_DIM]).tolist()
    q, kvc, kvs, kvw, g, z, xbc, dt = jnp.split(w_in, cuts, axis=-1)
    pad = jnp.zeros((D_MODEL, SMALL_W - N_GATE - SSM_HEADS), w_in.dtype)
    return jnp.concatenate([q, kvc, kvs, kvw, z, xbc, g, dt, pad], axis=-1).astype(BF16)


def kernel(x_prompt, x_sample, cache_cmp_kv, cache_slc_kv, cache_win_kv, state_conv, state_ssm, page_table,
           p_prompt, p_sample, ffn1_norm, ffn1_w_gate, ffn1_w_up, ffn1_w_down, mix_norm, w_in,
           cmp_w1, cmp_pe, cmp_b1, cmp_w2, conv_w, conv_b, dt_bias, a_log, d_skip, ssm_norm, w_out,
           ffn2_norm, ffn2_w_gate, ffn2_w_up, ffn2_w_down, ple_norm, w_ple_gate, w_ple, final_norm):
    b, l, _ = x_prompt.shape
    db, t_new, _ = x_sample.shape
    depth = ffn1_norm.shape[0]
    n_pool = cache_cmp_kv.shape[0]
    n_pages = page_table.shape[1]
    past = n_pages * PAGE_SIZE
    assert depth == 1 and t_new == 1 and past >= WINDOW and (b * l) % PROMPT_TM == 0
    i = 0
    row = lambda v: v.reshape(1, -1).astype(F32)

    ffn1 = _ffn_weights(ffn1_w_gate[i], ffn1_w_up[i], ffn1_w_down[i])
    stage1_w = (row(ffn1_norm[i]),) + ffn1 + (row(mix_norm[i]), _in_proj_weight(w_in[i]))
    cmp_w = _cmp_weights(cmp_w1[i], cmp_pe[i], cmp_b1[i], cmp_w2[i])
    ssm_w = (conv_w[i], conv_b[i], dt_bias[i], a_log[i], d_skip[i], ssm_norm[i])
    w_o = w_out[i].astype(BF16)
    stage3_w = (w_o[:ATT_DIM], w_o[ATT_DIM:], row(ffn2_norm[i])) + _ffn_weights(ffn2_w_gate[i], ffn2_w_up[i], ffn2_w_down[i]) + (
        row(ple_norm[i]), w_ple_gate[i].astype(BF16), w_ple[i].astype(BF16), row(final_norm))
    kv6 = lambda a, lead, rows: a.reshape(lead, 1, rows, 2, N_KV_HEADS, HEAD_DIM)

    h, q, kvc, kvs, kvw, z, xbc, small = _ffn_inproj(x_prompt.reshape(b * l, D_MODEL), *stage1_w, PROMPT_TM)
    per_b = lambda a: a.reshape(b, l, a.shape[-1])
    kcv, kcv_t = _cmp_prompt(per_b(kvc), *cmp_w)
    att = _nsa_prompt(per_b(q), per_b(small), kcv, kcv_t, per_b(kvs), per_b(kvw))
    ssm, ssm_state = _ssd_prompt(per_b(xbc), per_b(z), per_b(small), *ssm_w)
    y_prompt = _out_ffn(h, att.reshape(b * l, ATT_DIM), ssm.reshape(b * l, D_INNER),
                        p_prompt[i].reshape(b * l, PLE_DIM), stage3_w, PROMPT_TM).reshape(b, l, D_MODEL)
    keep = min(WINDOW, l)
    new_cmp_p = kv6(kvc, b, l)
    new_slc_p = kv6(kvs, b, l)
    new_win_p = kv6(per_b(kvw)[:, l - keep:], b, keep)
    new_conv_p = per_b(xbc)[:, l - (CONV_W - 1):].reshape(b, 1, CONV_W - 1, CONV_DIM)
    new_ssm_p = ssm_state.reshape(b, 1, SSM_HEADS, SSM_HEAD_DIM, D_STATE)

    hs, qs, kvc_s, kvs_s, kvw_s, z_s, xbc_s, small_s = _ffn_inproj(x_sample.reshape(db, D_MODEL), *stage1_w, db)
    q8 = qs.reshape(db, N_HEADS, HEAD_DIM)
    n_cmp_s = (past + t_new - CMP_LEN) // CMP_STRIDE + 1
    past_blk = past // SEL_BLOCK
    mselt_s = _cmp_to_sel_t(n_pages * BLOCKS_PER_PAGE, n_cmp_s, past_blk)
    row_minor = lambda a: jnp.transpose(a, (0, 1, 3, 4, 5, 2))
    o_cmp, imp = _cmp_sample(page_table, q8, row_minor(cache_cmp_kv), *cmp_w, mselt_s, n_cmp_s)
    n_query = db * N_KV_HEADS
    imp2 = jnp.pad(imp[:, :N_KV_HEADS].reshape(n_query, SEL_LANES), ((0, -n_query % LANES), (0, 0)))
    idx = _topk_sample(imp2, past_blk)
    att_s, new_win_t = _sel_win_sample(
        idx, page_table, q8, small_s[:, :N_GATE].reshape(db, N_HEADS, 3), o_cmp,
        kvs_s.reshape(db, 1, KV_DIM), kvw_s.reshape(db, 1, KV_DIM),
        row_minor(cache_win_kv), row_minor(cache_slc_kv))
    new_win_s = jnp.transpose(new_win_t, (0, 1, 5, 2, 3, 4))
    ssm_s, ssm_state_s = _ssd_sample(xbc_s, jnp.transpose(state_conv[:, i], (1, 0, 2)), z_s, small_s,
                                     state_ssm[:, i], *ssm_w)
    y_sample = _out_ffn(hs, att_s.reshape(db, ATT_DIM), ssm_s, p_sample[i].reshape(db, PLE_DIM),
                        stage3_w, db).reshape(db, 1, D_MODEL)
    new_conv_s = jnp.concatenate([state_conv[:, i, 1:], xbc_s[:, None]], axis=1)[:, None]

    return (y_prompt, y_sample, new_cmp_p, new_slc_p, new_win_p, new_conv_p, new_ssm_p,
            kv6(kvc_s, db, 1), kv6(kvs_s, db, 1), new_win_s, new_conv_s, ssm_state_s[:, None])
```

```python
import functools

import jax
import jax.numpy as jnp
import numpy as np
from jax import lax
from jax.experimental import pallas as pl
from jax.experimental.pallas import tpu as pltpu

F32 = jnp.float32
BF16 = jnp.bfloat16

D_MODEL = 1024
N_HEADS = 8
N_KV_HEADS = 2
HEAD_DIM = 64
GQA = N_HEADS // N_KV_HEADS
ATT_DIM = N_HEADS * HEAD_DIM
KV_DIM = 2 * N_KV_HEADS * HEAD_DIM
CMP_LEN = 32
CMP_STRIDE = 16
CMP_HID = 128
SEL_BLOCK = 64
SEL_TOPK = 16
WINDOW = 512
Q_BLOCK = 128
SSM_HEADS = 8
SSM_HEAD_DIM = 64
D_INNER = SSM_HEADS * SSM_HEAD_DIM
N_GROUPS = 2
D_STATE = 128
CONV_W = 4
CONV_DIM = D_INNER + 2 * N_GROUPS * D_STATE
SSD_CHUNK = 128
D_FF = 2816
PLE_DIM = 256
PAGE_SIZE = 128
EPS = 1e-6

LANES = 128
FF_CHUNK = 256
N_FF_CHUNKS = D_FF // FF_CHUNK
SMALL_W = LANES
N_GATE = 3 * N_HEADS
IN_PROJ_PAD = ATT_DIM + 3 * KV_DIM + D_INNER + CONV_DIM + SMALL_W
VMEM_LIMIT = 56 * 1024 * 1024
NEG_BIG = -1e30
IMP_FORCED = 3e38
IMP_BLOCKED = -1e38
IMP_TAKEN = -3e38


def _dot(a, b):
    return jnp.dot(a, b, preferred_element_type=F32)


def _dot_nt(a, b):
    return lax.dot_general(a, b, (((1,), (1,)), ((), ())), preferred_element_type=F32)


def _rms(x, g):
    return x * lax.rsqrt(jnp.mean(x * x, axis=-1, keepdims=True) + EPS) * g


def _silu(x):
    return x * jax.nn.sigmoid(x)


def _const_spec(shape):
    nd = len(shape)
    return pl.BlockSpec(shape, lambda *_: (0,) * nd, pipeline_mode=pl.Buffered(1))


def _row_spec(tm, width):
    return pl.BlockSpec((tm, width), lambda i: (i, 0))


def _swiglu_half_step(x, g_ref, wg_ref, wu_ref, wd_ref, acc_ref):
    xn = _rms(x, g_ref[...]).astype(BF16)
    acc_ref[...] = jnp.zeros_like(acc_ref)

    def body(c, carry):
        a = _dot(xn, wg_ref[c])
        b = _dot(xn, wu_ref[c])
        hm = (_silu(a) * b).astype(BF16)
        acc_ref[...] += _dot(hm, wd_ref[c])
        return carry

    lax.fori_loop(0, N_FF_CHUNKS, body, 0, unroll=True)
    return x + 0.5 * acc_ref[...]


_IN_SEGS = (ATT_DIM, KV_DIM, KV_DIM, KV_DIM, D_INNER, CONV_DIM, SMALL_W)


SEL_TILE = 512
SEL_LANES = 128
AUG = SEL_LANES + 2 * HEAD_DIM
SEL_SHIFT = SEL_BLOCK.bit_length() - 1


def _ffn_inproj_kernel(x_ref, g1_ref, wg_ref, wu_ref, wd_ref, gm_ref, win_ref,
                       h_ref, q_ref, kvc_ref, kvs_ref, kvw_ref, z_ref, xbc_ref, small_ref, *rest, tiles):
    *prompt_refs, acc_ref = rest
    tm = x_ref.shape[0]
    h = _swiglu_half_step(x_ref[...], g1_ref, wg_ref, wu_ref, wd_ref, acc_ref)
    h_ref[...] = h
    hn = _rms(h, gm_ref[...]).astype(BF16)
    off = 0
    for out_ref, width in zip((q_ref, kvc_ref, kvs_ref, kvw_ref, z_ref, xbc_ref, small_ref), _IN_SEGS):
        out_ref[...] = _dot(hn, win_ref[:, off:off + width])
        off += width
    if not prompt_refs:
        return
    kvct_ref, kvst_ref, kaug_ref, vaugt_ref, kwin_ref = prompt_refs
    kvs = kvs_ref[...]
    kvc_t = kvc_ref[...].T
    kvs_t = kvs.T
    for ch in range(2 * N_KV_HEADS):
        rows = slice(ch * HEAD_DIM, (ch + 1) * HEAD_DIM)
        kvct_ref[0, 0, ch // N_KV_HEADS, ch % N_KV_HEADS] = kvc_t[rows, :]
        kvst_ref[0, 0, ch // N_KV_HEADS, ch % N_KV_HEADS] = kvs_t[rows, :]
    kwin_ref[...] = kvw_ref[...].astype(BF16)
    pos = (pl.program_id(0) % tiles) * tm + lax.broadcasted_iota(jnp.int32, (tm, 1), 0)
    onehot = ((pos >> SEL_SHIFT) == lax.broadcasted_iota(jnp.int32, (1, SEL_LANES), 1)).astype(BF16)
    ones_row = (lax.broadcasted_iota(jnp.int32, (LANES - HEAD_DIM, tm), 0) == 0).astype(BF16)
    for hh in range(N_KV_HEADS):
        kaug_ref[0, hh, :, 0:SEL_LANES] = onehot
        kaug_ref[0, hh, :, SEL_LANES:SEL_LANES + HEAD_DIM] = kvs[:, hh * HEAD_DIM:(hh + 1) * HEAD_DIM].astype(BF16)
        kaug_ref[0, hh, :, SEL_LANES + HEAD_DIM:] = jnp.zeros((tm, HEAD_DIM), BF16)
        vaugt_ref[0, hh, 0, 0:HEAD_DIM, :] = kvs_t[(N_KV_HEADS + hh) * HEAD_DIM:(N_KV_HEADS + hh + 1) * HEAD_DIM, :].astype(BF16)
        vaugt_ref[0, hh, 0, HEAD_DIM:, :] = ones_row


def _ffn_inproj(x, g1, wg, wu, wd, gm, win, tm, cache_batch=None):
    n = x.shape[0]
    outs = [jax.ShapeDtypeStruct((n, D_MODEL), F32)] + [jax.ShapeDtypeStruct((n, w), F32) for w in _IN_SEGS]
    out_specs = [_row_spec(tm, D_MODEL)] + [_row_spec(tm, w) for w in _IN_SEGS]
    tiles = 1
    if cache_batch is not None:
        l = n // cache_batch
        tiles = l // tm
        assert l % tm == 0 and tm == SEL_TILE
        outs += [jax.ShapeDtypeStruct((cache_batch, 1, 2, N_KV_HEADS, HEAD_DIM, l), F32)] * 2
        out_specs += [pl.BlockSpec((1, 1, 2, N_KV_HEADS, HEAD_DIM, tm),
                                   lambda i: (i // tiles, 0, 0, 0, 0, i % tiles))] * 2
        outs += [jax.ShapeDtypeStruct((cache_batch, N_KV_HEADS, l, AUG), BF16),
                 jax.ShapeDtypeStruct((cache_batch, N_KV_HEADS, tiles, LANES, tm), BF16),
                 jax.ShapeDtypeStruct((n, KV_DIM), BF16)]
        out_specs += [pl.BlockSpec((1, N_KV_HEADS, tm, AUG), lambda i: (i // tiles, 0, i % tiles, 0)),
                      pl.BlockSpec((1, N_KV_HEADS, 1, LANES, tm), lambda i: (i // tiles, 0, i % tiles, 0, 0)),
                      _row_spec(tm, KV_DIM)]
    return pl.pallas_call(
        functools.partial(_ffn_inproj_kernel, tiles=tiles),
        grid=(n // tm,),
        in_specs=[_row_spec(tm, D_MODEL), _const_spec(g1.shape), _const_spec(wg.shape), _const_spec(wu.shape),
                  _const_spec(wd.shape), _const_spec(gm.shape), _const_spec(win.shape)],
        out_specs=out_specs,
        out_shape=outs,
        scratch_shapes=[pltpu.VMEM((tm, D_MODEL), F32)],
        compiler_params=pltpu.CompilerParams(dimension_semantics=("arbitrary",), vmem_limit_bytes=VMEM_LIMIT),
        name="ffn1_inproj",
    )(x, g1, wg, wu, wd, gm, win)


HD2 = N_KV_HEADS * HEAD_DIM
CMP_R = CMP_LEN // CMP_STRIDE
CMP_PROJ = N_KV_HEADS * CMP_R * CMP_HID
PE_ROWS = 16
CMP_K = CMP_STRIDE * HD2


def _cmp_weights(cmp_w1, cmp_pe, cmp_b1, cmp_w2):
    w1r = cmp_w1.reshape(2, CMP_R, CMP_STRIDE, HEAD_DIM, CMP_HID)
    per = cmp_pe.reshape(2, CMP_R, CMP_STRIDE, HEAD_DIM)
    wbd = jnp.zeros((2, CMP_STRIDE, N_KV_HEADS, HEAD_DIM, N_KV_HEADS, CMP_R, CMP_HID), F32)
    blk = jnp.transpose(w1r, (0, 2, 3, 1, 4))
    for h in range(N_KV_HEADS):
        wbd = wbd.at[:, :, h, :, h, :, :].set(blk)
    wbd = wbd.reshape(2, CMP_K, CMP_PROJ).astype(BF16)
    pe_rows = jnp.broadcast_to(per[:, :, :, None, :], (2, CMP_R, CMP_STRIDE, N_KV_HEADS, HEAD_DIM))
    pe_rows = jnp.concatenate([pe_rows.reshape(2, CMP_R, CMP_K), jnp.zeros((2, PE_ROWS - CMP_R, CMP_K), F32)], axis=1)
    return wbd, pe_rows.astype(BF16), cmp_b1, cmp_w2.astype(BF16)


def _cmp_tokens(load_rows, nb, lhs_ref, wbd_ref, pe_ref, b1_ref, w2_ref):
    outs = []
    for c in range(2):
        for s in range(CMP_STRIDE):
            lhs_ref[0:nb, s * HD2:(s + 1) * HD2] = load_rows(c, s).astype(BF16)
        lhs_ref[nb:nb + PE_ROWS, :] = pe_ref[c]
        acc = _dot(lhs_ref[...], wbd_ref[c])
        proj, pe_proj = acc[:nb], acc[nb:]
        for h in range(N_KV_HEADS):
            lo = h * CMP_R * CMP_HID
            p0 = proj[:, lo:lo + CMP_HID]
            p1 = pltpu.roll(proj[:, lo + CMP_HID:lo + 2 * CMP_HID], nb - 1, 0)
            pe_add = pe_proj[0:1, lo:lo + CMP_HID] + pe_proj[1:2, lo + CMP_HID:lo + 2 * CMP_HID]
            hid = _silu(p0 + p1 + pe_add + b1_ref[c:c + 1, :])
            outs.append(_dot(hid.astype(BF16), w2_ref[c]))
    return jnp.concatenate(outs, axis=1)


def _cmp_prompt_kernel(xk_ref, xv_ref, wbd_ref, pe_ref, b1_ref, w2_ref, out_ref, out_t_ref, lhs_ref):
    nb = out_ref.shape[1]
    load_rows = lambda c, s: (xk_ref, xv_ref)[c][0, pl.ds(s, nb, stride=CMP_STRIDE), :]
    tokens = _cmp_tokens(load_rows, nb, lhs_ref, wbd_ref, pe_ref, b1_ref, w2_ref)
    out_ref[0] = tokens
    out_t_ref[0] = tokens.T


def _cmp_prompt(kvc, wbd, pe_rows, b1, w2):
    b, l, _ = kvc.shape
    nb = l // CMP_STRIDE
    return pl.pallas_call(
        _cmp_prompt_kernel,
        grid=(b,),
        in_specs=[pl.BlockSpec((1, l, HD2), lambda i: (i, 0, 0)), pl.BlockSpec((1, l, HD2), lambda i: (i, 0, 1)),
                  _const_spec(wbd.shape), _const_spec(pe_rows.shape), _const_spec(b1.shape), _const_spec(w2.shape)],
        out_specs=[pl.BlockSpec((1, nb, KV_DIM), lambda i: (i, 0, 0)), pl.BlockSpec((1, KV_DIM, nb), lambda i: (i, 0, 0))],
        out_shape=[jax.ShapeDtypeStruct((b, nb, KV_DIM), F32), jax.ShapeDtypeStruct((b, KV_DIM, nb), F32)],
        scratch_shapes=[pltpu.VMEM((nb + PE_ROWS, CMP_K), BF16)],
        compiler_params=pltpu.CompilerParams(dimension_semantics=("arbitrary",), vmem_limit_bytes=VMEM_LIMIT),
        name="nsa_cmp_prompt",
    )(kvc, kvc, wbd, pe_rows, b1, w2)


COLS = GQA * Q_BLOCK
TINY = float(np.finfo(np.float32).tiny)


def _softmax_rows(s):
    m = jnp.max(s, axis=-1, keepdims=True)
    m = jnp.where(m > 0.5 * NEG_BIG, m, 0.0)
    e = jnp.exp(s - m)
    return e / jnp.maximum(jnp.sum(e, axis=-1, keepdims=True), TINY)


def _select_blocks(imp_t, t_lane):
    jj = lax.broadcasted_iota(jnp.int32, imp_t.shape, 0)
    cb = t_lane >> SEL_SHIFT
    forced = (jj == 0) | (jj == cb) | (jj == cb - 1)
    causal = (jj << SEL_SHIFT) <= t_lane
    x = jnp.where(forced, IMP_FORCED, imp_t)
    x = jnp.where(causal, x, IMP_BLOCKED)
    sel, _ = _take_top(x, SEL_TOPK)
    return sel


def _take_top(x, k):
    nj = x.shape[0]
    jf = lax.broadcasted_iota(jnp.int32, x.shape, 0).astype(F32)
    sel = jnp.zeros_like(x)
    picks = []
    for _ in range(k):
        m = jnp.max(x, axis=0, keepdims=True)
        first = jnp.min(jnp.where(x == m, jf, float(nj)), axis=0, keepdims=True)
        hit = jf == first
        sel = jnp.where(hit, 1.0, sel)
        x = jnp.where(hit, IMP_TAKEN, x)
        picks.append(first)
    return sel, picks


def _softmax_cols(s):
    m = jnp.max(s, axis=0, keepdims=True)
    m = jnp.where(m > 0.5 * NEG_BIG, m, 0.0)
    e = jnp.exp(s - m)
    return e, 1.0 / jnp.maximum(jnp.sum(e, axis=0, keepdims=True), TINY)


def _dot_tn(a, b):
    return lax.dot_general(a, b, (((0,), (0,)), ((), ())), preferred_element_type=F32)


def _nsa_prompt_kernel(q_ref, small_ref, kcv_ref, kcvt_ref, kaug_ref, vaugt_ref, kwin_ref, mselt_ref, out_ref,
                       lt_ref, acc_ref, ot_ref, sa_ref, sb_ref):
    n = pl.program_id(1)
    t0 = n * Q_BLOCK
    nb = kcv_ref.shape[1]
    col_t = t0 + (lax.broadcasted_iota(jnp.int32, (1, COLS), 1) & (Q_BLOCK - 1))
    lane_t = t0 + lax.broadcasted_iota(jnp.int32, (1, Q_BLOCK), 1)
    gates_t = jax.nn.sigmoid(small_ref[0]).T
    q_t = (q_ref[0] * (HEAD_DIM ** -0.5)).T.astype(BF16)
    c_end = lax.broadcasted_iota(jnp.int32, (nb, 1), 0) * CMP_STRIDE + (CMP_LEN - 1)
    o_cmp_t = []

    for h in range(N_KV_HEADS):
        for g in range(GQA):
            lo = (h * GQA + g) * HEAD_DIM
            lt_ref[h, SEL_LANES:SEL_LANES + HEAD_DIM, g * Q_BLOCK:(g + 1) * Q_BLOCK] = q_t[lo:lo + HEAD_DIM, :]
        lt_ref[h, SEL_LANES + HEAD_DIM:, :] = jnp.zeros((HEAD_DIM, COLS), BF16)
        qh_t = lt_ref[h, SEL_LANES:SEL_LANES + HEAD_DIM, :]

        kc = kcv_ref[0, :, h * HEAD_DIM:(h + 1) * HEAD_DIM].astype(BF16)
        vc_t = kcvt_ref[0, (N_KV_HEADS + h) * HEAD_DIM:(N_KV_HEADS + h + 1) * HEAD_DIM, :].astype(BF16)
        e_t, r_t = _softmax_cols(jnp.where(c_end <= col_t, _dot(kc, qh_t), NEG_BIG))
        o_cmp_t.append(_dot(vc_t, e_t.astype(BF16)) * r_t)

        psum = e_t[:, 0:Q_BLOCK] * r_t[:, 0:Q_BLOCK]
        for g in range(1, GQA):
            psum = psum + e_t[:, g * Q_BLOCK:(g + 1) * Q_BLOCK] * r_t[:, g * Q_BLOCK:(g + 1) * Q_BLOCK]
        p_hi = psum.astype(BF16)
        p_lo = (psum - p_hi.astype(F32)).astype(BF16)
        imp_t = _dot(mselt_ref[...], p_hi) + _dot(mselt_ref[...], p_lo)
        neg = jnp.where(_select_blocks(imp_t, lane_t) > 0.0, 0.0, NEG_BIG).astype(BF16)
        for g in range(GQA):
            lt_ref[h, 0:SEL_LANES, g * Q_BLOCK:(g + 1) * Q_BLOCK] = neg

    acc_ref[...] = jnp.zeros(acc_ref.shape, F32)
    n_pairs = (t0 + Q_BLOCK + 2 * SEL_TILE - 1) // (2 * SEL_TILE)

    def scores(s_ref, kt):
        k0 = pl.multiple_of(kt * SEL_TILE, SEL_TILE)
        for h in range(N_KV_HEADS):
            s_ref[h] = _dot(kaug_ref[0, h, pl.ds(k0, SEL_TILE), :], lt_ref[h])

    def consume(s_ref, kt, ms, causal_mask):
        out = []
        for h in range(N_KV_HEADS):
            s = s_ref[h]
            if causal_mask:
                kpos = kt * SEL_TILE + lax.broadcasted_iota(jnp.int32, (SEL_TILE, 1), 0)
                s = jnp.where(kpos <= col_t, s, NEG_BIG)
            m_new = jnp.maximum(ms[h], jnp.max(s, axis=0, keepdims=True))
            pe = jnp.exp(s - m_new).astype(BF16)
            acc_ref[h] = jnp.exp(ms[h] - m_new) * acc_ref[h] + _dot(vaugt_ref[0, h, kt], pe)
            out.append(m_new)
        return tuple(out)

    def pair(i, ms, causal_mask):
        scores(sb_ref, 2 * i + 1)
        ms = consume(sa_ref, 2 * i, ms, causal_mask)
        if not causal_mask:
            scores(sa_ref, 2 * i + 2)
        return consume(sb_ref, 2 * i + 1, ms, causal_mask)

    scores(sa_ref, 0)
    m_init = (jnp.full((1, COLS), NEG_BIG, F32),) * N_KV_HEADS
    ms = lax.fori_loop(0, n_pairs - 1, lambda i, ms: pair(i, ms, False), m_init)
    pair(n_pairs - 1, ms, True)

    w0 = pl.multiple_of(jnp.maximum(t0 - WINDOW, 0), Q_BLOCK)
    kpos = w0 + lax.broadcasted_iota(jnp.int32, (WINDOW + Q_BLOCK, 1), 0)
    dpos = col_t - kpos
    win_ok = (dpos >= 0) & (dpos < WINDOW)
    for h in range(N_KV_HEADS):
        acc = acc_ref[h]
        o_sel_t = acc[:HEAD_DIM] / jnp.maximum(acc[HEAD_DIM:HEAD_DIM + 1], TINY)

        kw = kwin_ref[0, pl.ds(w0, WINDOW + Q_BLOCK), h * HEAD_DIM:(h + 1) * HEAD_DIM]
        vw = kwin_ref[0, pl.ds(w0, WINDOW + Q_BLOCK), (N_KV_HEADS + h) * HEAD_DIM:(N_KV_HEADS + h + 1) * HEAD_DIM]
        qh_t = lt_ref[h, SEL_LANES:SEL_LANES + HEAD_DIM, :]
        e_t, r_t = _softmax_cols(jnp.where(win_ok, _dot(kw, qh_t), NEG_BIG))
        o_win_t = _dot_tn(vw, e_t.astype(BF16)) * r_t

        for g in range(GQA):
            hd = h * GQA + g
            cols = slice(g * Q_BLOCK, (g + 1) * Q_BLOCK)
            ot_ref[hd * HEAD_DIM:(hd + 1) * HEAD_DIM, :] = (
                gates_t[3 * hd:3 * hd + 1, :] * o_cmp_t[h][:, cols] + gates_t[3 * hd + 1:3 * hd + 2, :] * o_sel_t[:, cols]
                + gates_t[3 * hd + 2:3 * hd + 3, :] * o_win_t[:, cols])
    out_ref[0] = ot_ref[...].T


def _cmp_to_sel_t(nb, n_cmp, n_blk):
    cs = np.arange(nb)[None, :] * CMP_STRIDE
    bs = np.arange(SEL_LANES)[:, None] * SEL_BLOCK
    ov = np.clip(np.minimum(cs + CMP_LEN, bs + SEL_BLOCK) - np.maximum(cs, bs), 0, None) / CMP_LEN
    ov = ov * (np.arange(nb)[None, :] < n_cmp) * (np.arange(SEL_LANES)[:, None] < n_blk)
    return jnp.asarray(ov, BF16)


def _nsa_prompt(q, small, kcv, kcv_t, kaug, vaug_t, kwin):
    b, l, _ = q.shape
    nb = kcv.shape[1]
    n_blk = l // SEL_BLOCK
    assert l % (2 * SEL_TILE) == 0 and n_blk <= SEL_LANES and l >= WINDOW + Q_BLOCK
    mselt = _cmp_to_sel_t(nb, (l - CMP_LEN) // CMP_STRIDE + 1, n_blk)

    def per_batch(shape):
        nd = len(shape)
        return pl.BlockSpec((1,) + shape[1:], lambda i, j: (i,) + (0,) * (nd - 1), pipeline_mode=pl.Buffered(1))

    return pl.pallas_call(
        _nsa_prompt_kernel,
        grid=(b, l // Q_BLOCK),
        in_specs=[pl.BlockSpec((1, Q_BLOCK, ATT_DIM), lambda i, j: (i, j, 0)),
                  pl.BlockSpec((1, Q_BLOCK, SMALL_W), lambda i, j: (i, j, 0)),
                  per_batch(kcv.shape), per_batch(kcv_t.shape), per_batch(kaug.shape), per_batch(vaug_t.shape),
                  per_batch(kwin.shape), pl.BlockSpec(mselt.shape, lambda i, j: (0, 0), pipeline_mode=pl.Buffered(1))],
        out_specs=pl.BlockSpec((1, Q_BLOCK, ATT_DIM), lambda i, j: (i, j, 0)),
        out_shape=jax.ShapeDtypeStruct((b, l, ATT_DIM), F32),
        scratch_shapes=[pltpu.VMEM((N_KV_HEADS, AUG, COLS), BF16), pltpu.VMEM((N_KV_HEADS, LANES, COLS), F32),
                        pltpu.VMEM((ATT_DIM, Q_BLOCK), F32), pltpu.VMEM((N_KV_HEADS, SEL_TILE, COLS), F32),
                        pltpu.VMEM((N_KV_HEADS, SEL_TILE, COLS), F32)],
        compiler_params=pltpu.CompilerParams(dimension_semantics=("arbitrary", "arbitrary"),
                                             vmem_limit_bytes=VMEM_LIMIT),
        name="nsa_prompt",
    )(q, small, kcv, kcv_t, kaug, vaug_t, kwin, mselt)


DT_LANE = N_GATE
HEADS_PER_GROUP = SSM_HEADS // N_GROUPS
GROUP_W = D_INNER // N_GROUPS
TAIL = 8


def _expand_heads(v):
    rows = v.shape[0]
    lane = lax.broadcasted_iota(jnp.int32, (rows, LANES), 1)
    tiles = []
    for j in range(D_INNER // LANES):
        a = jnp.broadcast_to(v[:, DT_LANE + 2 * j:DT_LANE + 2 * j + 1], (rows, LANES))
        b = jnp.broadcast_to(v[:, DT_LANE + 2 * j + 1:DT_LANE + 2 * j + 2], (rows, LANES))
        tiles.append(jnp.where(lane < SSM_HEAD_DIM, a, b))
    return jnp.concatenate(tiles, axis=1)


def _cumsum_rows(x):
    n = x.shape[0]
    row = lax.broadcasted_iota(jnp.int32, x.shape, 0)
    s = 1
    while s < n:
        x = x + jnp.where(row >= s, pltpu.roll(x, s, 0), 0.0)
        s *= 2
    return x


def _grouped_norm_gate(y, z, norm):
    v = y * _silu(z)
    outs = []
    for g in range(N_GROUPS):
        vg = v[:, g * GROUP_W:(g + 1) * GROUP_W]
        outs.append(vg * lax.rsqrt(jnp.mean(vg * vg, axis=-1, keepdims=True) + EPS))
    return jnp.concatenate(outs, axis=1) * norm


def _ssd_prompt_kernel(xbc_ref, z_ref, small_ref, convw_ref, convb_ref, dtb_ref, alog_ref, dskip_ref, norm_ref,
                       y_ref, state_ref, xe_ref, st_ref, yd_ref):
    c = pl.program_id(1)
    lc = xbc_ref.shape[1]

    @pl.when(c == 0)
    def _():
        xe_ref[0:TAIL, :] = jnp.zeros((TAIL, CONV_DIM), F32)
        st_ref[...] = jnp.zeros(st_ref.shape, F32)

    xe_ref[TAIL:TAIL + lc, :] = xbc_ref[0]
    conv = convb_ref[...] + xe_ref[TAIL:TAIL + lc, :] * convw_ref[CONV_W - 1:CONV_W, :]
    for k in range(CONV_W - 1):
        conv = conv + xe_ref[pl.ds(TAIL - (CONV_W - 1) + k, lc), :] * convw_ref[k:k + 1, :]
    xe_ref[0:TAIL, :] = xe_ref[lc:lc + TAIL, :]
    xc = _silu(conv)
    xs = xc[:, :D_INNER]

    dt = jax.nn.softplus(small_ref[0] + dtb_ref[...])
    ad = dt * (-jnp.exp(alog_ref[...]))
    acs = _cumsum_rows(ad)
    acs_t = acs.T
    dt_e = _expand_heads(dt)
    acs_e = _expand_heads(acs)
    last_e = acs_e[lc - 1:lc, :]
    xd = xs * dt_e
    xd_bf = xd.astype(BF16)
    xdd_bf = (xd * jnp.exp(last_e - acs_e)).astype(BF16)
    grow = jnp.exp(acs_e)
    li = lax.broadcasted_iota(jnp.int32, (lc, lc), 0)
    si = lax.broadcasted_iota(jnp.int32, (lc, lc), 1)

    y_off = []
    for g in range(N_GROUPS):
        bm = xc[:, D_INNER + g * D_STATE:D_INNER + (g + 1) * D_STATE]
        cm = xc[:, D_INNER + N_GROUPS * D_STATE + g * D_STATE:D_INNER + N_GROUPS * D_STATE + (g + 1) * D_STATE]
        bm_bf = bm.astype(BF16)
        cm_bf = cm.astype(BF16)
        cb = _dot_nt(cm_bf, bm_bf)
        cols = slice(g * GROUP_W, (g + 1) * GROUP_W)
        st_g = st_ref[:, cols]
        y_off.append(_dot(cm_bf, st_g.astype(BF16)))
        for hh in range(HEADS_PER_GROUP):
            h = g * HEADS_PER_GROUP + hh
            seg = acs[:, DT_LANE + h:DT_LANE + h + 1] - acs_t[DT_LANE + h:DT_LANE + h + 1, :]
            m = jnp.where(li >= si, cb * jnp.exp(seg), 0.0).astype(BF16)
            yd_ref[:, h * SSM_HEAD_DIM:(h + 1) * SSM_HEAD_DIM] = _dot(m, xd_bf[:, h * SSM_HEAD_DIM:(h + 1) * SSM_HEAD_DIM])
        st_ref[:, cols] = st_g * jnp.exp(last_e[:, cols]) + _dot(bm.T.astype(BF16), xdd_bf[:, cols])

    y = yd_ref[...] + jnp.concatenate(y_off, axis=1) * grow + dskip_ref[...] * xs
    y_ref[0] = _grouped_norm_gate(y, z_ref[0], norm_ref[...])

    @pl.when(c == pl.num_programs(1) - 1)
    def _():
        state_ref[0] = st_ref[...].T


def _pad_small(v):
    return jnp.zeros((1, SMALL_W), F32).at[0, DT_LANE:DT_LANE + SSM_HEADS].set(v.astype(F32))


def _ssd_prompt(xbc, z, small, conv_w, conv_b, dt_bias, a_log, d_skip, ssm_norm):
    b, l, _ = xbc.shape
    lc = min(SSD_CHUNK, l)
    params = (conv_w, conv_b.reshape(1, CONV_DIM), _pad_small(dt_bias), _pad_small(a_log),
              jnp.repeat(d_skip.astype(F32), SSM_HEAD_DIM).reshape(1, D_INNER), ssm_norm.reshape(1, D_INNER))
    tile = lambda w: pl.BlockSpec((1, lc, w), lambda i, j: (i, j, 0))
    return pl.pallas_call(
        _ssd_prompt_kernel,
        grid=(b, l // lc),
        in_specs=[tile(CONV_DIM), tile(D_INNER), tile(SMALL_W)] + [_const_spec(p.shape) for p in params],
        out_specs=[tile(D_INNER), pl.BlockSpec((1, D_INNER, D_STATE), lambda i, j: (i, 0, 0))],
        out_shape=[jax.ShapeDtypeStruct((b, l, D_INNER), F32), jax.ShapeDtypeStruct((b, D_INNER, D_STATE), F32)],
        scratch_shapes=[pltpu.VMEM((lc + TAIL, CONV_DIM), F32), pltpu.VMEM((D_STATE, D_INNER), F32),
                        pltpu.VMEM((lc, D_INNER), F32)],
        compiler_params=pltpu.CompilerParams(dimension_semantics=("arbitrary", "arbitrary"),
                                             vmem_limit_bytes=VMEM_LIMIT),
        name="ssd_prompt",
    )(xbc, z, small, *params)


def _out_ffn_kernel(h_ref, att_ref, ssm_ref, p_ref, woa_ref, wos_ref, g2_ref, wg_ref, wu_ref, wd_ref,
                    gp_ref, wpg_ref, wple_ref, gf_ref, y_ref, acc_ref):
    h = h_ref[...] + _dot(att_ref[...].astype(BF16), woa_ref[...]) + _dot(ssm_ref[...].astype(BF16), wos_ref[...])
    h = _swiglu_half_step(h, g2_ref, wg_ref, wu_ref, wd_ref, acc_ref)
    gate = jax.nn.sigmoid(_dot(_rms(h, gp_ref[...]).astype(BF16), wpg_ref[...]))
    h = h + gate * _dot(p_ref[...].astype(BF16), wple_ref[...])
    y_ref[...] = _rms(h, gf_ref[...])


def _out_ffn(h, att, ssm, p, weights, tm):
    n = h.shape[0]
    return pl.pallas_call(
        _out_ffn_kernel,
        grid=(n // tm,),
        in_specs=[_row_spec(tm, D_MODEL), _row_spec(tm, ATT_DIM), _row_spec(tm, D_INNER), _row_spec(tm, PLE_DIM)]
        + [_const_spec(w.shape) for w in weights],
        out_specs=_row_spec(tm, D_MODEL),
        out_shape=jax.ShapeDtypeStruct((n, D_MODEL), F32),
        scratch_shapes=[pltpu.VMEM((tm, D_MODEL), F32)],
        compiler_params=pltpu.CompilerParams(dimension_semantics=("arbitrary",), vmem_limit_bytes=VMEM_LIMIT),
        name="outproj_ffn2_ple",
    )(h, att, ssm, p, *weights)


BLOCKS_PER_PAGE = PAGE_SIZE // CMP_STRIDE
HEAD_ROWS = 8


def _head_rows_of(h):
    row = lax.broadcasted_iota(jnp.int32, (HEAD_ROWS, 1), 0)
    return (row >= h * GQA) & (row < (h + 1) * GQA)


def _cmp_sample_kernel(pt_ref, q_ref, cache_ref, wbd_ref, pe_ref, b1_ref, w2_ref, mselt_ref,
                       ocmp_ref, imp_ref, xbuf_ref, xrow_ref, lhs_ref, sem_ref, *, n_cmp):
    b = pl.program_id(0)
    nb_total = pl.num_programs(0)
    n_pages = pt_ref.shape[1]
    nbp = n_pages * BLOCKS_PER_PAGE

    def page_copy(bb, slot, i):
        return pltpu.make_async_copy(cache_ref.at[pt_ref[bb, i], 0], xbuf_ref.at[slot, i], sem_ref.at[slot])

    def start_all(bb, slot):
        def body(i2, carry):
            for prio in range(2):
                page_copy(bb, slot, 2 * i2 + prio).start(priority=prio)
            return carry
        lax.fori_loop(0, n_pages // 2, body, 0)

    def wait_all(bb, slot):
        def body(i, carry):
            page_copy(bb, slot, i).wait()
            return carry
        lax.fori_loop(0, n_pages, body, 0)

    slot = b % 2

    @pl.when(b == 0)
    def _():
        start_all(0, 0)

    @pl.when(b + 1 < nb_total)
    def _():
        start_all(b + 1, 1 - slot)

    wait_all(b, slot)

    def load_rows(c, s):
        if s == 0:
            for i in range(n_pages):
                xrow_ref[c, i * PAGE_SIZE:(i + 1) * PAGE_SIZE, :] = xbuf_ref[slot, i, c].reshape(HD2, PAGE_SIZE).T
        return xrow_ref[c, pl.ds(s, nbp, stride=CMP_STRIDE), :]

    kcv = _cmp_tokens(load_rows, nbp, lhs_ref, wbd_ref, pe_ref, b1_ref, w2_ref)
    q8 = (q_ref[0] * (HEAD_DIM ** -0.5)).astype(BF16)
    valid = lax.broadcasted_iota(jnp.int32, (1, nbp), 1) < n_cmp
    o_cmp = jnp.zeros((HEAD_ROWS, HEAD_DIM), F32)
    psum = jnp.zeros((HEAD_ROWS, nbp), F32)
    row = lax.broadcasted_iota(jnp.int32, (HEAD_ROWS, 1), 0)
    for h in range(N_KV_HEADS):
        kc = kcv[:, h * HEAD_DIM:(h + 1) * HEAD_DIM].astype(BF16)
        vc = kcv[:, (N_KV_HEADS + h) * HEAD_DIM:(N_KV_HEADS + h + 1) * HEAD_DIM].astype(BF16)
        p = _softmax_rows(jnp.where(valid, _dot_nt(q8, kc), NEG_BIG))
        mine = _head_rows_of(h)
        o_cmp = jnp.where(mine, _dot(p.astype(BF16), vc), o_cmp)
        ph = jnp.sum(jnp.where(mine, p, 0.0), axis=0, keepdims=True)
        psum = jnp.where(row == h, ph, psum)
    ocmp_ref[0] = o_cmp
    p_hi = psum.astype(BF16)
    p_lo = (psum - p_hi.astype(F32)).astype(BF16)
    imp_ref[0] = _dot_nt(p_hi, mselt_ref[...]) + _dot_nt(p_lo, mselt_ref[...])


def _cmp_sample(page_table, q8, cache_t, wbd, pe_rows, b1, w2, mselt, n_cmp):
    db, n_pages = page_table.shape
    assert n_pages % 2 == 0
    consts = (wbd, pe_rows, b1, w2, mselt)
    grid_spec = pltpu.PrefetchScalarGridSpec(
        num_scalar_prefetch=1,
        grid=(db,),
        in_specs=[pl.BlockSpec((1, HEAD_ROWS, HEAD_DIM), lambda i, pt: (i, 0, 0)),
                  pl.BlockSpec(memory_space=pl.ANY)]
        + [pl.BlockSpec(c.shape, lambda i, pt, nd=c.ndim: (0,) * nd, pipeline_mode=pl.Buffered(1)) for c in consts],
        out_specs=[pl.BlockSpec((1, HEAD_ROWS, HEAD_DIM), lambda i, pt: (i, 0, 0)),
                   pl.BlockSpec((1, HEAD_ROWS, SEL_LANES), lambda i, pt: (i, 0, 0))],
        scratch_shapes=[pltpu.VMEM((2, n_pages, 2, N_KV_HEADS, HEAD_DIM, PAGE_SIZE), F32),
                        pltpu.VMEM((2, n_pages * PAGE_SIZE, HD2), F32),
                        pltpu.VMEM((n_pages * BLOCKS_PER_PAGE + PE_ROWS, CMP_K), BF16), pltpu.SemaphoreType.DMA((2,))],
    )
    return pl.pallas_call(
        functools.partial(_cmp_sample_kernel, n_cmp=n_cmp),
        grid_spec=grid_spec,
        out_shape=[jax.ShapeDtypeStruct((db, HEAD_ROWS, HEAD_DIM), F32),
                   jax.ShapeDtypeStruct((db, HEAD_ROWS, SEL_LANES), F32)],
        compiler_params=pltpu.CompilerParams(dimension_semantics=("arbitrary",), vmem_limit_bytes=VMEM_LIMIT),
        name="nsa_cmp_sample",
    )(page_table, q8, cache_t, *consts)


N_PAST_PICKS = SEL_TOPK - 1


def _topk_sample_kernel(imp_ref, idx_ref, *, past_blk):
    x = imp_ref[...].T
    jj = lax.broadcasted_iota(jnp.int32, x.shape, 0)
    x = jnp.where((jj == 0) | (jj == past_blk - 1), IMP_FORCED, x)
    x = jnp.where(jj < past_blk, x, IMP_BLOCKED)
    _, picks = _take_top(x, N_PAST_PICKS)
    picks = picks + [jnp.zeros_like(picks[0])] * (idx_ref.shape[0] - N_PAST_PICKS)
    idx_ref[...] = jnp.concatenate(picks, axis=0).astype(jnp.int32)


def _topk_sample(imp2, past_blk):
    nq = imp2.shape[0]
    assert N_PAST_PICKS <= past_blk <= SEL_LANES
    return pl.pallas_call(
        functools.partial(_topk_sample_kernel, past_blk=past_blk),
        out_shape=jax.ShapeDtypeStruct((SEL_TOPK, nq), jnp.int32),
        name="nsa_topk_sample",
    )(imp2)


SUB_PER_PAGE = PAGE_SIZE // SEL_BLOCK
SUB_SHIFT = SUB_PER_PAGE.bit_length() - 1
SEL_SHIFT_IN_PAGE = SEL_BLOCK.bit_length() - 1
N_SEL_COLS = N_PAST_PICKS * PAGE_SIZE


def _attend_one_token(q8, q8f, kt, vt, valid, k_new, v_new):
    s = jnp.where(valid, _dot(q8, kt.astype(BF16)), NEG_BIG)
    s_new = jnp.sum(q8f * k_new, axis=-1, keepdims=True)
    m = jnp.maximum(jnp.max(s, axis=-1, keepdims=True), s_new)
    e = jnp.exp(s - m)
    e_new = jnp.exp(s_new - m)
    norm = jnp.maximum(jnp.sum(e, axis=-1, keepdims=True) + e_new, TINY)
    return (_dot_nt(e.astype(BF16), vt.astype(BF16)) + e_new * v_new) / norm


def _sel_win_sample_kernel(idx_ref, pt_ref, q_ref, gate_ref, ocmp_ref, kvs_ref, kvw_ref, win_ref, slc_ref,
                           att_ref, newwin_ref, gbuf_ref, sem_ref):
    b = pl.program_id(0)
    nb_total = pl.num_programs(0)
    w_buf = win_ref.shape[-1]

    def block_copies(bb, slot, h, k):
        j = idx_ref[k, bb * N_KV_HEADS + h]
        page = pt_ref[bb, j >> SUB_SHIFT]
        return [pltpu.make_async_copy(slc_ref.at[page, 0, c, h],
                                      gbuf_ref.at[slot, h, c, :, pl.ds(k * PAGE_SIZE, PAGE_SIZE)],
                                      sem_ref.at[slot]) for c in range(2)]

    def start_all(bb, slot):
        for h in range(N_KV_HEADS):
            for k in range(N_PAST_PICKS):
                for cp in block_copies(bb, slot, h, k):
                    cp.start()

    def wait_all(bb, slot):
        for h in range(N_KV_HEADS):
            for k in range(N_PAST_PICKS):
                for cp in block_copies(bb, slot, h, k):
                    cp.wait()

    slot = b % 2

    @pl.when(b == 0)
    def _():
        start_all(0, 0)

    @pl.when(b + 1 < nb_total)
    def _():
        start_all(b + 1, 1 - slot)

    wait_all(b, slot)

    q8f = q_ref[0] * (HEAD_DIM ** -0.5)
    q8 = q8f.astype(BF16)
    page_lane = lax.broadcasted_iota(jnp.int32, (1, PAGE_SIZE), 1)
    win_lane = lax.broadcasted_iota(jnp.int32, (1, w_buf), 1)
    win_valid = w_buf - win_lane < WINDOW
    o_sel = jnp.zeros((HEAD_ROWS, HEAD_DIM), F32)
    o_win = jnp.zeros((HEAD_ROWS, HEAD_DIM), F32)
    for h in range(N_KV_HEADS):
        kcols = slice(h * HEAD_DIM, (h + 1) * HEAD_DIM)
        vcols = slice((N_KV_HEADS + h) * HEAD_DIM, (N_KV_HEADS + h + 1) * HEAD_DIM)
        mine = _head_rows_of(h)
        sel_valid = jnp.concatenate(
            [(page_lane >> SEL_SHIFT_IN_PAGE) == (idx_ref[k, b * N_KV_HEADS + h] & (SUB_PER_PAGE - 1))
             for k in range(N_PAST_PICKS)], axis=1)
        o_sel = jnp.where(mine, _attend_one_token(q8, q8f, gbuf_ref[slot, h, 0], gbuf_ref[slot, h, 1], sel_valid,
                                                  kvs_ref[0][:, kcols], kvs_ref[0][:, vcols]), o_sel)
        o_win = jnp.where(mine, _attend_one_token(q8, q8f, win_ref[0, 0, 0, h], win_ref[0, 0, 1, h], win_valid,
                                                  kvw_ref[0][:, kcols], kvw_ref[0][:, vcols]), o_win)
    gates = jax.nn.sigmoid(gate_ref[0])
    att_ref[0] = gates[:, 0:1] * ocmp_ref[0] + gates[:, 1:2] * o_sel + gates[:, 2:3] * o_win

    new_col = jnp.concatenate([kvw_ref[0], jnp.zeros((LANES - 1, KV_DIM), F32)], axis=0).T
    tile_lane = lax.broadcasted_iota(jnp.int32, (HEAD_DIM, w_buf), 1)
    for c in range(2):
        for h in range(N_KV_HEADS):
            lo = (c * N_KV_HEADS + h) * HEAD_DIM
            newwin_ref[0, 0, c, h] = jnp.where(tile_lane == w_buf - 1, new_col[lo:lo + HEAD_DIM, 0:1],
                                               pltpu.roll(win_ref[0, 0, c, h], w_buf - 1, 1))


def _sel_win_sample(idx, page_table, q8, gate8, o_cmp, kvs_new, kvw_new, win_t, slc_t):
    db = page_table.shape[0]
    w_buf = win_t.shape[-1]
    assert w_buf == WINDOW and win_t.shape[1] == 1
    per_tok = lambda shape: pl.BlockSpec((1,) + shape[1:], lambda i, idx, pt, nd=len(shape): (i,) + (0,) * (nd - 1))
    grid_spec = pltpu.PrefetchScalarGridSpec(
        num_scalar_prefetch=2,
        grid=(db,),
        in_specs=[per_tok(q8.shape), per_tok(gate8.shape), per_tok(o_cmp.shape), per_tok(kvs_new.shape),
                  per_tok(kvw_new.shape), per_tok(win_t.shape), pl.BlockSpec(memory_space=pl.ANY)],
        out_specs=[per_tok(o_cmp.shape), per_tok(win_t.shape)],
        scratch_shapes=[pltpu.VMEM((2, N_KV_HEADS, 2, HEAD_DIM, N_SEL_COLS), F32), pltpu.SemaphoreType.DMA((2,))],
    )
    return pl.pallas_call(
        _sel_win_sample_kernel,
        grid_spec=grid_spec,
        out_shape=[jax.ShapeDtypeStruct(o_cmp.shape, F32), jax.ShapeDtypeStruct(win_t.shape, F32)],
        compiler_params=pltpu.CompilerParams(dimension_semantics=("arbitrary",), vmem_limit_bytes=VMEM_LIMIT),
        name="nsa_sel_win_sample",
    )(idx, page_table, q8, gate8, o_cmp, kvs_new, kvw_new, win_t, slc_t)


SSD_BT = 8


def _ssd_sample_kernel(xbc_ref, cst_ref, z_ref, small_ref, h0_ref, convw_ref, convb_ref, dtb_ref, alog_ref,
                       dskip_ref, norm_ref, y_ref, hnew_ref, ys_ref):
    conv = convb_ref[...] + xbc_ref[...] * convw_ref[CONV_W - 1:CONV_W, :]
    for k in range(CONV_W - 1):
        conv = conv + cst_ref[k] * convw_ref[k:k + 1, :]
    xc = _silu(conv)
    xs = xc[:, :D_INNER]
    dt = jax.nn.softplus(small_ref[...] + dtb_ref[...])
    dt_e = _expand_heads(dt)
    decay_e = jnp.exp(dt_e * _expand_heads(-jnp.exp(alog_ref[...])))
    xd = xs * dt_e
    fill = jnp.zeros((LANES - SSD_BT, D_INNER), F32)
    xd_t = jnp.concatenate([xd, fill], axis=0).T
    decay_t = jnp.concatenate([decay_e, fill], axis=0).T
    lane = lax.broadcasted_iota(jnp.int32, (1, D_INNER), 1)
    for i in range(SSD_BT):
        bsel = jnp.concatenate(
            [jnp.broadcast_to(xc[i:i + 1, D_INNER + g * D_STATE:D_INNER + (g + 1) * D_STATE], (GROUP_W, D_STATE))
             for g in range(N_GROUPS)], axis=0)
        h0 = h0_ref[i].reshape(D_INNER, D_STATE)
        hn = decay_t[:, i:i + 1] * h0 + xd_t[:, i:i + 1] * bsel
        hnew_ref[i] = hn.reshape(SSM_HEADS, SSM_HEAD_DIM, D_STATE)
        c8 = jnp.concatenate(
            [xc[i:i + 1, D_INNER + (N_GROUPS + g) * D_STATE:D_INNER + (N_GROUPS + g + 1) * D_STATE]
             for g in range(N_GROUPS)] + [jnp.zeros((HEAD_ROWS - N_GROUPS, D_STATE), F32)], axis=0)
        y8 = _dot_nt(c8.astype(BF16), hn.astype(BF16))
        ys_ref[i:i + 1, :] = jnp.where(lane < GROUP_W, y8[0:1], y8[1:2])
    y = ys_ref[...] + dskip_ref[...] * xs
    y_ref[...] = _grouped_norm_gate(y, z_ref[...], norm_ref[...])


def _ssd_sample(xbc, conv_state_t, z, small, h0, conv_w, conv_b, dt_bias, a_log, d_skip, ssm_norm):
    db = xbc.shape[0]
    assert db % SSD_BT == 0 and N_GROUPS == 2
    params = (conv_w, conv_b.reshape(1, CONV_DIM), _pad_small(dt_bias), _pad_small(a_log),
              jnp.repeat(d_skip.astype(F32), SSM_HEAD_DIM).reshape(1, D_INNER), ssm_norm.reshape(1, D_INNER))
    rows = lambda w: pl.BlockSpec((SSD_BT, w), lambda i: (i, 0))
    state_spec = pl.BlockSpec((SSD_BT, SSM_HEADS, SSM_HEAD_DIM, D_STATE), lambda i: (i, 0, 0, 0))
    return pl.pallas_call(
        _ssd_sample_kernel,
        grid=(db // SSD_BT,),
        in_specs=[rows(CONV_DIM), pl.BlockSpec((CONV_W - 1, SSD_BT, CONV_DIM), lambda i: (0, i, 0)), rows(D_INNER),
                  rows(SMALL_W), state_spec] + [_const_spec(p.shape) for p in params],
        out_specs=[rows(D_INNER), state_spec],
        out_shape=[jax.ShapeDtypeStruct((db, D_INNER), F32), jax.ShapeDtypeStruct(h0.shape, F32)],
        scratch_shapes=[pltpu.VMEM((SSD_BT, D_INNER), F32)],
        compiler_params=pltpu.CompilerParams(dimension_semantics=("arbitrary",), vmem_limit_bytes=VMEM_LIMIT),
        name="ssd_sample",
    )(xbc, conv_state_t, z, small, h0, *params)


PROMPT_TM = 512


def _ffn_weights(w_gate, w_up, w_down):
    chunked = lambda w: jnp.transpose(w.astype(BF16).reshape(D_MODEL, N_FF_CHUNKS, FF_CHUNK), (1, 0, 2))
    return chunked(w_gate), chunked(w_up), w_down.astype(BF16).reshape(N_FF_CHUNKS, FF_CHUNK, D_MODEL)


def _in_proj_weight(w_in):
    cuts = np.cumsum([ATT_DIM, KV_DIM, KV_DIM, KV_DIM, N_GATE, D_INNER, CONV_DIM]).tolist()
    q, kvc, kvs, kvw, g, z, xbc, dt = jnp.split(w_in, cuts, axis=-1)
    pad = jnp.zeros((D_MODEL, SMALL_W - N_GATE - SSM_HEADS), w_in.dtype)
    return jnp.concatenate([q, kvc, kvs, kvw, z, xbc, g, dt, pad], axis=-1).astype(BF16)


def kernel(x_prompt, x_sample, cache_cmp_kv, cache_slc_kv, cache_win_kv, state_conv, state_ssm, page_table,
           p_prompt, p_sample, ffn1_norm, ffn1_w_gate, ffn1_w_up, ffn1_w_down, mix_norm, w_in,
           cmp_w1, cmp_pe, cmp_b1, cmp_w2, conv_w, conv_b, dt_bias, a_log, d_skip, ssm_norm, w_out,
           ffn2_norm, ffn2_w_gate, ffn2_w_up, ffn2_w_down, ple_norm, w_ple_gate, w_ple, final_norm):
    b, l, _ = x_prompt.shape
    db, t_new, _ = x_sample.shape
    depth = ffn1_norm.shape[0]
    n_pool = cache_cmp_kv.shape[0]
    n_pages = page_table.shape[1]
    past = n_pages * PAGE_SIZE
    assert depth == 1 and t_new == 1 and past >= WINDOW and (b * l) % PROMPT_TM == 0
    i = 0
    row = lambda v: v.reshape(1, -1).astype(F32)

    ffn1 = _ffn_weights(ffn1_w_gate[i], ffn1_w_up[i], ffn1_w_down[i])
    stage1_w = (row(ffn1_norm[i]),) + ffn1 + (row(mix_norm[i]), _in_proj_weight(w_in[i]))
    cmp_w = _cmp_weights(cmp_w1[i], cmp_pe[i], cmp_b1[i], cmp_w2[i])
    ssm_w = (conv_w[i], conv_b[i], dt_bias[i], a_log[i], d_skip[i], ssm_norm[i])
    w_o = w_out[i].astype(BF16)
    stage3_w = (w_o[:ATT_DIM], w_o[ATT_DIM:], row(ffn2_norm[i])) + _ffn_weights(ffn2_w_gate[i], ffn2_w_up[i], ffn2_w_down[i]) + (
        row(ple_norm[i]), w_ple_gate[i].astype(BF16), w_ple[i].astype(BF16), row(final_norm))
    kv6 = lambda a, lead, rows: a.reshape(lead, 1, rows, 2, N_KV_HEADS, HEAD_DIM)

    h, q, kvc, kvs, kvw, z, xbc, small, kvc_t, kvs_t, kaug, vaug_t, kwin = _ffn_inproj(
        x_prompt.reshape(b * l, D_MODEL), *stage1_w, PROMPT_TM, cache_batch=b)
    per_b = lambda a: a.reshape(b, l, a.shape[-1])
    kcv, kcv_t = _cmp_prompt(per_b(kvc), *cmp_w)
    att = _nsa_prompt(per_b(q), per_b(small), kcv, kcv_t, kaug, vaug_t, per_b(kwin))
    ssm, ssm_state = _ssd_prompt(per_b(xbc), per_b(z), per_b(small), *ssm_w)
    y_prompt = _out_ffn(h, att.reshape(b * l, ATT_DIM), ssm.reshape(b * l, D_INNER),
                        p_prompt[i].reshape(b * l, PLE_DIM), stage3_w, PROMPT_TM).reshape(b, l, D_MODEL)
    keep = min(WINDOW, l)
    rows_major = lambda a: jnp.transpose(a, (0, 1, 5, 2, 3, 4))
    new_cmp_p = rows_major(kvc_t)
    new_slc_p = rows_major(kvs_t)
    new_win_p = kv6(per_b(kvw)[:, l - keep:], b, keep)
    new_conv_p = per_b(xbc)[:, l - (CONV_W - 1):].reshape(b, 1, CONV_W - 1, CONV_DIM)
    new_ssm_p = ssm_state.reshape(b, 1, SSM_HEADS, SSM_HEAD_DIM, D_STATE)

    hs, qs, kvc_s, kvs_s, kvw_s, z_s, xbc_s, small_s = _ffn_inproj(x_sample.reshape(db, D_MODEL), *stage1_w, db)
    q8 = qs.reshape(db, N_HEADS, HEAD_DIM)
    n_cmp_s = (past + t_new - CMP_LEN) // CMP_STRIDE + 1
    past_blk = past // SEL_BLOCK
    mselt_s = _cmp_to_sel_t(n_pages * BLOCKS_PER_PAGE, n_cmp_s, past_blk)
    row_minor = lambda a: jnp.transpose(a, (0, 1, 3, 4, 5, 2))
    o_cmp, imp = _cmp_sample(page_table, q8, row_minor(cache_cmp_kv), *cmp_w, mselt_s, n_cmp_s)
    n_query = db * N_KV_HEADS
    imp2 = jnp.pad(imp[:, :N_KV_HEADS].reshape(n_query, SEL_LANES), ((0, -n_query % LANES), (0, 0)))
    idx = _topk_sample(imp2, past_blk)
    att_s, new_win_t = _sel_win_sample(
        idx, page_table, q8, small_s[:, :N_GATE].reshape(db, N_HEADS, 3), o_cmp,
        kvs_s.reshape(db, 1, KV_DIM), kvw_s.reshape(db, 1, KV_DIM),
        row_minor(cache_win_kv), row_minor(cache_slc_kv))
    new_win_s = rows_major(new_win_t)
    ssm_s, ssm_state_s = _ssd_sample(xbc_s, jnp.transpose(state_conv[:, i], (1, 0, 2)), z_s, small_s,
                                     state_ssm[:, i], *ssm_w)
    y_sample = _out_ffn(hs, att_s.reshape(db, ATT_DIM), ssm_s, p_sample[i].reshape(db, PLE_DIM),
                        stage3_w, db).reshape(db, 1, D_MODEL)
    new_conv_s = jnp.concatenate([state_conv[:, i, 1:], xbc_s[:, None]], axis=1)[:, None]

    return (y_prompt, y_sample, new_cmp_p, new_slc_p, new_win_p, new_conv_p, new_ssm_p,
            kv6(kvc_s, db, 1), kv6(kvs_s, db, 1), new_win_s, new_conv_s, ssm_state_s[:, None])
```

```python
import functools

import jax
import jax.numpy as jnp
import numpy as np
from jax import lax
from jax.experimental import pallas as pl
from jax.experimental.pallas import tpu as pltpu

F32 = jnp.float32
BF16 = jnp.bfloat16

D_MODEL = 1024
N_HEADS = 8
N_KV_HEADS = 2
HEAD_DIM = 64
GQA = N_HEADS // N_KV_HEADS
ATT_DIM = N_HEADS * HEAD_DIM
KV_DIM = 2 * N_KV_HEADS * HEAD_DIM
CMP_LEN = 32
CMP_STRIDE = 16
CMP_HID = 128
SEL_BLOCK = 64
SEL_TOPK = 16
WINDOW = 512
Q_BLOCK = 128
SSM_HEADS = 8
SSM_HEAD_DIM = 64
D_INNER = SSM_HEADS * SSM_HEAD_DIM
N_GROUPS = 2
D_STATE = 128
CONV_W = 4
CONV_DIM = D_INNER + 2 * N_GROUPS * D_STATE
SSD_CHUNK = 128
D_FF = 2816
PLE_DIM = 256
PAGE_SIZE = 128
EPS = 1e-6

LANES = 128
FF_CHUNK = 256
N_FF_CHUNKS = D_FF // FF_CHUNK
SMALL_W = LANES
N_GATE = 3 * N_HEADS
IN_PROJ_PAD = ATT_DIM + 3 * KV_DIM + D_INNER + CONV_DIM + SMALL_W
VMEM_LIMIT = 56 * 1024 * 1024
NEG_BIG = -1e30
IMP_FORCED = 3e38
IMP_BLOCKED = -1e38
IMP_TAKEN = -3e38


def _dot(a, b):
    return jnp.dot(a, b, preferred_element_type=F32)


def _dot_nt(a, b):
    return lax.dot_general(a, b, (((1,), (1,)), ((), ())), preferred_element_type=F32)


def _rms(x, g):
    return x * lax.rsqrt(jnp.mean(x * x, axis=-1, keepdims=True) + EPS) * g


def _silu(x):
    return x * jax.nn.sigmoid(x)


def _const_spec(shape):
    nd = len(shape)
    return pl.BlockSpec(shape, lambda *_: (0,) * nd, pipeline_mode=pl.Buffered(1))


def _row_spec(tm, width):
    return pl.BlockSpec((tm, width), lambda i: (i, 0))


def _swiglu_half_step(x, g_ref, wg_ref, wu_ref, wd_ref, acc_ref):
    xn = _rms(x, g_ref[...]).astype(BF16)
    acc_ref[...] = jnp.zeros_like(acc_ref)

    def body(c, carry):
        a = _dot(xn, wg_ref[c])
        b = _dot(xn, wu_ref[c])
        hm = (_silu(a) * b).astype(BF16)
        acc_ref[...] += _dot(hm, wd_ref[c])
        return carry

    lax.fori_loop(0, N_FF_CHUNKS, body, 0, unroll=True)
    return x + 0.5 * acc_ref[...]


_IN_SEGS = (ATT_DIM, KV_DIM, KV_DIM, KV_DIM, D_INNER, CONV_DIM, SMALL_W)


SEL_TILE = 512
SEL_LANES = 128
AUG = SEL_LANES + 2 * HEAD_DIM
SEL_SHIFT = SEL_BLOCK.bit_length() - 1
VAUG_ROWS = HEAD_DIM + 16


def _ffn_inproj_kernel(x_ref, g1_ref, wg_ref, wu_ref, wd_ref, gm_ref, win_ref,
                       h_ref, q_ref, kvc_ref, kvs_ref, kvw_ref, z_ref, xbc_ref, small_ref, *rest, tiles):
    *prompt_refs, acc_ref = rest
    tm = x_ref.shape[0]
    h = _swiglu_half_step(x_ref[...], g1_ref, wg_ref, wu_ref, wd_ref, acc_ref)
    h_ref[...] = h
    hn = _rms(h, gm_ref[...]).astype(BF16)
    off = 0
    for out_ref, width in zip((q_ref, kvc_ref, kvs_ref, kvw_ref, z_ref, xbc_ref, small_ref), _IN_SEGS):
        out_ref[...] = _dot(hn, win_ref[:, off:off + width])
        off += width
    if not prompt_refs:
        return
    kvct_ref, kvst_ref, kaug_ref, vaugt_ref, kwin_ref = prompt_refs
    kvs = kvs_ref[...]
    kvc_t = kvc_ref[...].T
    kvs_t = kvs.T
    for ch in range(2 * N_KV_HEADS):
        rows = slice(ch * HEAD_DIM, (ch + 1) * HEAD_DIM)
        kvct_ref[0, 0, ch // N_KV_HEADS, ch % N_KV_HEADS] = kvc_t[rows, :]
        kvst_ref[0, 0, ch // N_KV_HEADS, ch % N_KV_HEADS] = kvs_t[rows, :]
    kwin_ref[...] = kvw_ref[...].astype(BF16)
    pos = (pl.program_id(0) % tiles) * tm + lax.broadcasted_iota(jnp.int32, (tm, 1), 0)
    onehot = ((pos >> SEL_SHIFT) == lax.broadcasted_iota(jnp.int32, (1, SEL_LANES), 1)).astype(BF16)
    ones_row = (lax.broadcasted_iota(jnp.int32, (VAUG_ROWS - HEAD_DIM, tm), 0) == 0).astype(BF16)
    for hh in range(N_KV_HEADS):
        kaug_ref[0, hh, :, 0:SEL_LANES] = onehot
        kaug_ref[0, hh, :, SEL_LANES:SEL_LANES + HEAD_DIM] = kvs[:, hh * HEAD_DIM:(hh + 1) * HEAD_DIM].astype(BF16)
        kaug_ref[0, hh, :, SEL_LANES + HEAD_DIM:] = jnp.zeros((tm, HEAD_DIM), BF16)
        vaugt_ref[0, hh, 0, 0:HEAD_DIM, :] = kvs_t[(N_KV_HEADS + hh) * HEAD_DIM:(N_KV_HEADS + hh + 1) * HEAD_DIM, :].astype(BF16)
        vaugt_ref[0, hh, 0, HEAD_DIM:, :] = ones_row


def _ffn_inproj(x, g1, wg, wu, wd, gm, win, tm, cache_batch=None):
    n = x.shape[0]
    outs = [jax.ShapeDtypeStruct((n, D_MODEL), F32)] + [jax.ShapeDtypeStruct((n, w), F32) for w in _IN_SEGS]
    out_specs = [_row_spec(tm, D_MODEL)] + [_row_spec(tm, w) for w in _IN_SEGS]
    tiles = 1
    if cache_batch is not None:
        l = n // cache_batch
        tiles = l // tm
        assert l % tm == 0 and tm == SEL_TILE
        outs += [jax.ShapeDtypeStruct((cache_batch, 1, 2, N_KV_HEADS, HEAD_DIM, l), F32)] * 2
        out_specs += [pl.BlockSpec((1, 1, 2, N_KV_HEADS, HEAD_DIM, tm),
                                   lambda i: (i // tiles, 0, 0, 0, 0, i % tiles))] * 2
        outs += [jax.ShapeDtypeStruct((cache_batch, N_KV_HEADS, l, AUG), BF16),
                 jax.ShapeDtypeStruct((cache_batch, N_KV_HEADS, tiles, VAUG_ROWS, tm), BF16),
                 jax.ShapeDtypeStruct((n, KV_DIM), BF16)]
        out_specs += [pl.BlockSpec((1, N_KV_HEADS, tm, AUG), lambda i: (i // tiles, 0, i % tiles, 0)),
                      pl.BlockSpec((1, N_KV_HEADS, 1, VAUG_ROWS, tm), lambda i: (i // tiles, 0, i % tiles, 0, 0)),
                      _row_spec(tm, KV_DIM)]
    return pl.pallas_call(
        functools.partial(_ffn_inproj_kernel, tiles=tiles),
        grid=(n // tm,),
        in_specs=[_row_spec(tm, D_MODEL), _const_spec(g1.shape), _const_spec(wg.shape), _const_spec(wu.shape),
                  _const_spec(wd.shape), _const_spec(gm.shape), _const_spec(win.shape)],
        out_specs=out_specs,
        out_shape=outs,
        scratch_shapes=[pltpu.VMEM((tm, D_MODEL), F32)],
        compiler_params=pltpu.CompilerParams(dimension_semantics=("arbitrary",), vmem_limit_bytes=VMEM_LIMIT),
        name="ffn1_inproj",
    )(x, g1, wg, wu, wd, gm, win)


HD2 = N_KV_HEADS * HEAD_DIM
CMP_R = CMP_LEN // CMP_STRIDE
CMP_PROJ = N_KV_HEADS * CMP_R * CMP_HID
PE_ROWS = 16
CMP_K = CMP_STRIDE * HD2


def _cmp_weights(cmp_w1, cmp_pe, cmp_b1, cmp_w2):
    w1r = cmp_w1.reshape(2, CMP_R, CMP_STRIDE, HEAD_DIM, CMP_HID)
    per = cmp_pe.reshape(2, CMP_R, CMP_STRIDE, HEAD_DIM)
    wbd = jnp.zeros((2, CMP_STRIDE, N_KV_HEADS, HEAD_DIM, N_KV_HEADS, CMP_R, CMP_HID), F32)
    blk = jnp.transpose(w1r, (0, 2, 3, 1, 4))
    for h in range(N_KV_HEADS):
        wbd = wbd.at[:, :, h, :, h, :, :].set(blk)
    wbd = wbd.reshape(2, CMP_K, CMP_PROJ).astype(BF16)
    pe_rows = jnp.broadcast_to(per[:, :, :, None, :], (2, CMP_R, CMP_STRIDE, N_KV_HEADS, HEAD_DIM))
    pe_rows = jnp.concatenate([pe_rows.reshape(2, CMP_R, CMP_K), jnp.zeros((2, PE_ROWS - CMP_R, CMP_K), F32)], axis=1)
    return wbd, pe_rows.astype(BF16), cmp_b1, cmp_w2.astype(BF16)


def _cmp_tokens(load_rows, nb, lhs_ref, wbd_ref, pe_ref, b1_ref, w2_ref):
    outs = []
    for c in range(2):
        for s in range(CMP_STRIDE):
            lhs_ref[0:nb, s * HD2:(s + 1) * HD2] = load_rows(c, s).astype(BF16)
        lhs_ref[nb:nb + PE_ROWS, :] = pe_ref[c]
        acc = _dot(lhs_ref[...], wbd_ref[c])
        proj, pe_proj = acc[:nb], acc[nb:]
        for h in range(N_KV_HEADS):
            lo = h * CMP_R * CMP_HID
            p0 = proj[:, lo:lo + CMP_HID]
            p1 = pltpu.roll(proj[:, lo + CMP_HID:lo + 2 * CMP_HID], nb - 1, 0)
            pe_add = pe_proj[0:1, lo:lo + CMP_HID] + pe_proj[1:2, lo + CMP_HID:lo + 2 * CMP_HID]
            hid = _silu(p0 + p1 + pe_add + b1_ref[c:c + 1, :])
            outs.append(_dot(hid.astype(BF16), w2_ref[c]))
    return jnp.concatenate(outs, axis=1)


def _cmp_prompt_kernel(xk_ref, xv_ref, wbd_ref, pe_ref, b1_ref, w2_ref, out_ref, out_t_ref, lhs_ref):
    nb = out_ref.shape[1]
    load_rows = lambda c, s: (xk_ref, xv_ref)[c][0, pl.ds(s, nb, stride=CMP_STRIDE), :]
    tokens = _cmp_tokens(load_rows, nb, lhs_ref, wbd_ref, pe_ref, b1_ref, w2_ref)
    out_ref[0] = tokens
    out_t_ref[0] = tokens.T


def _cmp_prompt(kvc, wbd, pe_rows, b1, w2):
    b, l, _ = kvc.shape
    nb = l // CMP_STRIDE
    return pl.pallas_call(
        _cmp_prompt_kernel,
        grid=(b,),
        in_specs=[pl.BlockSpec((1, l, HD2), lambda i: (i, 0, 0)), pl.BlockSpec((1, l, HD2), lambda i: (i, 0, 1)),
                  _const_spec(wbd.shape), _const_spec(pe_rows.shape), _const_spec(b1.shape), _const_spec(w2.shape)],
        out_specs=[pl.BlockSpec((1, nb, KV_DIM), lambda i: (i, 0, 0)), pl.BlockSpec((1, KV_DIM, nb), lambda i: (i, 0, 0))],
        out_shape=[jax.ShapeDtypeStruct((b, nb, KV_DIM), F32), jax.ShapeDtypeStruct((b, KV_DIM, nb), F32)],
        scratch_shapes=[pltpu.VMEM((nb + PE_ROWS, CMP_K), BF16)],
        compiler_params=pltpu.CompilerParams(dimension_semantics=("arbitrary",), vmem_limit_bytes=VMEM_LIMIT),
        name="nsa_cmp_prompt",
    )(kvc, kvc, wbd, pe_rows, b1, w2)


COLS = GQA * Q_BLOCK
TINY = float(np.finfo(np.float32).tiny)
LOG2E = float(np.log2(np.e))


def _softmax_rows(s):
    m = jnp.max(s, axis=-1, keepdims=True)
    m = jnp.where(m > 0.5 * NEG_BIG, m, 0.0)
    e = jnp.exp(s - m)
    return e / jnp.maximum(jnp.sum(e, axis=-1, keepdims=True), TINY)


def _select_blocks(imp_t, t_lane):
    jj = lax.broadcasted_iota(jnp.int32, imp_t.shape, 0)
    cb = t_lane >> SEL_SHIFT
    forced = (jj == 0) | (jj == cb) | (jj == cb - 1)
    causal = (jj << SEL_SHIFT) <= t_lane
    x = jnp.where(forced, IMP_FORCED, imp_t)
    x = jnp.where(causal, x, IMP_BLOCKED)
    sel, _ = _take_top(x, SEL_TOPK)
    return sel


def _take_top(x, k):
    nj = x.shape[0]
    jf = lax.broadcasted_iota(jnp.int32, x.shape, 0).astype(F32)
    sel = jnp.zeros_like(x)
    picks = []
    for _ in range(k):
        m = jnp.max(x, axis=0, keepdims=True)
        first = jnp.min(jnp.where(x == m, jf, float(nj)), axis=0, keepdims=True)
        hit = jf == first
        sel = jnp.where(hit, 1.0, sel)
        x = jnp.where(hit, IMP_TAKEN, x)
        picks.append(first)
    return sel, picks


def _softmax_cols(s):
    m = jnp.max(s, axis=0, keepdims=True)
    m = jnp.where(m > 0.5 * NEG_BIG, m, 0.0)
    e = jnp.exp2(s - m)
    return e, 1.0 / jnp.maximum(jnp.sum(e, axis=0, keepdims=True), TINY)


def _dot_tn(a, b):
    return lax.dot_general(a, b, (((0,), (0,)), ((), ())), preferred_element_type=F32)


def _nsa_prompt_kernel(q_ref, small_ref, kcv_ref, kcvt_ref, kaug_ref, vaugt_ref, kwin_ref, mselt_ref, out_ref,
                       lt_ref, acc_ref, ot_ref, sa_ref, sb_ref):
    n = pl.program_id(1)
    t0 = n * Q_BLOCK
    nb = kcv_ref.shape[1]
    col_t = t0 + (lax.broadcasted_iota(jnp.int32, (1, COLS), 1) & (Q_BLOCK - 1))
    lane_t = t0 + lax.broadcasted_iota(jnp.int32, (1, Q_BLOCK), 1)
    gates_t = jax.nn.sigmoid(small_ref[0]).T
    q_t = (q_ref[0] * (HEAD_DIM ** -0.5 * LOG2E)).T.astype(BF16)
    c_end = lax.broadcasted_iota(jnp.int32, (nb, 1), 0) * CMP_STRIDE + (CMP_LEN - 1)
    o_cmp_t = []

    for h in range(N_KV_HEADS):
        for g in range(GQA):
            lo = (h * GQA + g) * HEAD_DIM
            lt_ref[h, SEL_LANES:SEL_LANES + HEAD_DIM, g * Q_BLOCK:(g + 1) * Q_BLOCK] = q_t[lo:lo + HEAD_DIM, :]
        lt_ref[h, SEL_LANES + HEAD_DIM:, :] = jnp.zeros((HEAD_DIM, COLS), BF16)
        qh_t = lt_ref[h, SEL_LANES:SEL_LANES + HEAD_DIM, :]

        kc = kcv_ref[0, :, h * HEAD_DIM:(h + 1) * HEAD_DIM].astype(BF16)
        vc_t = kcvt_ref[0, (N_KV_HEADS + h) * HEAD_DIM:(N_KV_HEADS + h + 1) * HEAD_DIM, :].astype(BF16)
        e_t, r_t = _softmax_cols(jnp.where(c_end <= col_t, _dot(kc, qh_t), NEG_BIG))
        o_cmp_t.append(_dot(vc_t, e_t.astype(BF16)) * r_t)

        psum = e_t[:, 0:Q_BLOCK] * r_t[:, 0:Q_BLOCK]
        for g in range(1, GQA):
            psum = psum + e_t[:, g * Q_BLOCK:(g + 1) * Q_BLOCK] * r_t[:, g * Q_BLOCK:(g + 1) * Q_BLOCK]
        p_hi = psum.astype(BF16)
        p_lo = (psum - p_hi.astype(F32)).astype(BF16)
        imp_t = _dot(mselt_ref[...], p_hi) + _dot(mselt_ref[...], p_lo)
        neg = jnp.where(_select_blocks(imp_t, lane_t) > 0.0, 0.0, NEG_BIG).astype(BF16)
        for g in range(GQA):
            lt_ref[h, 0:SEL_LANES, g * Q_BLOCK:(g + 1) * Q_BLOCK] = neg

    acc_ref[...] = jnp.zeros(acc_ref.shape, F32)
    n_pairs = (t0 + Q_BLOCK + 2 * SEL_TILE - 1) // (2 * SEL_TILE)

    def scores(s_ref, kt):
        k0 = pl.multiple_of(kt * SEL_TILE, SEL_TILE)
        for h in range(N_KV_HEADS):
            s_ref[h] = _dot(kaug_ref[0, h, pl.ds(k0, SEL_TILE), :], lt_ref[h])

    def consume(s_ref, kt, ms, causal_mask):
        out = []
        for h in range(N_KV_HEADS):
            s = s_ref[h]
            if causal_mask:
                kpos = kt * SEL_TILE + lax.broadcasted_iota(jnp.int32, (SEL_TILE, 1), 0)
                s = jnp.where(kpos <= col_t, s, NEG_BIG)
            m_new = jnp.maximum(ms[h], jnp.max(s, axis=0, keepdims=True))
            pe = jnp.exp2(s - m_new).astype(BF16)
            acc_ref[h] = jnp.exp2(ms[h] - m_new) * acc_ref[h] + _dot(vaugt_ref[0, h, kt], pe)
            out.append(m_new)
        return tuple(out)

    def pair(i, ms):
        scores(sb_ref, 2 * i + 1)
        ms = consume(sa_ref, 2 * i, ms, False)
        scores(sa_ref, 2 * i + 2)
        return consume(sb_ref, 2 * i + 1, ms, False)

    scores(sa_ref, 0)
    m_init = (jnp.full((1, COLS), NEG_BIG, F32),) * N_KV_HEADS
    ms = lax.fori_loop(0, n_pairs - 1, pair, m_init)
    last = 2 * (n_pairs - 1)
    two_tiles = t0 + Q_BLOCK > (last + 1) * SEL_TILE

    @pl.when(two_tiles)
    def _():
        scores(sb_ref, last + 1)
        consume(sb_ref, last + 1, consume(sa_ref, last, ms, True), True)

    @pl.when(jnp.logical_not(two_tiles))
    def _():
        consume(sa_ref, last, ms, True)

    w0 = pl.multiple_of(jnp.maximum(t0 - WINDOW, 0), Q_BLOCK)
    kpos = w0 + lax.broadcasted_iota(jnp.int32, (WINDOW + Q_BLOCK, 1), 0)
    dpos = col_t - kpos
    win_ok = (dpos >= 0) & (dpos < WINDOW)
    for h in range(N_KV_HEADS):
        acc = acc_ref[h]
        o_sel_t = acc[:HEAD_DIM] / jnp.maximum(acc[HEAD_DIM:HEAD_DIM + 1], TINY)

        kw = kwin_ref[0, pl.ds(w0, WINDOW + Q_BLOCK), h * HEAD_DIM:(h + 1) * HEAD_DIM]
        vw = kwin_ref[0, pl.ds(w0, WINDOW + Q_BLOCK), (N_KV_HEADS + h) * HEAD_DIM:(N_KV_HEADS + h + 1) * HEAD_DIM]
        qh_t = lt_ref[h, SEL_LANES:SEL_LANES + HEAD_DIM, :]
        e_t, r_t = _softmax_cols(jnp.where(win_ok, _dot(kw, qh_t), NEG_BIG))
        o_win_t = _dot_tn(vw, e_t.astype(BF16)) * r_t

        for g in range(GQA):
            hd = h * GQA + g
            cols = slice(g * Q_BLOCK, (g + 1) * Q_BLOCK)
            ot_ref[hd * HEAD_DIM:(hd + 1) * HEAD_DIM, :] = (
                gates_t[3 * hd:3 * hd + 1, :] * o_cmp_t[h][:, cols] + gates_t[3 * hd + 1:3 * hd + 2, :] * o_sel_t[:, cols]
                + gates_t[3 * hd + 2:3 * hd + 3, :] * o_win_t[:, cols])
    out_ref[0] = ot_ref[...].T


def _cmp_to_sel_t(nb, n_cmp, n_blk):
    cs = np.arange(nb)[None, :] * CMP_STRIDE
    bs = np.arange(SEL_LANES)[:, None] * SEL_BLOCK
    ov = np.clip(np.minimum(cs + CMP_LEN, bs + SEL_BLOCK) - np.maximum(cs, bs), 0, None) / CMP_LEN
    ov = ov * (np.arange(nb)[None, :] < n_cmp) * (np.arange(SEL_LANES)[:, None] < n_blk)
    return jnp.asarray(ov, BF16)


def _nsa_prompt(q, small, kcv, kcv_t, kaug, vaug_t, kwin):
    b, l, _ = q.shape
    nb = kcv.shape[1]
    n_blk = l // SEL_BLOCK
    assert l % (2 * SEL_TILE) == 0 and n_blk <= SEL_LANES and l >= WINDOW + Q_BLOCK
    mselt = _cmp_to_sel_t(nb, (l - CMP_LEN) // CMP_STRIDE + 1, n_blk)

    def per_batch(shape):
        nd = len(shape)
        return pl.BlockSpec((1,) + shape[1:], lambda i, j: (i,) + (0,) * (nd - 1), pipeline_mode=pl.Buffered(1))

    return pl.pallas_call(
        _nsa_prompt_kernel,
        grid=(b, l // Q_BLOCK),
        in_specs=[pl.BlockSpec((1, Q_BLOCK, ATT_DIM), lambda i, j: (i, j, 0)),
                  pl.BlockSpec((1, Q_BLOCK, SMALL_W), lambda i, j: (i, j, 0)),
                  per_batch(kcv.shape), per_batch(kcv_t.shape), per_batch(kaug.shape), per_batch(vaug_t.shape),
                  per_batch(kwin.shape), pl.BlockSpec(mselt.shape, lambda i, j: (0, 0), pipeline_mode=pl.Buffered(1))],
        out_specs=pl.BlockSpec((1, Q_BLOCK, ATT_DIM), lambda i, j: (i, j, 0)),
        out_shape=jax.ShapeDtypeStruct((b, l, ATT_DIM), F32),
        scratch_shapes=[pltpu.VMEM((N_KV_HEADS, AUG, COLS), BF16), pltpu.VMEM((N_KV_HEADS, VAUG_ROWS, COLS), F32),
                        pltpu.VMEM((ATT_DIM, Q_BLOCK), F32), pltpu.VMEM((N_KV_HEADS, SEL_TILE, COLS), F32),
                        pltpu.VMEM((N_KV_HEADS, SEL_TILE, COLS), F32)],
        compiler_params=pltpu.CompilerParams(dimension_semantics=("arbitrary", "arbitrary"),
                                             vmem_limit_bytes=VMEM_LIMIT),
        name="nsa_prompt",
    )(q, small, kcv, kcv_t, kaug, vaug_t, kwin, mselt)


DT_LANE = N_GATE
HEADS_PER_GROUP = SSM_HEADS // N_GROUPS
GROUP_W = D_INNER // N_GROUPS
TAIL = 8


def _expand_heads(v):
    rows = v.shape[0]
    lane = lax.broadcasted_iota(jnp.int32, (rows, LANES), 1)
    tiles = []
    for j in range(D_INNER // LANES):
        a = jnp.broadcast_to(v[:, DT_LANE + 2 * j:DT_LANE + 2 * j + 1], (rows, LANES))
        b = jnp.broadcast_to(v[:, DT_LANE + 2 * j + 1:DT_LANE + 2 * j + 2], (rows, LANES))
        tiles.append(jnp.where(lane < SSM_HEAD_DIM, a, b))
    return jnp.concatenate(tiles, axis=1)


def _cumsum_rows(x):
    n = x.shape[0]
    row = lax.broadcasted_iota(jnp.int32, x.shape, 0)
    s = 1
    while s < n:
        x = x + jnp.where(row >= s, pltpu.roll(x, s, 0), 0.0)
        s *= 2
    return x


def _grouped_norm_gate(y, z, norm):
    v = y * _silu(z)
    outs = []
    for g in range(N_GROUPS):
        vg = v[:, g * GROUP_W:(g + 1) * GROUP_W]
        outs.append(vg * lax.rsqrt(jnp.mean(vg * vg, axis=-1, keepdims=True) + EPS))
    return jnp.concatenate(outs, axis=1) * norm


def _ssd_prompt_kernel(xbc_ref, z_ref, small_ref, convw_ref, convb_ref, dtb_ref, alog_ref, dskip_ref, norm_ref,
                       y_ref, state_ref, xe_ref, st_ref, yd_ref):
    c = pl.program_id(1)
    lc = xbc_ref.shape[1]

    @pl.when(c == 0)
    def _():
        xe_ref[0:TAIL, :] = jnp.zeros((TAIL, CONV_DIM), F32)
        st_ref[...] = jnp.zeros(st_ref.shape, F32)

    xe_ref[TAIL:TAIL + lc, :] = xbc_ref[0]
    conv = convb_ref[...] + xe_ref[TAIL:TAIL + lc, :] * convw_ref[CONV_W - 1:CONV_W, :]
    for k in range(CONV_W - 1):
        conv = conv + xe_ref[pl.ds(TAIL - (CONV_W - 1) + k, lc), :] * convw_ref[k:k + 1, :]
    xe_ref[0:TAIL, :] = xe_ref[lc:lc + TAIL, :]
    xc = _silu(conv)
    xs = xc[:, :D_INNER]

    dt = jax.nn.softplus(small_ref[0] + dtb_ref[...])
    ad = dt * (-jnp.exp(alog_ref[...]))
    acs = _cumsum_rows(ad)
    acs_t = acs.T
    dt_e = _expand_heads(dt)
    acs_e = _expand_heads(acs)
    last_e = acs_e[lc - 1:lc, :]
    xd = xs * dt_e
    xd_bf = xd.astype(BF16)
    xdd_bf = (xd * jnp.exp(last_e - acs_e)).astype(BF16)
    grow = jnp.exp(acs_e)
    li = lax.broadcasted_iota(jnp.int32, (lc, lc), 0)
    si = lax.broadcasted_iota(jnp.int32, (lc, lc), 1)

    y_off = []
    for g in range(N_GROUPS):
        bm = xc[:, D_INNER + g * D_STATE:D_INNER + (g + 1) * D_STATE]
        cm = xc[:, D_INNER + N_GROUPS * D_STATE + g * D_STATE:D_INNER + N_GROUPS * D_STATE + (g + 1) * D_STATE]
        bm_bf = bm.astype(BF16)
        cm_bf = cm.astype(BF16)
        cb = _dot_nt(cm_bf, bm_bf)
        cols = slice(g * GROUP_W, (g + 1) * GROUP_W)
        st_g = st_ref[:, cols]
        y_off.append(_dot(cm_bf, st_g.astype(BF16)))
        for hh in range(HEADS_PER_GROUP):
            h = g * HEADS_PER_GROUP + hh
            seg = acs[:, DT_LANE + h:DT_LANE + h + 1] - acs_t[DT_LANE + h:DT_LANE + h + 1, :]
            m = jnp.where(li >= si, cb * jnp.exp(seg), 0.0).astype(BF16)
            yd_ref[:, h * SSM_HEAD_DIM:(h + 1) * SSM_HEAD_DIM] = _dot(m, xd_bf[:, h * SSM_HEAD_DIM:(h + 1) * SSM_HEAD_DIM])
        st_ref[:, cols] = st_g * jnp.exp(last_e[:, cols]) + _dot(bm.T.astype(BF16), xdd_bf[:, cols])

    y = yd_ref[...] + jnp.concatenate(y_off, axis=1) * grow + dskip_ref[...] * xs
    y_ref[0] = _grouped_norm_gate(y, z_ref[0], norm_ref[...])

    @pl.when(c == pl.num_programs(1) - 1)
    def _():
        state_ref[0] = st_ref[...].T


def _pad_small(v):
    return jnp.zeros((1, SMALL_W), F32).at[0, DT_LANE:DT_LANE + SSM_HEADS].set(v.astype(F32))


def _ssd_prompt(xbc, z, small, conv_w, conv_b, dt_bias, a_log, d_skip, ssm_norm):
    b, l, _ = xbc.shape
    lc = min(SSD_CHUNK, l)
    params = (conv_w, conv_b.reshape(1, CONV_DIM), _pad_small(dt_bias), _pad_small(a_log),
              jnp.repeat(d_skip.astype(F32), SSM_HEAD_DIM).reshape(1, D_INNER), ssm_norm.reshape(1, D_INNER))
    tile = lambda w: pl.BlockSpec((1, lc, w), lambda i, j: (i, j, 0))
    return pl.pallas_call(
        _ssd_prompt_kernel,
        grid=(b, l // lc),
        in_specs=[tile(CONV_DIM), tile(D_INNER), tile(SMALL_W)] + [_const_spec(p.shape) for p in params],
        out_specs=[tile(D_INNER), pl.BlockSpec((1, D_INNER, D_STATE), lambda i, j: (i, 0, 0))],
        out_shape=[jax.ShapeDtypeStruct((b, l, D_INNER), F32), jax.ShapeDtypeStruct((b, D_INNER, D_STATE), F32)],
        scratch_shapes=[pltpu.VMEM((lc + TAIL, CONV_DIM), F32), pltpu.VMEM((D_STATE, D_INNER), F32),
                        pltpu.VMEM((lc, D_INNER), F32)],
        compiler_params=pltpu.CompilerParams(dimension_semantics=("arbitrary", "arbitrary"),
                                             vmem_limit_bytes=VMEM_LIMIT),
        name="ssd_prompt",
    )(xbc, z, small, *params)


def _out_ffn_kernel(h_ref, att_ref, ssm_ref, p_ref, woa_ref, wos_ref, g2_ref, wg_ref, wu_ref, wd_ref,
                    gp_ref, wpg_ref, wple_ref, gf_ref, y_ref, acc_ref):
    h = h_ref[...] + _dot(att_ref[...].astype(BF16), woa_ref[...]) + _dot(ssm_ref[...].astype(BF16), wos_ref[...])
    h = _swiglu_half_step(h, g2_ref, wg_ref, wu_ref, wd_ref, acc_ref)
    gate = jax.nn.sigmoid(_dot(_rms(h, gp_ref[...]).astype(BF16), wpg_ref[...]))
    h = h + gate * _dot(p_ref[...].astype(BF16), wple_ref[...])
    y_ref[...] = _rms(h, gf_ref[...])


def _out_ffn(h, att, ssm, p, weights, tm):
    n = h.shape[0]
    return pl.pallas_call(
        _out_ffn_kernel,
        grid=(n // tm,),
        in_specs=[_row_spec(tm, D_MODEL), _row_spec(tm, ATT_DIM), _row_spec(tm, D_INNER), _row_spec(tm, PLE_DIM)]
        + [_const_spec(w.shape) for w in weights],
        out_specs=_row_spec(tm, D_MODEL),
        out_shape=jax.ShapeDtypeStruct((n, D_MODEL), F32),
        scratch_shapes=[pltpu.VMEM((tm, D_MODEL), F32)],
        compiler_params=pltpu.CompilerParams(dimension_semantics=("arbitrary",), vmem_limit_bytes=VMEM_LIMIT),
        name="outproj_ffn2_ple",
    )(h, att, ssm, p, *weights)


BLOCKS_PER_PAGE = PAGE_SIZE // CMP_STRIDE
HEAD_ROWS = 8


def _head_rows_of(h):
    row = lax.broadcasted_iota(jnp.int32, (HEAD_ROWS, 1), 0)
    return (row >= h * GQA) & (row < (h + 1) * GQA)


def _cmp_sample_kernel(pt_ref, q_ref, cache_ref, wbd_ref, pe_ref, b1_ref, w2_ref, mselt_ref,
                       ocmp_ref, imp_ref, xbuf_ref, xrow_ref, lhs_ref, sem_ref, *, n_cmp):
    b = pl.program_id(0)
    nb_total = pl.num_programs(0)
    n_pages = pt_ref.shape[1]
    nbp = n_pages * BLOCKS_PER_PAGE

    def page_copy(bb, slot, i):
        return pltpu.make_async_copy(cache_ref.at[pt_ref[bb, i], 0], xbuf_ref.at[slot, i], sem_ref.at[slot])

    def start_all(bb, slot):
        def body(i2, carry):
            for prio in range(2):
                page_copy(bb, slot, 2 * i2 + prio).start(priority=prio)
            return carry
        lax.fori_loop(0, n_pages // 2, body, 0)

    def wait_all(bb, slot):
        def body(i, carry):
            page_copy(bb, slot, i).wait()
            return carry
        lax.fori_loop(0, n_pages, body, 0)

    slot = b % 2

    @pl.when(b == 0)
    def _():
        start_all(0, 0)

    @pl.when(b + 1 < nb_total)
    def _():
        start_all(b + 1, 1 - slot)

    wait_all(b, slot)

    def load_rows(c, s):
        if s == 0:
            for i in range(n_pages):
                xrow_ref[c, i * PAGE_SIZE:(i + 1) * PAGE_SIZE, :] = xbuf_ref[slot, i, c].reshape(HD2, PAGE_SIZE).T
        return xrow_ref[c, pl.ds(s, nbp, stride=CMP_STRIDE), :]

    kcv = _cmp_tokens(load_rows, nbp, lhs_ref, wbd_ref, pe_ref, b1_ref, w2_ref)
    q8 = (q_ref[0] * (HEAD_DIM ** -0.5)).astype(BF16)
    valid = lax.broadcasted_iota(jnp.int32, (1, nbp), 1) < n_cmp
    o_cmp = jnp.zeros((HEAD_ROWS, HEAD_DIM), F32)
    psum = jnp.zeros((HEAD_ROWS, nbp), F32)
    row = lax.broadcasted_iota(jnp.int32, (HEAD_ROWS, 1), 0)
    for h in range(N_KV_HEADS):
        kc = kcv[:, h * HEAD_DIM:(h + 1) * HEAD_DIM].astype(BF16)
        vc = kcv[:, (N_KV_HEADS + h) * HEAD_DIM:(N_KV_HEADS + h + 1) * HEAD_DIM].astype(BF16)
        p = _softmax_rows(jnp.where(valid, _dot_nt(q8, kc), NEG_BIG))
        mine = _head_rows_of(h)
        o_cmp = jnp.where(mine, _dot(p.astype(BF16), vc), o_cmp)
        ph = jnp.sum(jnp.where(mine, p, 0.0), axis=0, keepdims=True)
        psum = jnp.where(row == h, ph, psum)
    ocmp_ref[0] = o_cmp
    p_hi = psum.astype(BF16)
    p_lo = (psum - p_hi.astype(F32)).astype(BF16)
    imp_ref[0] = _dot_nt(p_hi, mselt_ref[...]) + _dot_nt(p_lo, mselt_ref[...])


def _cmp_sample(page_table, q8, cache_t, wbd, pe_rows, b1, w2, mselt, n_cmp):
    db, n_pages = page_table.shape
    assert n_pages % 2 == 0
    consts = (wbd, pe_rows, b1, w2, mselt)
    grid_spec = pltpu.PrefetchScalarGridSpec(
        num_scalar_prefetch=1,
        grid=(db,),
        in_specs=[pl.BlockSpec((1, HEAD_ROWS, HEAD_DIM), lambda i, pt: (i, 0, 0)),
                  pl.BlockSpec(memory_space=pl.ANY)]
        + [pl.BlockSpec(c.shape, lambda i, pt, nd=c.ndim: (0,) * nd, pipeline_mode=pl.Buffered(1)) for c in consts],
        out_specs=[pl.BlockSpec((1, HEAD_ROWS, HEAD_DIM), lambda i, pt: (i, 0, 0)),
                   pl.BlockSpec((1, HEAD_ROWS, SEL_LANES), lambda i, pt: (i, 0, 0))],
        scratch_shapes=[pltpu.VMEM((2, n_pages, 2, N_KV_HEADS, HEAD_DIM, PAGE_SIZE), F32),
                        pltpu.VMEM((2, n_pages * PAGE_SIZE, HD2), F32),
                        pltpu.VMEM((n_pages * BLOCKS_PER_PAGE + PE_ROWS, CMP_K), BF16), pltpu.SemaphoreType.DMA((2,))],
    )
    return pl.pallas_call(
        functools.partial(_cmp_sample_kernel, n_cmp=n_cmp),
        grid_spec=grid_spec,
        out_shape=[jax.ShapeDtypeStruct((db, HEAD_ROWS, HEAD_DIM), F32),
                   jax.ShapeDtypeStruct((db, HEAD_ROWS, SEL_LANES), F32)],
        compiler_params=pltpu.CompilerParams(dimension_semantics=("arbitrary",), vmem_limit_bytes=VMEM_LIMIT),
        name="nsa_cmp_sample",
    )(page_table, q8, cache_t, *consts)


N_PAST_PICKS = SEL_TOPK - 1


def _topk_sample_kernel(imp_ref, idx_ref, *, past_blk):
    x = imp_ref[...].T
    jj = lax.broadcasted_iota(jnp.int32, x.shape, 0)
    x = jnp.where((jj == 0) | (jj == past_blk - 1), IMP_FORCED, x)
    x = jnp.where(jj < past_blk, x, IMP_BLOCKED)
    _, picks = _take_top(x, N_PAST_PICKS)
    picks = picks + [jnp.zeros_like(picks[0])] * (idx_ref.shape[0] - N_PAST_PICKS)
    idx_ref[...] = jnp.concatenate(picks, axis=0).astype(jnp.int32)


def _topk_sample(imp2, past_blk):
    nq = imp2.shape[0]
    assert N_PAST_PICKS <= past_blk <= SEL_LANES
    return pl.pallas_call(
        functools.partial(_topk_sample_kernel, past_blk=past_blk),
        out_shape=jax.ShapeDtypeStruct((SEL_TOPK, nq), jnp.int32),
        name="nsa_topk_sample",
    )(imp2)


SUB_PER_PAGE = PAGE_SIZE // SEL_BLOCK
SUB_SHIFT = SUB_PER_PAGE.bit_length() - 1
SEL_SHIFT_IN_PAGE = SEL_BLOCK.bit_length() - 1
N_SEL_COLS = N_PAST_PICKS * PAGE_SIZE


def _attend_one_token(q8, q8f, kt, vt, valid, k_new, v_new):
    s = jnp.where(valid, _dot(q8, kt.astype(BF16)), NEG_BIG)
    s_new = jnp.sum(q8f * k_new, axis=-1, keepdims=True)
    m = jnp.maximum(jnp.max(s, axis=-1, keepdims=True), s_new)
    e = jnp.exp(s - m)
    e_new = jnp.exp(s_new - m)
    norm = jnp.maximum(jnp.sum(e, axis=-1, keepdims=True) + e_new, TINY)
    return (_dot_nt(e.astype(BF16), vt.astype(BF16)) + e_new * v_new) / norm


def _sel_win_sample_kernel(idx_ref, pt_ref, q_ref, gate_ref, ocmp_ref, kvs_ref, kvw_ref, win_ref, slc_ref,
                           att_ref, newwin_ref, gbuf_ref, sem_ref):
    b = pl.program_id(0)
    nb_total = pl.num_programs(0)
    w_buf = win_ref.shape[-1]

    def block_copies(bb, slot, h, k):
        j = idx_ref[k, bb * N_KV_HEADS + h]
        page = pt_ref[bb, j >> SUB_SHIFT]
        return [pltpu.make_async_copy(slc_ref.at[page, 0, c, h],
                                      gbuf_ref.at[slot, h, c, :, pl.ds(k * PAGE_SIZE, PAGE_SIZE)],
                                      sem_ref.at[slot]) for c in range(2)]

    def start_all(bb, slot):
        for h in range(N_KV_HEADS):
            for k in range(N_PAST_PICKS):
                for cp in block_copies(bb, slot, h, k):
                    cp.start()

    def wait_all(bb, slot):
        for h in range(N_KV_HEADS):
            for k in range(N_PAST_PICKS):
                for cp in block_copies(bb, slot, h, k):
                    cp.wait()

    slot = b % 2

    @pl.when(b == 0)
    def _():
        start_all(0, 0)

    @pl.when(b + 1 < nb_total)
    def _():
        start_all(b + 1, 1 - slot)

    wait_all(b, slot)

    q8f = q_ref[0] * (HEAD_DIM ** -0.5)
    q8 = q8f.astype(BF16)
    page_lane = lax.broadcasted_iota(jnp.int32, (1, PAGE_SIZE), 1)
    win_lane = lax.broadcasted_iota(jnp.int32, (1, w_buf), 1)
    win_valid = w_buf - win_lane < WINDOW
    o_sel = jnp.zeros((HEAD_ROWS, HEAD_DIM), F32)
    o_win = jnp.zeros((HEAD_ROWS, HEAD_DIM), F32)
    for h in range(N_KV_HEADS):
        kcols = slice(h * HEAD_DIM, (h + 1) * HEAD_DIM)
        vcols = slice((N_KV_HEADS + h) * HEAD_DIM, (N_KV_HEADS + h + 1) * HEAD_DIM)
        mine = _head_rows_of(h)
        sel_valid = jnp.concatenate(
            [(page_lane >> SEL_SHIFT_IN_PAGE) == (idx_ref[k, b * N_KV_HEADS + h] & (SUB_PER_PAGE - 1))
             for k in range(N_PAST_PICKS)], axis=1)
        o_sel = jnp.where(mine, _attend_one_token(q8, q8f, gbuf_ref[slot, h, 0], gbuf_ref[slot, h, 1], sel_valid,
                                                  kvs_ref[0][:, kcols], kvs_ref[0][:, vcols]), o_sel)
        o_win = jnp.where(mine, _attend_one_token(q8, q8f, win_ref[0, 0, 0, h], win_ref[0, 0, 1, h], win_valid,
                                                  kvw_ref[0][:, kcols], kvw_ref[0][:, vcols]), o_win)
    gates = jax.nn.sigmoid(gate_ref[0])
    att_ref[0] = gates[:, 0:1] * ocmp_ref[0] + gates[:, 1:2] * o_sel + gates[:, 2:3] * o_win

    new_col = jnp.concatenate([kvw_ref[0], jnp.zeros((LANES - 1, KV_DIM), F32)], axis=0).T
    tile_lane = lax.broadcasted_iota(jnp.int32, (HEAD_DIM, w_buf), 1)
    for c in range(2):
        for h in range(N_KV_HEADS):
            lo = (c * N_KV_HEADS + h) * HEAD_DIM
            newwin_ref[0, 0, c, h] = jnp.where(tile_lane == w_buf - 1, new_col[lo:lo + HEAD_DIM, 0:1],
                                               pltpu.roll(win_ref[0, 0, c, h], w_buf - 1, 1))


def _sel_win_sample(idx, page_table, q8, gate8, o_cmp, kvs_new, kvw_new, win_t, slc_t):
    db = page_table.shape[0]
    w_buf = win_t.shape[-1]
    assert w_buf == WINDOW and win_t.shape[1] == 1
    per_tok = lambda shape: pl.BlockSpec((1,) + shape[1:], lambda i, idx, pt, nd=len(shape): (i,) + (0,) * (nd - 1))
    grid_spec = pltpu.PrefetchScalarGridSpec(
        num_scalar_prefetch=2,
        grid=(db,),
        in_specs=[per_tok(q8.shape), per_tok(gate8.shape), per_tok(o_cmp.shape), per_tok(kvs_new.shape),
                  per_tok(kvw_new.shape), per_tok(win_t.shape), pl.BlockSpec(memory_space=pl.ANY)],
        out_specs=[per_tok(o_cmp.shape), per_tok(win_t.shape)],
        scratch_shapes=[pltpu.VMEM((2, N_KV_HEADS, 2, HEAD_DIM, N_SEL_COLS), F32), pltpu.SemaphoreType.DMA((2,))],
    )
    return pl.pallas_call(
        _sel_win_sample_kernel,
        grid_spec=grid_spec,
        out_shape=[jax.ShapeDtypeStruct(o_cmp.shape, F32), jax.ShapeDtypeStruct(win_t.shape, F32)],
        compiler_params=pltpu.CompilerParams(dimension_semantics=("arbitrary",), vmem_limit_bytes=VMEM_LIMIT),
        name="nsa_sel_win_sample",
    )(idx, page_table, q8, gate8, o_cmp, kvs_new, kvw_new, win_t, slc_t)


SSD_BT = 8


def _ssd_sample_kernel(xbc_ref, cst_ref, z_ref, small_ref, h0_ref, convw_ref, convb_ref, dtb_ref, alog_ref,
                       dskip_ref, norm_ref, y_ref, hnew_ref, ys_ref):
    conv = convb_ref[...] + xbc_ref[...] * convw_ref[CONV_W - 1:CONV_W, :]
    for k in range(CONV_W - 1):
        conv = conv + cst_ref[k] * convw_ref[k:k + 1, :]
    xc = _silu(conv)
    xs = xc[:, :D_INNER]
    dt = jax.nn.softplus(small_ref[...] + dtb_ref[...])
    dt_e = _expand_heads(dt)
    decay_e = jnp.exp(dt_e * _expand_heads(-jnp.exp(alog_ref[...])))
    xd = xs * dt_e
    fill = jnp.zeros((LANES - SSD_BT, D_INNER), F32)
    xd_t = jnp.concatenate([xd, fill], axis=0).T
    decay_t = jnp.concatenate([decay_e, fill], axis=0).T
    lane = lax.broadcasted_iota(jnp.int32, (1, D_INNER), 1)
    for i in range(SSD_BT):
        bsel = jnp.concatenate(
            [jnp.broadcast_to(xc[i:i + 1, D_INNER + g * D_STATE:D_INNER + (g + 1) * D_STATE], (GROUP_W, D_STATE))
             for g in range(N_GROUPS)], axis=0)
        h0 = h0_ref[i].reshape(D_INNER, D_STATE)
        hn = decay_t[:, i:i + 1] * h0 + xd_t[:, i:i + 1] * bsel
        hnew_ref[i] = hn.reshape(SSM_HEADS, SSM_HEAD_DIM, D_STATE)
        c8 = jnp.concatenate(
            [xc[i:i + 1, D_INNER + (N_GROUPS + g) * D_STATE:D_INNER + (N_GROUPS + g + 1) * D_STATE]
             for g in range(N_GROUPS)] + [jnp.zeros((HEAD_ROWS - N_GROUPS, D_STATE), F32)], axis=0)
        y8 = _dot_nt(c8.astype(BF16), hn.astype(BF16))
        ys_ref[i:i + 1, :] = jnp.where(lane < GROUP_W, y8[0:1], y8[1:2])
    y = ys_ref[...] + dskip_ref[...] * xs
    y_ref[...] = _grouped_norm_gate(y, z_ref[...], norm_ref[...])


def _ssd_sample(xbc, conv_state_t, z, small, h0, conv_w, conv_b, dt_bias, a_log, d_skip, ssm_norm):
    db = xbc.shape[0]
    assert db % SSD_BT == 0 and N_GROUPS == 2
    params = (conv_w, conv_b.reshape(1, CONV_DIM), _pad_small(dt_bias), _pad_small(a_log),
              jnp.repeat(d_skip.astype(F32), SSM_HEAD_DIM).reshape(1, D_INNER), ssm_norm.reshape(1, D_INNER))
    rows = lambda w: pl.BlockSpec((SSD_BT, w), lambda i: (i, 0))
    state_spec = pl.BlockSpec((SSD_BT, SSM_HEADS, SSM_HEAD_DIM, D_STATE), lambda i: (i, 0, 0, 0))
    return pl.pallas_call(
        _ssd_sample_kernel,
        grid=(db // SSD_BT,),
        in_specs=[rows(CONV_DIM), pl.BlockSpec((CONV_W - 1, SSD_BT, CONV_DIM), lambda i: (0, i, 0)), rows(D_INNER),
                  rows(SMALL_W), state_spec] + [_const_spec(p.shape) for p in params],
        out_specs=[rows(D_INNER), state_spec],
        out_shape=[jax.ShapeDtypeStruct((db, D_INNER), F32), jax.ShapeDtypeStruct(h0.shape, F32)],
        scratch_shapes=[pltpu.VMEM((SSD_BT, D_INNER), F32)],
        compiler_params=pltpu.CompilerParams(dimension_semantics=("arbitrary",), vmem_limit_bytes=VMEM_LIMIT),
        name="ssd_sample",
    )(xbc, conv_state_t, z, small, h0, *params)


PROMPT_TM = 512


def _ffn_weights(w_gate, w_up, w_down):
    chunked = lambda w: jnp.transpose(w.astype(BF16).reshape(D_MODEL, N_FF_CHUNKS, FF_CHUNK), (1, 0, 2))
    return chunked(w_gate), chunked(w_up), w_down.astype(BF16).reshape(N_FF_CHUNKS, FF_CHUNK, D_MODEL)


def _in_proj_weight(w_in):
    cuts = np.cumsum([ATT_DIM, KV_DIM, KV_DIM, KV_DIM, N_GATE, D_INNER, CONV_DIM]).tolist()
    q, kvc, kvs, kvw, g, z, xbc, dt = jnp.split(w_in, cuts, axis=-1)
    pad = jnp.zeros((D_MODEL, SMALL_W - N_GATE - SSM_HEADS), w_in.dtype)
    return jnp.concatenate([q, kvc, kvs, kvw, z, xbc, g, dt, pad], axis=-1).astype(BF16)


def kernel(x_prompt, x_sample, cache_cmp_kv, cache_slc_kv, cache_win_kv, state_conv, state_ssm, page_table,
           p_prompt, p_sample, ffn1_norm, ffn1_w_gate, ffn1_w_up, ffn1_w_down, mix_norm, w_in,
           cmp_w1, cmp_pe, cmp_b1, cmp_w2, conv_w, conv_b, dt_bias, a_log, d_skip, ssm_norm, w_out,
           ffn2_norm, ffn2_w_gate, ffn2_w_up, ffn2_w_down, ple_norm, w_ple_gate, w_ple, final_norm):
    b, l, _ = x_prompt.shape
    db, t_new, _ = x_sample.shape
    depth = ffn1_norm.shape[0]
    n_pool = cache_cmp_kv.shape[0]
    n_pages = page_table.shape[1]
    past = n_pages * PAGE_SIZE
    assert depth == 1 and t_new == 1 and past >= WINDOW and (b * l) % PROMPT_TM == 0
    i = 0
    row = lambda v: v.reshape(1, -1).astype(F32)

    ffn1 = _ffn_weights(ffn1_w_gate[i], ffn1_w_up[i], ffn1_w_down[i])
    stage1_w = (row(ffn1_norm[i]),) + ffn1 + (row(mix_norm[i]), _in_proj_weight(w_in[i]))
    cmp_w = _cmp_weights(cmp_w1[i], cmp_pe[i], cmp_b1[i], cmp_w2[i])
    ssm_w = (conv_w[i], conv_b[i], dt_bias[i], a_log[i], d_skip[i], ssm_norm[i])
    w_o = w_out[i].astype(BF16)
    stage3_w = (w_o[:ATT_DIM], w_o[ATT_DIM:], row(ffn2_norm[i])) + _ffn_weights(ffn2_w_gate[i], ffn2_w_up[i], ffn2_w_down[i]) + (
        row(ple_norm[i]), w_ple_gate[i].astype(BF16), w_ple[i].astype(BF16), row(final_norm))
    kv6 = lambda a, lead, rows: a.reshape(lead, 1, rows, 2, N_KV_HEADS, HEAD_DIM)

    h, q, kvc, kvs, kvw, z, xbc, small, kvc_t, kvs_t, kaug, vaug_t, kwin = _ffn_inproj(
        x_prompt.reshape(b * l, D_MODEL), *stage1_w, PROMPT_TM, cache_batch=b)
    per_b = lambda a: a.reshape(b, l, a.shape[-1])
    kcv, kcv_t = _cmp_prompt(per_b(kvc), *cmp_w)
    att = _nsa_prompt(per_b(q), per_b(small), kcv, kcv_t, kaug, vaug_t, per_b(kwin))
    ssm, ssm_state = _ssd_prompt(per_b(xbc), per_b(z), per_b(small), *ssm_w)
    y_prompt = _out_ffn(h, att.reshape(b * l, ATT_DIM), ssm.reshape(b * l, D_INNER),
                        p_prompt[i].reshape(b * l, PLE_DIM), stage3_w, PROMPT_TM).reshape(b, l, D_MODEL)
    keep = min(WINDOW, l)
    rows_major = lambda a: jnp.transpose(a, (0, 1, 5, 2, 3, 4))
    new_cmp_p = rows_major(kvc_t)
    new_slc_p = rows_major(kvs_t)
    new_win_p = kv6(per_b(kvw)[:, l - keep:], b, keep)
    new_conv_p = per_b(xbc)[:, l - (CONV_W - 1):].reshape(b, 1, CONV_W - 1, CONV_DIM)
    new_ssm_p = ssm_state.reshape(b, 1, SSM_HEADS, SSM_HEAD_DIM, D_STATE)

    hs, qs, kvc_s, kvs_s, kvw_s, z_s, xbc_s, small_s = _ffn_inproj(x_sample.reshape(db, D_MODEL), *stage1_w, db)
    q8 = qs.reshape(db, N_HEADS, HEAD_DIM)
    n_cmp_s = (past + t_new - CMP_LEN) // CMP_STRIDE + 1
    past_blk = past // SEL_BLOCK
    mselt_s = _cmp_to_sel_t(n_pages * BLOCKS_PER_PAGE, n_cmp_s, past_blk)
    row_minor = lambda a: jnp.transpose(a, (0, 1, 3, 4, 5, 2))
    o_cmp, imp = _cmp_sample(page_table, q8, row_minor(cache_cmp_kv), *cmp_w, mselt_s, n_cmp_s)
    n_query = db * N_KV_HEADS
    imp2 = jnp.pad(imp[:, :N_KV_HEADS].reshape(n_query, SEL_LANES), ((0, -n_query % LANES), (0, 0)))
    idx = _topk_sample(imp2, past_blk)
    att_s, new_win_t = _sel_win_sample(
        idx, page_table, q8, small_s[:, :N_GATE].reshape(db, N_HEADS, 3), o_cmp,
        kvs_s.reshape(db, 1, KV_DIM), kvw_s.reshape(db, 1, KV_DIM),
        row_minor(cache_win_kv), row_minor(cache_slc_kv))
    new_win_s = rows_major(new_win_t)
    ssm_s, ssm_state_s = _ssd_sample(xbc_s, jnp.transpose(state_conv[:, i], (1, 0, 2)), z_s, small_s,
                                     state_ssm[:, i], *ssm_w)
    y_sample = _out_ffn(hs, att_s.reshape(db, ATT_DIM), ssm_s, p_sample[i].reshape(db, PLE_DIM),
                        stage3_w, db).reshape(db, 1, D_MODEL)
    new_conv_s = jnp.concatenate([state_conv[:, i, 1:], xbc_s[:, None]], axis=1)[:, None]

    return (y_prompt, y_sample, new_cmp_p, new_slc_p, new_win_p, new_conv_p, new_ssm_p,
            kv6(kvc_s, db, 1), kv6(kvs_s, db, 1), new_win_s, new_conv_s, ssm_state_s[:, None])
```

```python
import functools

import jax
import jax.numpy as jnp
import numpy as np
from jax import lax
from jax.experimental import pallas as pl
from jax.experimental.pallas import tpu as pltpu

F32 = jnp.float32
BF16 = jnp.bfloat16

D_MODEL = 1024
N_HEADS = 8
N_KV_HEADS = 2
HEAD_DIM = 64
GQA = N_HEADS // N_KV_HEADS
ATT_DIM = N_HEADS * HEAD_DIM
KV_DIM = 2 * N_KV_HEADS * HEAD_DIM
CMP_LEN = 32
CMP_STRIDE = 16
CMP_HID = 128
SEL_BLOCK = 64
SEL_TOPK = 16
WINDOW = 512
Q_BLOCK = 128
SSM_HEADS = 8
SSM_HEAD_DIM = 64
D_INNER = SSM_HEADS * SSM_HEAD_DIM
N_GROUPS = 2
D_STATE = 128
CONV_W = 4
CONV_DIM = D_INNER + 2 * N_GROUPS * D_STATE
SSD_CHUNK = 128
D_FF = 2816
PLE_DIM = 256
PAGE_SIZE = 128
EPS = 1e-6

LANES = 128
FF_CHUNK = 256
N_FF_CHUNKS = D_FF // FF_CHUNK
SMALL_W = LANES
N_GATE = 3 * N_HEADS
IN_PROJ_PAD = ATT_DIM + 3 * KV_DIM + D_INNER + CONV_DIM + SMALL_W
VMEM_LIMIT = 56 * 1024 * 1024
NEG_BIG = -1e30
IMP_FORCED = 3e38
IMP_BLOCKED = -1e38
IMP_TAKEN = -3e38


def _dot(a, b):
    return jnp.dot(a, b, preferred_element_type=F32)


def _dot_nt(a, b):
    return lax.dot_general(a, b, (((1,), (1,)), ((), ())), preferred_element_type=F32)


def _rms(x, g):
    return x * lax.rsqrt(jnp.mean(x * x, axis=-1, keepdims=True) + EPS) * g


def _silu(x):
    return x * jax.nn.sigmoid(x)


def _const_spec(shape):
    nd = len(shape)
    return pl.BlockSpec(shape, lambda *_: (0,) * nd, pipeline_mode=pl.Buffered(1))


def _row_spec(tm, width):
    return pl.BlockSpec((tm, width), lambda i: (i, 0))


def _swiglu_half_step(x, g_ref, wg_ref, wu_ref, wd_ref, acc_ref):
    xn = _rms(x, g_ref[...]).astype(BF16)
    acc_ref[...] = jnp.zeros_like(acc_ref)

    def body(c, carry):
        a = _dot(xn, wg_ref[c])
        b = _dot(xn, wu_ref[c])
        hm = (_silu(a) * b).astype(BF16)
        acc_ref[...] += _dot(hm, wd_ref[c])
        return carry

    lax.fori_loop(0, N_FF_CHUNKS, body, 0, unroll=True)
    return x + 0.5 * acc_ref[...]


_IN_SEGS = (ATT_DIM, KV_DIM, KV_DIM, KV_DIM, D_INNER, CONV_DIM, SMALL_W)


SEL_TILE = 512
SEL_LANES = 128
AUG = SEL_LANES + 2 * HEAD_DIM
SEL_SHIFT = SEL_BLOCK.bit_length() - 1
VAUG_ROWS = HEAD_DIM + 16


def _ffn_inproj_kernel(x_ref, g1_ref, wg_ref, wu_ref, wd_ref, gm_ref, win_ref,
                       h_ref, q_ref, kvc_ref, kvs_ref, kvw_ref, z_ref, xbc_ref, small_ref, *rest, tiles):
    *prompt_refs, acc_ref = rest
    tm = x_ref.shape[0]
    h = _swiglu_half_step(x_ref[...], g1_ref, wg_ref, wu_ref, wd_ref, acc_ref)
    h_ref[...] = h
    hn = _rms(h, gm_ref[...]).astype(BF16)
    off = 0
    for out_ref, width in zip((q_ref, kvc_ref, kvs_ref, kvw_ref, z_ref, xbc_ref, small_ref), _IN_SEGS):
        out_ref[...] = _dot(hn, win_ref[:, off:off + width])
        off += width
    if not prompt_refs:
        return
    kvct_ref, kvst_ref, kaug_ref, vaugt_ref, kwin_ref = prompt_refs
    kvs = kvs_ref[...]
    kvc_t = kvc_ref[...].T
    kvs_t = kvs.T
    for ch in range(2 * N_KV_HEADS):
        rows = slice(ch * HEAD_DIM, (ch + 1) * HEAD_DIM)
        kvct_ref[0, 0, ch // N_KV_HEADS, ch % N_KV_HEADS] = kvc_t[rows, :]
        kvst_ref[0, 0, ch // N_KV_HEADS, ch % N_KV_HEADS] = kvs_t[rows, :]
    kwin_ref[...] = kvw_ref[...].astype(BF16)
    pos = (pl.program_id(0) % tiles) * tm + lax.broadcasted_iota(jnp.int32, (tm, 1), 0)
    onehot = ((pos >> SEL_SHIFT) == lax.broadcasted_iota(jnp.int32, (1, SEL_LANES), 1)).astype(BF16)
    ones_row = (lax.broadcasted_iota(jnp.int32, (VAUG_ROWS - HEAD_DIM, tm), 0) == 0).astype(BF16)
    for hh in range(N_KV_HEADS):
        kaug_ref[0, hh, :, 0:SEL_LANES] = onehot
        kaug_ref[0, hh, :, SEL_LANES:SEL_LANES + HEAD_DIM] = kvs[:, hh * HEAD_DIM:(hh + 1) * HEAD_DIM].astype(BF16)
        kaug_ref[0, hh, :, SEL_LANES + HEAD_DIM:] = jnp.zeros((tm, HEAD_DIM), BF16)
        vaugt_ref[0, hh, 0, 0:HEAD_DIM, :] = kvs_t[(N_KV_HEADS + hh) * HEAD_DIM:(N_KV_HEADS + hh + 1) * HEAD_DIM, :].astype(BF16)
        vaugt_ref[0, hh, 0, HEAD_DIM:, :] = ones_row


def _ffn_inproj(x, g1, wg, wu, wd, gm, win, tm, cache_batch=None):
    n = x.shape[0]
    outs = [jax.ShapeDtypeStruct((n, D_MODEL), F32)] + [jax.ShapeDtypeStruct((n, w), F32) for w in _IN_SEGS]
    out_specs = [_row_spec(tm, D_MODEL)] + [_row_spec(tm, w) for w in _IN_SEGS]
    tiles = 1
    if cache_batch is not None:
        l = n // cache_batch
        tiles = l // tm
        assert l % tm == 0 and tm == SEL_TILE
        outs += [jax.ShapeDtypeStruct((cache_batch, 1, 2, N_KV_HEADS, HEAD_DIM, l), F32)] * 2
        out_specs += [pl.BlockSpec((1, 1, 2, N_KV_HEADS, HEAD_DIM, tm),
                                   lambda i: (i // tiles, 0, 0, 0, 0, i % tiles))] * 2
        outs += [jax.ShapeDtypeStruct((cache_batch, N_KV_HEADS, l, AUG), BF16),
                 jax.ShapeDtypeStruct((cache_batch, N_KV_HEADS, tiles, VAUG_ROWS, tm), BF16),
                 jax.ShapeDtypeStruct((n, KV_DIM), BF16)]
        out_specs += [pl.BlockSpec((1, N_KV_HEADS, tm, AUG), lambda i: (i // tiles, 0, i % tiles, 0)),
                      pl.BlockSpec((1, N_KV_HEADS, 1, VAUG_ROWS, tm), lambda i: (i // tiles, 0, i % tiles, 0, 0)),
                      _row_spec(tm, KV_DIM)]
    return pl.pallas_call(
        functools.partial(_ffn_inproj_kernel, tiles=tiles),
        grid=(n // tm,),
        in_specs=[_row_spec(tm, D_MODEL), _const_spec(g1.shape), _const_spec(wg.shape), _const_spec(wu.shape),
                  _const_spec(wd.shape), _const_spec(gm.shape), _const_spec(win.shape)],
        out_specs=out_specs,
        out_shape=outs,
        scratch_shapes=[pltpu.VMEM((tm, D_MODEL), F32)],
        compiler_params=pltpu.CompilerParams(dimension_semantics=("arbitrary",), vmem_limit_bytes=VMEM_LIMIT),
        name="ffn1_inproj",
    )(x, g1, wg, wu, wd, gm, win)


HD2 = N_KV_HEADS * HEAD_DIM
CMP_R = CMP_LEN // CMP_STRIDE
CMP_PROJ = N_KV_HEADS * CMP_R * CMP_HID
PE_ROWS = 16
CMP_K = CMP_STRIDE * HD2


def _cmp_weights(cmp_w1, cmp_pe, cmp_b1, cmp_w2):
    w1r = cmp_w1.reshape(2, CMP_R, CMP_STRIDE, HEAD_DIM, CMP_HID)
    per = cmp_pe.reshape(2, CMP_R, CMP_STRIDE, HEAD_DIM)
    wbd = jnp.zeros((2, CMP_STRIDE, N_KV_HEADS, HEAD_DIM, N_KV_HEADS, CMP_R, CMP_HID), F32)
    blk = jnp.transpose(w1r, (0, 2, 3, 1, 4))
    for h in range(N_KV_HEADS):
        wbd = wbd.at[:, :, h, :, h, :, :].set(blk)
    wbd = wbd.reshape(2, CMP_K, CMP_PROJ).astype(BF16)
    pe_rows = jnp.broadcast_to(per[:, :, :, None, :], (2, CMP_R, CMP_STRIDE, N_KV_HEADS, HEAD_DIM))
    pe_rows = jnp.concatenate([pe_rows.reshape(2, CMP_R, CMP_K), jnp.zeros((2, PE_ROWS - CMP_R, CMP_K), F32)], axis=1)
    return wbd, pe_rows.astype(BF16), cmp_b1, cmp_w2.astype(BF16)


def _cmp_tokens(load_rows, nb, lhs_ref, wbd_ref, pe_ref, b1_ref, w2_ref):
    outs = []
    for c in range(2):
        for s in range(CMP_STRIDE):
            lhs_ref[0:nb, s * HD2:(s + 1) * HD2] = load_rows(c, s).astype(BF16)
        lhs_ref[nb:nb + PE_ROWS, :] = pe_ref[c]
        acc = _dot(lhs_ref[...], wbd_ref[c])
        proj, pe_proj = acc[:nb], acc[nb:]
        for h in range(N_KV_HEADS):
            lo = h * CMP_R * CMP_HID
            p0 = proj[:, lo:lo + CMP_HID]
            p1 = pltpu.roll(proj[:, lo + CMP_HID:lo + 2 * CMP_HID], nb - 1, 0)
            pe_add = pe_proj[0:1, lo:lo + CMP_HID] + pe_proj[1:2, lo + CMP_HID:lo + 2 * CMP_HID]
            hid = _silu(p0 + p1 + pe_add + b1_ref[c:c + 1, :])
            outs.append(_dot(hid.astype(BF16), w2_ref[c]))
    return jnp.concatenate(outs, axis=1)


def _cmp_prompt_kernel(xk_ref, xv_ref, wbd_ref, pe_ref, b1_ref, w2_ref, out_ref, out_t_ref, lhs_ref):
    nb = out_ref.shape[1]
    load_rows = lambda c, s: (xk_ref, xv_ref)[c][0, pl.ds(s, nb, stride=CMP_STRIDE), :]
    tokens = _cmp_tokens(load_rows, nb, lhs_ref, wbd_ref, pe_ref, b1_ref, w2_ref)
    out_ref[0] = tokens
    out_t_ref[0] = tokens.T


def _cmp_prompt(kvc, wbd, pe_rows, b1, w2):
    b, l, _ = kvc.shape
    nb = l // CMP_STRIDE
    return pl.pallas_call(
        _cmp_prompt_kernel,
        grid=(b,),
        in_specs=[pl.BlockSpec((1, l, HD2), lambda i: (i, 0, 0)), pl.BlockSpec((1, l, HD2), lambda i: (i, 0, 1)),
                  _const_spec(wbd.shape), _const_spec(pe_rows.shape), _const_spec(b1.shape), _const_spec(w2.shape)],
        out_specs=[pl.BlockSpec((1, nb, KV_DIM), lambda i: (i, 0, 0)), pl.BlockSpec((1, KV_DIM, nb), lambda i: (i, 0, 0))],
        out_shape=[jax.ShapeDtypeStruct((b, nb, KV_DIM), F32), jax.ShapeDtypeStruct((b, KV_DIM, nb), F32)],
        scratch_shapes=[pltpu.VMEM((nb + PE_ROWS, CMP_K), BF16)],
        compiler_params=pltpu.CompilerParams(dimension_semantics=("arbitrary",), vmem_limit_bytes=VMEM_LIMIT),
        name="nsa_cmp_prompt",
    )(kvc, kvc, wbd, pe_rows, b1, w2)


COLS = GQA * Q_BLOCK
TINY = float(np.finfo(np.float32).tiny)
LOG2E = float(np.log2(np.e))


def _softmax_rows(s):
    m = jnp.max(s, axis=-1, keepdims=True)
    m = jnp.where(m > 0.5 * NEG_BIG, m, 0.0)
    e = jnp.exp(s - m)
    return e / jnp.maximum(jnp.sum(e, axis=-1, keepdims=True), TINY)


def _select_blocks(imp_t, t_lane):
    jj = lax.broadcasted_iota(jnp.int32, imp_t.shape, 0)
    cb = t_lane >> SEL_SHIFT
    forced = (jj == 0) | (jj == cb) | (jj == cb - 1)
    causal = (jj << SEL_SHIFT) <= t_lane
    x = jnp.where(forced, IMP_FORCED, imp_t)
    x = jnp.where(causal, x, IMP_BLOCKED)
    sel, _ = _take_top(x, SEL_TOPK)
    return sel


def _take_top(x, k):
    nj = x.shape[0]
    jf = lax.broadcasted_iota(jnp.int32, x.shape, 0).astype(F32)
    sel = jnp.zeros_like(x)
    picks = []
    for _ in range(k):
        m = jnp.max(x, axis=0, keepdims=True)
        first = jnp.min(jnp.where(x == m, jf, float(nj)), axis=0, keepdims=True)
        hit = jf == first
        sel = jnp.where(hit, 1.0, sel)
        x = jnp.where(hit, IMP_TAKEN, x)
        picks.append(first)
    return sel, picks


def _softmax_cols(s):
    m = jnp.max(s, axis=0, keepdims=True)
    m = jnp.where(m > 0.5 * NEG_BIG, m, 0.0)
    e = jnp.exp2(s - m)
    return e, 1.0 / jnp.maximum(jnp.sum(e, axis=0, keepdims=True), TINY)


def _dot_tn(a, b):
    return lax.dot_general(a, b, (((0,), (0,)), ((), ())), preferred_element_type=F32)


def _nsa_prompt_kernel(q_ref, small_ref, kcv_ref, kcvt_ref, kaug_ref, vaugt_ref, kwin_ref, mselt_ref, out_ref,
                       lt_ref, acc_ref, ot_ref, sa_ref, sb_ref):
    n = pl.program_id(1)
    t0 = n * Q_BLOCK
    nb = kcv_ref.shape[1]
    col_t = t0 + (lax.broadcasted_iota(jnp.int32, (1, COLS), 1) & (Q_BLOCK - 1))
    lane_t = t0 + lax.broadcasted_iota(jnp.int32, (1, Q_BLOCK), 1)
    gates_t = jax.nn.sigmoid(small_ref[0]).T
    q_t = (q_ref[0] * (HEAD_DIM ** -0.5 * LOG2E)).T.astype(BF16)
    c_end = lax.broadcasted_iota(jnp.int32, (nb, 1), 0) * CMP_STRIDE + (CMP_LEN - 1)
    o_cmp_t = []

    for h in range(N_KV_HEADS):
        for g in range(GQA):
            lo = (h * GQA + g) * HEAD_DIM
            lt_ref[h, SEL_LANES:SEL_LANES + HEAD_DIM, g * Q_BLOCK:(g + 1) * Q_BLOCK] = q_t[lo:lo + HEAD_DIM, :]
        lt_ref[h, SEL_LANES + HEAD_DIM:, :] = jnp.zeros((HEAD_DIM, COLS), BF16)
        qh_t = lt_ref[h, SEL_LANES:SEL_LANES + HEAD_DIM, :]

        kc = kcv_ref[0, :, h * HEAD_DIM:(h + 1) * HEAD_DIM].astype(BF16)
        vc_t = kcvt_ref[0, (N_KV_HEADS + h) * HEAD_DIM:(N_KV_HEADS + h + 1) * HEAD_DIM, :].astype(BF16)
        e_t, r_t = _softmax_cols(jnp.where(c_end <= col_t, _dot(kc, qh_t), NEG_BIG))
        o_cmp_t.append(_dot(vc_t, e_t.astype(BF16)) * r_t)

        psum = e_t[:, 0:Q_BLOCK] * r_t[:, 0:Q_BLOCK]
        for g in range(1, GQA):
            psum = psum + e_t[:, g * Q_BLOCK:(g + 1) * Q_BLOCK] * r_t[:, g * Q_BLOCK:(g + 1) * Q_BLOCK]
        p_hi = psum.astype(BF16)
        p_lo = (psum - p_hi.astype(F32)).astype(BF16)
        imp_t = _dot(mselt_ref[...], p_hi) + _dot(mselt_ref[...], p_lo)
        neg = jnp.where(_select_blocks(imp_t, lane_t) > 0.0, 0.0, NEG_BIG).astype(BF16)
        for g in range(GQA):
            lt_ref[h, 0:SEL_LANES, g * Q_BLOCK:(g + 1) * Q_BLOCK] = neg

    acc_ref[...] = jnp.zeros(acc_ref.shape, F32)
    n_pairs = (t0 + Q_BLOCK + 2 * SEL_TILE - 1) // (2 * SEL_TILE)

    def scores(s_ref, kt):
        k0 = pl.multiple_of(kt * SEL_TILE, SEL_TILE)
        for h in range(N_KV_HEADS):
            s_ref[h] = _dot(kaug_ref[0, h, pl.ds(k0, SEL_TILE), :], lt_ref[h])

    def consume(s_ref, kt, ms, causal_mask):
        out = []
        for h in range(N_KV_HEADS):
            s = s_ref[h]
            if causal_mask:
                kpos = kt * SEL_TILE + lax.broadcasted_iota(jnp.int32, (SEL_TILE, 1), 0)
                s = jnp.where(kpos <= col_t, s, NEG_BIG)
            m_new = jnp.maximum(ms[h], jnp.max(s, axis=0, keepdims=True))
            pe = jnp.exp2(s - m_new).astype(BF16)
            acc_ref[h] = jnp.exp2(ms[h] - m_new) * acc_ref[h] + _dot(vaugt_ref[0, h, kt], pe)
            out.append(m_new)
        return tuple(out)

    def pair(i, ms):
        scores(sb_ref, 2 * i + 1)
        ms = consume(sa_ref, 2 * i, ms, False)
        scores(sa_ref, 2 * i + 2)
        return consume(sb_ref, 2 * i + 1, ms, False)

    scores(sa_ref, 0)
    m_init = (jnp.full((1, COLS), NEG_BIG, F32),) * N_KV_HEADS
    ms = lax.fori_loop(0, n_pairs - 1, pair, m_init)
    last = 2 * (n_pairs - 1)
    two_tiles = t0 + Q_BLOCK > (last + 1) * SEL_TILE

    @pl.when(two_tiles)
    def _():
        scores(sb_ref, last + 1)
        consume(sb_ref, last + 1, consume(sa_ref, last, ms, True), True)

    @pl.when(jnp.logical_not(two_tiles))
    def _():
        consume(sa_ref, last, ms, True)

    w0 = pl.multiple_of(jnp.maximum(t0 - WINDOW, 0), Q_BLOCK)
    kpos = w0 + lax.broadcasted_iota(jnp.int32, (WINDOW + Q_BLOCK, 1), 0)
    dpos = col_t - kpos
    win_ok = (dpos >= 0) & (dpos < WINDOW)
    for h in range(N_KV_HEADS):
        acc = acc_ref[h]
        o_sel_t = acc[:HEAD_DIM] / jnp.maximum(acc[HEAD_DIM:HEAD_DIM + 1], TINY)

        kw = kwin_ref[0, pl.ds(w0, WINDOW + Q_BLOCK), h * HEAD_DIM:(h + 1) * HEAD_DIM]
        vw = kwin_ref[0, pl.ds(w0, WINDOW + Q_BLOCK), (N_KV_HEADS + h) * HEAD_DIM:(N_KV_HEADS + h + 1) * HEAD_DIM]
        qh_t = lt_ref[h, SEL_LANES:SEL_LANES + HEAD_DIM, :]
        e_t, r_t = _softmax_cols(jnp.where(win_ok, _dot(kw, qh_t), NEG_BIG))
        o_win_t = _dot_tn(vw, e_t.astype(BF16)) * r_t

        for g in range(GQA):
            hd = h * GQA + g
            cols = slice(g * Q_BLOCK, (g + 1) * Q_BLOCK)
            ot_ref[hd * HEAD_DIM:(hd + 1) * HEAD_DIM, :] = (
                gates_t[3 * hd:3 * hd + 1, :] * o_cmp_t[h][:, cols] + gates_t[3 * hd + 1:3 * hd + 2, :] * o_sel_t[:, cols]
                + gates_t[3 * hd + 2:3 * hd + 3, :] * o_win_t[:, cols])
    out_ref[0] = ot_ref[...].T


def _cmp_to_sel_t(nb, n_cmp, n_blk):
    cs = np.arange(nb)[None, :] * CMP_STRIDE
    bs = np.arange(SEL_LANES)[:, None] * SEL_BLOCK
    ov = np.clip(np.minimum(cs + CMP_LEN, bs + SEL_BLOCK) - np.maximum(cs, bs), 0, None) / CMP_LEN
    ov = ov * (np.arange(nb)[None, :] < n_cmp) * (np.arange(SEL_LANES)[:, None] < n_blk)
    return jnp.asarray(ov, BF16)


def _nsa_prompt(q, small, kcv, kcv_t, kaug, vaug_t, kwin):
    b, l, _ = q.shape
    nb = kcv.shape[1]
    n_blk = l // SEL_BLOCK
    assert l % (2 * SEL_TILE) == 0 and n_blk <= SEL_LANES and l >= WINDOW + Q_BLOCK
    mselt = _cmp_to_sel_t(nb, (l - CMP_LEN) // CMP_STRIDE + 1, n_blk)

    def per_batch(shape):
        nd = len(shape)
        return pl.BlockSpec((1,) + shape[1:], lambda i, j: (i,) + (0,) * (nd - 1), pipeline_mode=pl.Buffered(1))

    return pl.pallas_call(
        _nsa_prompt_kernel,
        grid=(b, l // Q_BLOCK),
        in_specs=[pl.BlockSpec((1, Q_BLOCK, ATT_DIM), lambda i, j: (i, j, 0)),
                  pl.BlockSpec((1, Q_BLOCK, SMALL_W), lambda i, j: (i, j, 0)),
                  per_batch(kcv.shape), per_batch(kcv_t.shape), per_batch(kaug.shape), per_batch(vaug_t.shape),
                  per_batch(kwin.shape), pl.BlockSpec(mselt.shape, lambda i, j: (0, 0), pipeline_mode=pl.Buffered(1))],
        out_specs=pl.BlockSpec((1, Q_BLOCK, ATT_DIM), lambda i, j: (i, j, 0)),
        out_shape=jax.ShapeDtypeStruct((b, l, ATT_DIM), F32),
        scratch_shapes=[pltpu.VMEM((N_KV_HEADS, AUG, COLS), BF16), pltpu.VMEM((N_KV_HEADS, VAUG_ROWS, COLS), F32),
                        pltpu.VMEM((ATT_DIM, Q_BLOCK), F32), pltpu.VMEM((N_KV_HEADS, SEL_TILE, COLS), F32),
                        pltpu.VMEM((N_KV_HEADS, SEL_TILE, COLS), F32)],
        compiler_params=pltpu.CompilerParams(dimension_semantics=("arbitrary", "arbitrary"),
                                             vmem_limit_bytes=VMEM_LIMIT),
        name="nsa_prompt",
    )(q, small, kcv, kcv_t, kaug, vaug_t, kwin, mselt)


DT_LANE = N_GATE
HEADS_PER_GROUP = SSM_HEADS // N_GROUPS
GROUP_W = D_INNER // N_GROUPS
TAIL = 8


def _expand_heads(v):
    rows = v.shape[0]
    lane = lax.broadcasted_iota(jnp.int32, (rows, LANES), 1)
    tiles = []
    for j in range(D_INNER // LANES):
        a = jnp.broadcast_to(v[:, DT_LANE + 2 * j:DT_LANE + 2 * j + 1], (rows, LANES))
        b = jnp.broadcast_to(v[:, DT_LANE + 2 * j + 1:DT_LANE + 2 * j + 2], (rows, LANES))
        tiles.append(jnp.where(lane < SSM_HEAD_DIM, a, b))
    return jnp.concatenate(tiles, axis=1)


def _cumsum_rows(x):
    n = x.shape[0]
    row = lax.broadcasted_iota(jnp.int32, x.shape, 0)
    s = 1
    while s < n:
        x = x + jnp.where(row >= s, pltpu.roll(x, s, 0), 0.0)
        s *= 2
    return x


def _grouped_norm_gate(y, z, norm):
    v = y * _silu(z)
    outs = []
    for g in range(N_GROUPS):
        vg = v[:, g * GROUP_W:(g + 1) * GROUP_W]
        outs.append(vg * lax.rsqrt(jnp.mean(vg * vg, axis=-1, keepdims=True) + EPS))
    return jnp.concatenate(outs, axis=1) * norm


def _ssd_prompt_kernel(xbc_ref, z_ref, small_ref, convw_ref, convb_ref, dtb_ref, alog_ref, dskip_ref, norm_ref,
                       y_ref, state_ref, xe_ref, st_ref, yd_ref):
    c = pl.program_id(1)
    lc = xbc_ref.shape[1]

    @pl.when(c == 0)
    def _():
        xe_ref[0:TAIL, :] = jnp.zeros((TAIL, CONV_DIM), F32)
        st_ref[...] = jnp.zeros(st_ref.shape, F32)

    xe_ref[TAIL:TAIL + lc, :] = xbc_ref[0]
    conv = convb_ref[...] + xe_ref[TAIL:TAIL + lc, :] * convw_ref[CONV_W - 1:CONV_W, :]
    for k in range(CONV_W - 1):
        conv = conv + xe_ref[pl.ds(TAIL - (CONV_W - 1) + k, lc), :] * convw_ref[k:k + 1, :]
    xe_ref[0:TAIL, :] = xe_ref[lc:lc + TAIL, :]
    xc = _silu(conv)
    xs = xc[:, :D_INNER]

    dt = jax.nn.softplus(small_ref[0] + dtb_ref[...])
    ad = dt * (-jnp.exp(alog_ref[...]))
    acs = _cumsum_rows(ad)
    acs_t = acs.T
    dt_e = _expand_heads(dt)
    acs_e = _expand_heads(acs)
    last_e = acs_e[lc - 1:lc, :]
    xd = xs * dt_e
    xd_bf = xd.astype(BF16)
    xdd_bf = (xd * jnp.exp(last_e - acs_e)).astype(BF16)
    grow = jnp.exp(acs_e)
    li = lax.broadcasted_iota(jnp.int32, (lc, lc), 0)
    si = lax.broadcasted_iota(jnp.int32, (lc, lc), 1)

    y_off = []
    for g in range(N_GROUPS):
        bm = xc[:, D_INNER + g * D_STATE:D_INNER + (g + 1) * D_STATE]
        cm = xc[:, D_INNER + N_GROUPS * D_STATE + g * D_STATE:D_INNER + N_GROUPS * D_STATE + (g + 1) * D_STATE]
        bm_bf = bm.astype(BF16)
        cm_bf = cm.astype(BF16)
        cb = _dot_nt(cm_bf, bm_bf)
        cols = slice(g * GROUP_W, (g + 1) * GROUP_W)
        st_g = st_ref[:, cols]
        y_off.append(_dot(cm_bf, st_g.astype(BF16)))
        for hh in range(HEADS_PER_GROUP):
            h = g * HEADS_PER_GROUP + hh
            seg = acs[:, DT_LANE + h:DT_LANE + h + 1] - acs_t[DT_LANE + h:DT_LANE + h + 1, :]
            m = jnp.where(li >= si, cb * jnp.exp(seg), 0.0).astype(BF16)
            yd_ref[:, h * SSM_HEAD_DIM:(h + 1) * SSM_HEAD_DIM] = _dot(m, xd_bf[:, h * SSM_HEAD_DIM:(h + 1) * SSM_HEAD_DIM])
        st_ref[:, cols] = st_g * jnp.exp(last_e[:, cols]) + _dot(bm.T.astype(BF16), xdd_bf[:, cols])

    y = yd_ref[...] + jnp.concatenate(y_off, axis=1) * grow + dskip_ref[...] * xs
    y_ref[0] = _grouped_norm_gate(y, z_ref[0], norm_ref[...])

    @pl.when(c == pl.num_programs(1) - 1)
    def _():
        state_ref[0] = st_ref[...].T


def _pad_small(v):
    return jnp.zeros((1, SMALL_W), F32).at[0, DT_LANE:DT_LANE + SSM_HEADS].set(v.astype(F32))


def _ssd_prompt(xbc, z, small, conv_w, conv_b, dt_bias, a_log, d_skip, ssm_norm):
    b, l, _ = xbc.shape
    lc = min(SSD_CHUNK, l)
    params = (conv_w, conv_b.reshape(1, CONV_DIM), _pad_small(dt_bias), _pad_small(a_log),
              jnp.repeat(d_skip.astype(F32), SSM_HEAD_DIM).reshape(1, D_INNER), ssm_norm.reshape(1, D_INNER))
    tile = lambda w: pl.BlockSpec((1, lc, w), lambda i, j: (i, j, 0))
    return pl.pallas_call(
        _ssd_prompt_kernel,
        grid=(b, l // lc),
        in_specs=[tile(CONV_DIM), tile(D_INNER), tile(SMALL_W)] + [_const_spec(p.shape) for p in params],
        out_specs=[tile(D_INNER), pl.BlockSpec((1, D_INNER, D_STATE), lambda i, j: (i, 0, 0))],
        out_shape=[jax.ShapeDtypeStruct((b, l, D_INNER), F32), jax.ShapeDtypeStruct((b, D_INNER, D_STATE), F32)],
        scratch_shapes=[pltpu.VMEM((lc + TAIL, CONV_DIM), F32), pltpu.VMEM((D_STATE, D_INNER), F32),
                        pltpu.VMEM((lc, D_INNER), F32)],
        compiler_params=pltpu.CompilerParams(dimension_semantics=("arbitrary", "arbitrary"),
                                             vmem_limit_bytes=VMEM_LIMIT),
        name="ssd_prompt",
    )(xbc, z, small, *params)


def _out_ffn_kernel(h_ref, att_ref, ssm_ref, p_ref, woa_ref, wos_ref, g2_ref, wg_ref, wu_ref, wd_ref,
                    gp_ref, wpg_ref, wple_ref, gf_ref, y_ref, acc_ref):
    h = h_ref[...] + _dot(att_ref[...].astype(BF16), woa_ref[...]) + _dot(ssm_ref[...].astype(BF16), wos_ref[...])
    h = _swiglu_half_step(h, g2_ref, wg_ref, wu_ref, wd_ref, acc_ref)
    gate = jax.nn.sigmoid(_dot(_rms(h, gp_ref[...]).astype(BF16), wpg_ref[...]))
    h = h + gate * _dot(p_ref[...].astype(BF16), wple_ref[...])
    y_ref[...] = _rms(h, gf_ref[...])


def _out_ffn(h, att, ssm, p, weights, tm):
    n = h.shape[0]
    return pl.pallas_call(
        _out_ffn_kernel,
        grid=(n // tm,),
        in_specs=[_row_spec(tm, D_MODEL), _row_spec(tm, ATT_DIM), _row_spec(tm, D_INNER), _row_spec(tm, PLE_DIM)]
        + [_const_spec(w.shape) for w in weights],
        out_specs=_row_spec(tm, D_MODEL),
        out_shape=jax.ShapeDtypeStruct((n, D_MODEL), F32),
        scratch_shapes=[pltpu.VMEM((tm, D_MODEL), F32)],
        compiler_params=pltpu.CompilerParams(dimension_semantics=("arbitrary",), vmem_limit_bytes=VMEM_LIMIT),
        name="outproj_ffn2_ple",
    )(h, att, ssm, p, *weights)


BLOCKS_PER_PAGE = PAGE_SIZE // CMP_STRIDE
HEAD_ROWS = 8


def _head_rows_of(h):
    row = lax.broadcasted_iota(jnp.int32, (HEAD_ROWS, 1), 0)
    return (row >= h * GQA) & (row < (h + 1) * GQA)


def _cmp_sample_kernel(pt_ref, q_ref, cache_ref, wbd_ref, pe_ref, b1_ref, w2_ref, mselt_ref, perm_ref,
                       ocmp_ref, imp_ref, xbuf_ref, xrow_ref, lhs_ref, sem_ref, *, n_cmp):
    b = pl.program_id(0)
    nb_total = pl.num_programs(0)
    n_pages = pt_ref.shape[1]
    nbp = n_pages * BLOCKS_PER_PAGE

    def page_copy(bb, slot, i):
        return pltpu.make_async_copy(cache_ref.at[pt_ref[bb, i], 0], xbuf_ref.at[slot, i], sem_ref.at[slot])

    def start_all(bb, slot):
        def body(i2, carry):
            for prio in range(2):
                page_copy(bb, slot, 2 * i2 + prio).start(priority=prio)
            return carry
        lax.fori_loop(0, n_pages // 2, body, 0)

    def wait_all(bb, slot):
        def body(i, carry):
            page_copy(bb, slot, i).wait()
            return carry
        lax.fori_loop(0, n_pages, body, 0)

    slot = b % 2

    @pl.when(b == 0)
    def _():
        start_all(0, 0)

    @pl.when(b + 1 < nb_total)
    def _():
        start_all(b + 1, 1 - slot)

    wait_all(b, slot)

    def load_rows(c, s):
        if c == 0 and s == 0:
            for i in range(n_pages):
                page = xbuf_ref[slot, i].reshape(KV_DIM, PAGE_SIZE).astype(BF16)
                rows = _dot_nt(perm_ref[...], page)
                for cc in range(2):
                    xrow_ref[cc, i] = rows[:, cc * HD2:(cc + 1) * HD2].reshape(CMP_STRIDE, BLOCKS_PER_PAGE, HD2)
        return xrow_ref[c, :, s].reshape(nbp, HD2)

    kcv = _cmp_tokens(load_rows, nbp, lhs_ref, wbd_ref, pe_ref, b1_ref, w2_ref)
    q8 = (q_ref[0] * (HEAD_DIM ** -0.5)).astype(BF16)
    valid = lax.broadcasted_iota(jnp.int32, (1, nbp), 1) < n_cmp
    o_cmp = jnp.zeros((HEAD_ROWS, HEAD_DIM), F32)
    psum = jnp.zeros((HEAD_ROWS, nbp), F32)
    row = lax.broadcasted_iota(jnp.int32, (HEAD_ROWS, 1), 0)
    for h in range(N_KV_HEADS):
        kc = kcv[:, h * HEAD_DIM:(h + 1) * HEAD_DIM].astype(BF16)
        vc = kcv[:, (N_KV_HEADS + h) * HEAD_DIM:(N_KV_HEADS + h + 1) * HEAD_DIM].astype(BF16)
        p = _softmax_rows(jnp.where(valid, _dot_nt(q8, kc), NEG_BIG))
        mine = _head_rows_of(h)
        o_cmp = jnp.where(mine, _dot(p.astype(BF16), vc), o_cmp)
        ph = jnp.sum(jnp.where(mine, p, 0.0), axis=0, keepdims=True)
        psum = jnp.where(row == h, ph, psum)
    ocmp_ref[0] = o_cmp
    p_hi = psum.astype(BF16)
    p_lo = (psum - p_hi.astype(F32)).astype(BF16)
    imp_ref[0] = _dot_nt(p_hi, mselt_ref[...]) + _dot_nt(p_lo, mselt_ref[...])


def _cmp_sample(page_table, q8, cache_t, wbd, pe_rows, b1, w2, mselt, n_cmp):
    db, n_pages = page_table.shape
    assert n_pages % 2 == 0
    r = np.arange(PAGE_SIZE)
    perm = jnp.asarray(r[:, None] == (r % CMP_STRIDE * BLOCKS_PER_PAGE + r // CMP_STRIDE)[None, :], BF16)
    consts = (wbd, pe_rows, b1, w2, mselt, perm)
    grid_spec = pltpu.PrefetchScalarGridSpec(
        num_scalar_prefetch=1,
        grid=(db,),
        in_specs=[pl.BlockSpec((1, HEAD_ROWS, HEAD_DIM), lambda i, pt: (i, 0, 0)),
                  pl.BlockSpec(memory_space=pl.ANY)]
        + [pl.BlockSpec(c.shape, lambda i, pt, nd=c.ndim: (0,) * nd, pipeline_mode=pl.Buffered(1)) for c in consts],
        out_specs=[pl.BlockSpec((1, HEAD_ROWS, HEAD_DIM), lambda i, pt: (i, 0, 0)),
                   pl.BlockSpec((1, HEAD_ROWS, SEL_LANES), lambda i, pt: (i, 0, 0))],
        scratch_shapes=[pltpu.VMEM((2, n_pages, 2, N_KV_HEADS, HEAD_DIM, PAGE_SIZE), F32),
                        pltpu.VMEM((2, n_pages, CMP_STRIDE, BLOCKS_PER_PAGE, HD2), F32),
                        pltpu.VMEM((n_pages * BLOCKS_PER_PAGE + PE_ROWS, CMP_K), BF16), pltpu.SemaphoreType.DMA((2,))],
    )
    return pl.pallas_call(
        functools.partial(_cmp_sample_kernel, n_cmp=n_cmp),
        grid_spec=grid_spec,
        out_shape=[jax.ShapeDtypeStruct((db, HEAD_ROWS, HEAD_DIM), F32),
                   jax.ShapeDtypeStruct((db, HEAD_ROWS, SEL_LANES), F32)],
        compiler_params=pltpu.CompilerParams(dimension_semantics=("arbitrary",), vmem_limit_bytes=VMEM_LIMIT),
        name="nsa_cmp_sample",
    )(page_table, q8, cache_t, *consts)


N_PAST_PICKS = SEL_TOPK - 1


def _topk_sample_kernel(imp_ref, idx_ref, *, past_blk):
    x = imp_ref[...].T
    jj = lax.broadcasted_iota(jnp.int32, x.shape, 0)
    x = jnp.where((jj == 0) | (jj == past_blk - 1), IMP_FORCED, x)
    x = jnp.where(jj < past_blk, x, IMP_BLOCKED)
    _, picks = _take_top(x, N_PAST_PICKS)
    picks = picks + [jnp.zeros_like(picks[0])] * (idx_ref.shape[0] - N_PAST_PICKS)
    idx_ref[...] = jnp.concatenate(picks, axis=0).astype(jnp.int32)


def _topk_sample(imp2, past_blk):
    nq = imp2.shape[0]
    assert N_PAST_PICKS <= past_blk <= SEL_LANES
    return pl.pallas_call(
        functools.partial(_topk_sample_kernel, past_blk=past_blk),
        out_shape=jax.ShapeDtypeStruct((SEL_TOPK, nq), jnp.int32),
        name="nsa_topk_sample",
    )(imp2)


SUB_PER_PAGE = PAGE_SIZE // SEL_BLOCK
SUB_SHIFT = SUB_PER_PAGE.bit_length() - 1
SEL_SHIFT_IN_PAGE = SEL_BLOCK.bit_length() - 1
N_SEL_COLS = N_PAST_PICKS * PAGE_SIZE


def _attend_one_token(q8, q8f, kt, vt, valid, k_new, v_new):
    s = jnp.where(valid, _dot(q8, kt.astype(BF16)), NEG_BIG)
    s_new = jnp.sum(q8f * k_new, axis=-1, keepdims=True)
    m = jnp.maximum(jnp.max(s, axis=-1, keepdims=True), s_new)
    e = jnp.exp(s - m)
    e_new = jnp.exp(s_new - m)
    norm = jnp.maximum(jnp.sum(e, axis=-1, keepdims=True) + e_new, TINY)
    return (_dot_nt(e.astype(BF16), vt.astype(BF16)) + e_new * v_new) / norm


def _sel_win_sample_kernel(idx_ref, pt_ref, q_ref, gate_ref, ocmp_ref, kvs_ref, kvw_ref, win_ref, slc_ref,
                           att_ref, newwin_ref, gbuf_ref, sem_ref):
    b = pl.program_id(0)
    nb_total = pl.num_programs(0)
    w_buf = win_ref.shape[-1]

    def block_copies(bb, slot, h, k):
        j = idx_ref[k, bb * N_KV_HEADS + h]
        page = pt_ref[bb, j >> SUB_SHIFT]
        return [pltpu.make_async_copy(slc_ref.at[page, 0, c, h],
                                      gbuf_ref.at[slot, h, c, :, pl.ds(k * PAGE_SIZE, PAGE_SIZE)],
                                      sem_ref.at[slot]) for c in range(2)]

    def start_all(bb, slot):
        for h in range(N_KV_HEADS):
            for k in range(N_PAST_PICKS):
                for cp in block_copies(bb, slot, h, k):
                    cp.start()

    def wait_all(bb, slot):
        for h in range(N_KV_HEADS):
            for k in range(N_PAST_PICKS):
                for cp in block_copies(bb, slot, h, k):
                    cp.wait()

    slot = b % 2

    @pl.when(b == 0)
    def _():
        start_all(0, 0)

    @pl.when(b + 1 < nb_total)
    def _():
        start_all(b + 1, 1 - slot)

    wait_all(b, slot)

    q8f = q_ref[0] * (HEAD_DIM ** -0.5)
    q8 = q8f.astype(BF16)
    page_lane = lax.broadcasted_iota(jnp.int32, (1, PAGE_SIZE), 1)
    win_lane = lax.broadcasted_iota(jnp.int32, (1, w_buf), 1)
    win_valid = w_buf - win_lane < WINDOW
    o_sel = jnp.zeros((HEAD_ROWS, HEAD_DIM), F32)
    o_win = jnp.zeros((HEAD_ROWS, HEAD_DIM), F32)
    for h in range(N_KV_HEADS):
        kcols = slice(h * HEAD_DIM, (h + 1) * HEAD_DIM)
        vcols = slice((N_KV_HEADS + h) * HEAD_DIM, (N_KV_HEADS + h + 1) * HEAD_DIM)
        mine = _head_rows_of(h)
        sel_valid = jnp.concatenate(
            [(page_lane >> SEL_SHIFT_IN_PAGE) == (idx_ref[k, b * N_KV_HEADS + h] & (SUB_PER_PAGE - 1))
             for k in range(N_PAST_PICKS)], axis=1)
        o_sel = jnp.where(mine, _attend_one_token(q8, q8f, gbuf_ref[slot, h, 0], gbuf_ref[slot, h, 1], sel_valid,
                                                  kvs_ref[0][:, kcols], kvs_ref[0][:, vcols]), o_sel)
        o_win = jnp.where(mine, _attend_one_token(q8, q8f, win_ref[0, 0, 0, h], win_ref[0, 0, 1, h], win_valid,
                                                  kvw_ref[0][:, kcols], kvw_ref[0][:, vcols]), o_win)
    gates = jax.nn.sigmoid(gate_ref[0])
    att_ref[0] = gates[:, 0:1] * ocmp_ref[0] + gates[:, 1:2] * o_sel + gates[:, 2:3] * o_win

    new_col = jnp.concatenate([kvw_ref[0], jnp.zeros((LANES - 1, KV_DIM), F32)], axis=0).T
    tile_lane = lax.broadcasted_iota(jnp.int32, (HEAD_DIM, w_buf), 1)
    for c in range(2):
        for h in range(N_KV_HEADS):
            lo = (c * N_KV_HEADS + h) * HEAD_DIM
            newwin_ref[0, 0, c, h] = jnp.where(tile_lane == w_buf - 1, new_col[lo:lo + HEAD_DIM, 0:1],
                                               pltpu.roll(win_ref[0, 0, c, h], w_buf - 1, 1))


def _sel_win_sample(idx, page_table, q8, gate8, o_cmp, kvs_new, kvw_new, win_t, slc_t):
    db = page_table.shape[0]
    w_buf = win_t.shape[-1]
    assert w_buf == WINDOW and win_t.shape[1] == 1
    per_tok = lambda shape: pl.BlockSpec((1,) + shape[1:], lambda i, idx, pt, nd=len(shape): (i,) + (0,) * (nd - 1))
    grid_spec = pltpu.PrefetchScalarGridSpec(
        num_scalar_prefetch=2,
        grid=(db,),
        in_specs=[per_tok(q8.shape), per_tok(gate8.shape), per_tok(o_cmp.shape), per_tok(kvs_new.shape),
                  per_tok(kvw_new.shape), per_tok(win_t.shape), pl.BlockSpec(memory_space=pl.ANY)],
        out_specs=[per_tok(o_cmp.shape), per_tok(win_t.shape)],
        scratch_shapes=[pltpu.VMEM((2, N_KV_HEADS, 2, HEAD_DIM, N_SEL_COLS), F32), pltpu.SemaphoreType.DMA((2,))],
    )
    return pl.pallas_call(
        _sel_win_sample_kernel,
        grid_spec=grid_spec,
        out_shape=[jax.ShapeDtypeStruct(o_cmp.shape, F32), jax.ShapeDtypeStruct(win_t.shape, F32)],
        compiler_params=pltpu.CompilerParams(dimension_semantics=("arbitrary",), vmem_limit_bytes=VMEM_LIMIT),
        name="nsa_sel_win_sample",
    )(idx, page_table, q8, gate8, o_cmp, kvs_new, kvw_new, win_t, slc_t)


SSD_BT = 8


def _ssd_sample_kernel(xbc_ref, cst_ref, z_ref, small_ref, h0_ref, convw_ref, convb_ref, dtb_ref, alog_ref,
                       dskip_ref, norm_ref, y_ref, hnew_ref, ys_ref):
    conv = convb_ref[...] + xbc_ref[...] * convw_ref[CONV_W - 1:CONV_W, :]
    for k in range(CONV_W - 1):
        conv = conv + cst_ref[k] * convw_ref[k:k + 1, :]
    xc = _silu(conv)
    xs = xc[:, :D_INNER]
    dt = jax.nn.softplus(small_ref[...] + dtb_ref[...])
    dt_e = _expand_heads(dt)
    decay_e = jnp.exp(dt_e * _expand_heads(-jnp.exp(alog_ref[...])))
    xd = xs * dt_e
    fill = jnp.zeros((LANES - SSD_BT, D_INNER), F32)
    xd_t = jnp.concatenate([xd, fill], axis=0).T
    decay_t = jnp.concatenate([decay_e, fill], axis=0).T
    lane = lax.broadcasted_iota(jnp.int32, (1, D_INNER), 1)
    for i in range(SSD_BT):
        bsel = jnp.concatenate(
            [jnp.broadcast_to(xc[i:i + 1, D_INNER + g * D_STATE:D_INNER + (g + 1) * D_STATE], (GROUP_W, D_STATE))
             for g in range(N_GROUPS)], axis=0)
        h0 = h0_ref[i].reshape(D_INNER, D_STATE)
        hn = decay_t[:, i:i + 1] * h0 + xd_t[:, i:i + 1] * bsel
        hnew_ref[i] = hn.reshape(SSM_HEADS, SSM_HEAD_DIM, D_STATE)
        c8 = jnp.concatenate(
            [xc[i:i + 1, D_INNER + (N_GROUPS + g) * D_STATE:D_INNER + (N_GROUPS + g + 1) * D_STATE]
             for g in range(N_GROUPS)] + [jnp.zeros((HEAD_ROWS - N_GROUPS, D_STATE), F32)], axis=0)
        y8 = _dot_nt(c8.astype(BF16), hn.astype(BF16))
        ys_ref[i:i + 1, :] = jnp.where(lane < GROUP_W, y8[0:1], y8[1:2])
    y = ys_ref[...] + dskip_ref[...] * xs
    y_ref[...] = _grouped_norm_gate(y, z_ref[...], norm_ref[...])


def _ssd_sample(xbc, conv_state_t, z, small, h0, conv_w, conv_b, dt_bias, a_log, d_skip, ssm_norm):
    db = xbc.shape[0]
    assert db % SSD_BT == 0 and N_GROUPS == 2
    params = (conv_w, conv_b.reshape(1, CONV_DIM), _pad_small(dt_bias), _pad_small(a_log),
              jnp.repeat(d_skip.astype(F32), SSM_HEAD_DIM).reshape(1, D_INNER), ssm_norm.reshape(1, D_INNER))
    rows = lambda w: pl.BlockSpec((SSD_BT, w), lambda i: (i, 0))
    state_spec = pl.BlockSpec((SSD_BT, SSM_HEADS, SSM_HEAD_DIM, D_STATE), lambda i: (i, 0, 0, 0))
    return pl.pallas_call(
        _ssd_sample_kernel,
        grid=(db // SSD_BT,),
        in_specs=[rows(CONV_DIM), pl.BlockSpec((CONV_W - 1, SSD_BT, CONV_DIM), lambda i: (0, i, 0)), rows(D_INNER),
                  rows(SMALL_W), state_spec] + [_const_spec(p.shape) for p in params],
        out_specs=[rows(D_INNER), state_spec],
        out_shape=[jax.ShapeDtypeStruct((db, D_INNER), F32), jax.ShapeDtypeStruct(h0.shape, F32)],
        scratch_shapes=[pltpu.VMEM((SSD_BT, D_INNER), F32)],
        compiler_params=pltpu.CompilerParams(dimension_semantics=("arbitrary",), vmem_limit_bytes=VMEM_LIMIT),
        name="ssd_sample",
    )(xbc, conv_state_t, z, small, h0, *params)


PROMPT_TM = 512


def _ffn_weights(w_gate, w_up, w_down):
    chunked = lambda w: jnp.transpose(w.astype(BF16).reshape(D_MODEL, N_FF_CHUNKS, FF_CHUNK), (1, 0, 2))
    return chunked(w_gate), chunked(w_up), w_down.astype(BF16).reshape(N_FF_CHUNKS, FF_CHUNK, D_MODEL)


def _in_proj_weight(w_in):
    cuts = np.cumsum([ATT_DIM, KV_DIM, KV_DIM, KV_DIM, N_GATE, D_INNER, CONV_DIM]).tolist()
    q, kvc, kvs, kvw, g, z, xbc, dt = jnp.split(w_in, cuts, axis=-1)
    pad = jnp.zeros((D_MODEL, SMALL_W - N_GATE - SSM_HEADS), w_in.dtype)
    return jnp.concatenate([q, kvc, kvs, kvw, z, xbc, g, dt, pad], axis=-1).astype(BF16)


def kernel(x_prompt, x_sample, cache_cmp_kv, cache_slc_kv, cache_win_kv, state_conv, state_ssm, page_table,
           p_prompt, p_sample, ffn1_norm, ffn1_w_gate, ffn1_w_up, ffn1_w_down, mix_norm, w_in,
           cmp_w1, cmp_pe, cmp_b1, cmp_w2, conv_w, conv_b, dt_bias, a_log, d_skip, ssm_norm, w_out,
           ffn2_norm, ffn2_w_gate, ffn2_w_up, ffn2_w_down, ple_norm, w_ple_gate, w_ple, final_norm):
    b, l, _ = x_prompt.shape
    db, t_new, _ = x_sample.shape
    depth = ffn1_norm.shape[0]
    n_pool = cache_cmp_kv.shape[0]
    n_pages = page_table.shape[1]
    past = n_pages * PAGE_SIZE
    assert depth == 1 and t_new == 1 and past >= WINDOW and (b * l) % PROMPT_TM == 0
    i = 0
    row = lambda v: v.reshape(1, -1).astype(F32)

    ffn1 = _ffn_weights(ffn1_w_gate[i], ffn1_w_up[i], ffn1_w_down[i])
    stage1_w = (row(ffn1_norm[i]),) + ffn1 + (row(mix_norm[i]), _in_proj_weight(w_in[i]))
    cmp_w = _cmp_weights(cmp_w1[i], cmp_pe[i], cmp_b1[i], cmp_w2[i])
    ssm_w = (conv_w[i], conv_b[i], dt_bias[i], a_log[i], d_skip[i], ssm_norm[i])
    w_o = w_out[i].astype(BF16)
    stage3_w = (w_o[:ATT_DIM], w_o[ATT_DIM:], row(ffn2_norm[i])) + _ffn_weights(ffn2_w_gate[i], ffn2_w_up[i], ffn2_w_down[i]) + (
        row(ple_norm[i]), w_ple_gate[i].astype(BF16), w_ple[i].astype(BF16), row(final_norm))
    kv6 = lambda a, lead, rows: a.reshape(lead, 1, rows, 2, N_KV_HEADS, HEAD_DIM)

    h, q, kvc, kvs, kvw, z, xbc, small, kvc_t, kvs_t, kaug, vaug_t, kwin = _ffn_inproj(
        x_prompt.reshape(b * l, D_MODEL), *stage1_w, PROMPT_TM, cache_batch=b)
    per_b = lambda a: a.reshape(b, l, a.shape[-1])
    kcv, kcv_t = _cmp_prompt(per_b(kvc), *cmp_w)
    att = _nsa_prompt(per_b(q), per_b(small), kcv, kcv_t, kaug, vaug_t, per_b(kwin))
    ssm, ssm_state = _ssd_prompt(per_b(xbc), per_b(z), per_b(small), *ssm_w)
    y_prompt = _out_ffn(h, att.reshape(b * l, ATT_DIM), ssm.reshape(b * l, D_INNER),
                        p_prompt[i].reshape(b * l, PLE_DIM), stage3_w, PROMPT_TM).reshape(b, l, D_MODEL)
    keep = min(WINDOW, l)
    rows_major = lambda a: jnp.transpose(a, (0, 1, 5, 2, 3, 4))
    new_cmp_p = rows_major(kvc_t)
    new_slc_p = rows_major(kvs_t)
    new_win_p = kv6(per_b(kvw)[:, l - keep:], b, keep)
    new_conv_p = per_b(xbc)[:, l - (CONV_W - 1):].reshape(b, 1, CONV_W - 1, CONV_DIM)
    new_ssm_p = ssm_state.reshape(b, 1, SSM_HEADS, SSM_HEAD_DIM, D_STATE)

    hs, qs, kvc_s, kvs_s, kvw_s, z_s, xbc_s, small_s = _ffn_inproj(x_sample.reshape(db, D_MODEL), *stage1_w, db)
    q8 = qs.reshape(db, N_HEADS, HEAD_DIM)
    n_cmp_s = (past + t_new - CMP_LEN) // CMP_STRIDE + 1
    past_blk = past // SEL_BLOCK
    mselt_s = _cmp_to_sel_t(n_pages * BLOCKS_PER_PAGE, n_cmp_s, past_blk)
    row_minor = lambda a: jnp.transpose(a, (0, 1, 3, 4, 5, 2))
    o_cmp, imp = _cmp_sample(page_table, q8, row_minor(cache_cmp_kv), *cmp_w, mselt_s, n_cmp_s)
    n_query = db * N_KV_HEADS
    imp2 = jnp.pad(imp[:, :N_KV_HEADS].reshape(n_query, SEL_LANES), ((0, -n_query % LANES), (0, 0)))
    idx = _topk_sample(imp2, past_blk)
    att_s, new_win_t = _sel_win_sample(
        idx, page_table, q8, small_s[:, :N_GATE].reshape(db, N_HEADS, 3), o_cmp,
        kvs_s.reshape(db, 1, KV_DIM), kvw_s.reshape(db, 1, KV_DIM),
        row_minor(cache_win_kv), row_minor(cache_slc_kv))
    new_win_s = rows_major(new_win_t)
    ssm_s, ssm_state_s = _ssd_sample(xbc_s, jnp.transpose(state_conv[:, i], (1, 0, 2)), z_s, small_s,
                                     state_ssm[:, i], *ssm_w)
    y_sample = _out_ffn(hs, att_s.reshape(db, ATT_DIM), ssm_s, p_sample[i].reshape(db, PLE_DIM),
                        stage3_w, db).reshape(db, 1, D_MODEL)
    new_conv_s = jnp.concatenate([state_conv[:, i, 1:], xbc_s[:, None]], axis=1)[:, None]

    return (y_prompt, y_sample, new_cmp_p, new_slc_p, new_win_p, new_conv_p, new_ssm_p,
            kv6(kvc_s, db, 1), kv6(kvs_s, db, 1), new_win_s, new_conv_s, ssm_state_s[:, None])
```

```python
import functools

import jax
import jax.numpy as jnp
import numpy as np
from jax import lax
from jax.experimental import pallas as pl
from jax.experimental.pallas import tpu as pltpu

F32 = jnp.float32
BF16 = jnp.bfloat16

D_MODEL = 1024
N_HEADS = 8
N_KV_HEADS = 2
HEAD_DIM = 64
GQA = N_HEADS // N_KV_HEADS
ATT_DIM = N_HEADS * HEAD_DIM
KV_DIM = 2 * N_KV_HEADS * HEAD_DIM
CMP_LEN = 32
CMP_STRIDE = 16
CMP_HID = 128
SEL_BLOCK = 64
SEL_TOPK = 16
WINDOW = 512
Q_BLOCK = 128
SSM_HEADS = 8
SSM_HEAD_DIM = 64
D_INNER = SSM_HEADS * SSM_HEAD_DIM
N_GROUPS = 2
D_STATE = 128
CONV_W = 4
CONV_DIM = D_INNER + 2 * N_GROUPS * D_STATE
SSD_CHUNK = 128
D_FF = 2816
PLE_DIM = 256
PAGE_SIZE = 128
EPS = 1e-6

LANES = 128
FF_CHUNK = 256
N_FF_CHUNKS = D_FF // FF_CHUNK
SMALL_W = LANES
N_GATE = 3 * N_HEADS
IN_PROJ_PAD = ATT_DIM + 3 * KV_DIM + D_INNER + CONV_DIM + SMALL_W
VMEM_LIMIT = 56 * 1024 * 1024
NEG_BIG = -1e30
IMP_FORCED = 3e38
IMP_BLOCKED = -1e38
IMP_TAKEN = -3e38


def _dot(a, b):
    return jnp.dot(a, b, preferred_element_type=F32)


def _dot_nt(a, b):
    return lax.dot_general(a, b, (((1,), (1,)), ((), ())), preferred_element_type=F32)


def _rms(x, g):
    return x * lax.rsqrt(jnp.mean(x * x, axis=-1, keepdims=True) + EPS) * g


def _silu(x):
    return x * jax.nn.sigmoid(x)


def _const_spec(shape):
    nd = len(shape)
    return pl.BlockSpec(shape, lambda *_: (0,) * nd, pipeline_mode=pl.Buffered(1))


def _row_spec(tm, width):
    return pl.BlockSpec((tm, width), lambda i: (i, 0))


def _swiglu_half_step(x, g_ref, wg_ref, wu_ref, wd_ref, acc_ref):
    xn = _rms(x, g_ref[...]).astype(BF16)
    acc_ref[...] = jnp.zeros_like(acc_ref)

    def body(c, carry):
        a = _dot(xn, wg_ref[c])
        b = _dot(xn, wu_ref[c])
        hm = (_silu(a) * b).astype(BF16)
        acc_ref[...] += _dot(hm, wd_ref[c])
        return carry

    lax.fori_loop(0, N_FF_CHUNKS, body, 0, unroll=True)
    return x + 0.5 * acc_ref[...]


_IN_SEGS = (ATT_DIM, KV_DIM, KV_DIM, KV_DIM, D_INNER, CONV_DIM, SMALL_W)


SEL_TILE = 512
SEL_LANES = 128
AUG = SEL_LANES + 2 * HEAD_DIM
SEL_SHIFT = SEL_BLOCK.bit_length() - 1
VAUG_ROWS = HEAD_DIM + 16


def _ffn_inproj_kernel(x_ref, g1_ref, wg_ref, wu_ref, wd_ref, gm_ref, win_ref,
                       h_ref, q_ref, kvc_ref, kvs_ref, kvw_ref, z_ref, xbc_ref, small_ref, *rest, tiles):
    *prompt_refs, acc_ref = rest
    tm = x_ref.shape[0]
    h = _swiglu_half_step(x_ref[...], g1_ref, wg_ref, wu_ref, wd_ref, acc_ref)
    h_ref[...] = h
    hn = _rms(h, gm_ref[...]).astype(BF16)
    off = 0
    for out_ref, width in zip((q_ref, kvc_ref, kvs_ref, kvw_ref, z_ref, xbc_ref, small_ref), _IN_SEGS):
        out_ref[...] = _dot(hn, win_ref[:, off:off + width])
        off += width
    if not prompt_refs:
        return
    kvct_ref, kvst_ref, kaug_ref, vaugt_ref, kwin_ref = prompt_refs
    kvs = kvs_ref[...]
    kvc_t = kvc_ref[...].T
    kvs_t = kvs.T
    for ch in range(2 * N_KV_HEADS):
        rows = slice(ch * HEAD_DIM, (ch + 1) * HEAD_DIM)
        kvct_ref[0, 0, ch // N_KV_HEADS, ch % N_KV_HEADS] = kvc_t[rows, :]
        kvst_ref[0, 0, ch // N_KV_HEADS, ch % N_KV_HEADS] = kvs_t[rows, :]
    kwin_ref[...] = kvw_ref[...].astype(BF16)
    pos = (pl.program_id(0) % tiles) * tm + lax.broadcasted_iota(jnp.int32, (tm, 1), 0)
    onehot = ((pos >> SEL_SHIFT) == lax.broadcasted_iota(jnp.int32, (1, SEL_LANES), 1)).astype(BF16)
    ones_row = (lax.broadcasted_iota(jnp.int32, (VAUG_ROWS - HEAD_DIM, tm), 0) == 0).astype(BF16)
    for hh in range(N_KV_HEADS):
        kaug_ref[0, hh, :, 0:SEL_LANES] = onehot
        kaug_ref[0, hh, :, SEL_LANES:SEL_LANES + HEAD_DIM] = kvs[:, hh * HEAD_DIM:(hh + 1) * HEAD_DIM].astype(BF16)
        kaug_ref[0, hh, :, SEL_LANES + HEAD_DIM:] = jnp.zeros((tm, HEAD_DIM), BF16)
        vaugt_ref[0, hh, 0, 0:HEAD_DIM, :] = kvs_t[(N_KV_HEADS + hh) * HEAD_DIM:(N_KV_HEADS + hh + 1) * HEAD_DIM, :].astype(BF16)
        vaugt_ref[0, hh, 0, HEAD_DIM:, :] = ones_row


def _ffn_inproj(x, g1, wg, wu, wd, gm, win, tm, cache_batch=None):
    n = x.shape[0]
    outs = [jax.ShapeDtypeStruct((n, D_MODEL), F32)] + [jax.ShapeDtypeStruct((n, w), F32) for w in _IN_SEGS]
    out_specs = [_row_spec(tm, D_MODEL)] + [_row_spec(tm, w) for w in _IN_SEGS]
    tiles = 1
    if cache_batch is not None:
        l = n // cache_batch
        tiles = l // tm
        assert l % tm == 0 and tm == SEL_TILE
        outs += [jax.ShapeDtypeStruct((cache_batch, 1, 2, N_KV_HEADS, HEAD_DIM, l), F32)] * 2
        out_specs += [pl.BlockSpec((1, 1, 2, N_KV_HEADS, HEAD_DIM, tm),
                                   lambda i: (i // tiles, 0, 0, 0, 0, i % tiles))] * 2
        outs += [jax.ShapeDtypeStruct((cache_batch, N_KV_HEADS, l, AUG), BF16),
                 jax.ShapeDtypeStruct((cache_batch, N_KV_HEADS, tiles, VAUG_ROWS, tm), BF16),
                 jax.ShapeDtypeStruct((n, KV_DIM), BF16)]
        out_specs += [pl.BlockSpec((1, N_KV_HEADS, tm, AUG), lambda i: (i // tiles, 0, i % tiles, 0)),
                      pl.BlockSpec((1, N_KV_HEADS, 1, VAUG_ROWS, tm), lambda i: (i // tiles, 0, i % tiles, 0, 0)),
                      _row_spec(tm, KV_DIM)]
    return pl.pallas_call(
        functools.partial(_ffn_inproj_kernel, tiles=tiles),
        grid=(n // tm,),
        in_specs=[_row_spec(tm, D_MODEL), _const_spec(g1.shape), _const_spec(wg.shape), _const_spec(wu.shape),
                  _const_spec(wd.shape), _const_spec(gm.shape), _const_spec(win.shape)],
        out_specs=out_specs,
        out_shape=outs,
        scratch_shapes=[pltpu.VMEM((tm, D_MODEL), F32)],
        compiler_params=pltpu.CompilerParams(dimension_semantics=("arbitrary",), vmem_limit_bytes=VMEM_LIMIT),
        name="ffn1_inproj",
    )(x, g1, wg, wu, wd, gm, win)


HD2 = N_KV_HEADS * HEAD_DIM
CMP_R = CMP_LEN // CMP_STRIDE
CMP_PROJ = N_KV_HEADS * CMP_R * CMP_HID
PE_ROWS = 16
CMP_K = CMP_STRIDE * HD2


def _cmp_weights(cmp_w1, cmp_pe, cmp_b1, cmp_w2):
    w1r = cmp_w1.reshape(2, CMP_R, CMP_STRIDE, HEAD_DIM, CMP_HID)
    per = cmp_pe.reshape(2, CMP_R, CMP_STRIDE, HEAD_DIM)
    wbd = jnp.zeros((2, CMP_STRIDE, N_KV_HEADS, HEAD_DIM, N_KV_HEADS, CMP_R, CMP_HID), F32)
    blk = jnp.transpose(w1r, (0, 2, 3, 1, 4))
    for h in range(N_KV_HEADS):
        wbd = wbd.at[:, :, h, :, h, :, :].set(blk)
    wbd = wbd.reshape(2, CMP_K, CMP_PROJ).astype(BF16)
    pe_rows = jnp.broadcast_to(per[:, :, :, None, :], (2, CMP_R, CMP_STRIDE, N_KV_HEADS, HEAD_DIM))
    pe_rows = jnp.concatenate([pe_rows.reshape(2, CMP_R, CMP_K), jnp.zeros((2, PE_ROWS - CMP_R, CMP_K), F32)], axis=1)
    return wbd, pe_rows.astype(BF16), cmp_b1, cmp_w2.astype(BF16)


def _cmp_tokens(load_rows, nb, lhs_ref, wbd_ref, pe_ref, b1_ref, w2_ref):
    outs = []
    for c in range(2):
        for s in range(CMP_STRIDE):
            lhs_ref[0:nb, s * HD2:(s + 1) * HD2] = load_rows(c, s).astype(BF16)
        lhs_ref[nb:nb + PE_ROWS, :] = pe_ref[c]
        acc = _dot(lhs_ref[...], wbd_ref[c])
        proj, pe_proj = acc[:nb], acc[nb:]
        for h in range(N_KV_HEADS):
            lo = h * CMP_R * CMP_HID
            p0 = proj[:, lo:lo + CMP_HID]
            p1 = pltpu.roll(proj[:, lo + CMP_HID:lo + 2 * CMP_HID], nb - 1, 0)
            pe_add = pe_proj[0:1, lo:lo + CMP_HID] + pe_proj[1:2, lo + CMP_HID:lo + 2 * CMP_HID]
            hid = _silu(p0 + p1 + pe_add + b1_ref[c:c + 1, :])
            outs.append(_dot(hid.astype(BF16), w2_ref[c]))
    return jnp.concatenate(outs, axis=1)


def _cmp_prompt_kernel(xk_ref, xv_ref, wbd_ref, pe_ref, b1_ref, w2_ref, out_ref, out_t_ref, lhs_ref):
    nb = out_ref.shape[1]
    load_rows = lambda c, s: (xk_ref, xv_ref)[c][0, pl.ds(s, nb, stride=CMP_STRIDE), :]
    tokens = _cmp_tokens(load_rows, nb, lhs_ref, wbd_ref, pe_ref, b1_ref, w2_ref)
    out_ref[0] = tokens
    out_t_ref[0] = tokens.T


def _cmp_prompt(kvc, wbd, pe_rows, b1, w2):
    b, l, _ = kvc.shape
    nb = l // CMP_STRIDE
    return pl.pallas_call(
        _cmp_prompt_kernel,
        grid=(b,),
        in_specs=[pl.BlockSpec((1, l, HD2), lambda i: (i, 0, 0)), pl.BlockSpec((1, l, HD2), lambda i: (i, 0, 1)),
                  _const_spec(wbd.shape), _const_spec(pe_rows.shape), _const_spec(b1.shape), _const_spec(w2.shape)],
        out_specs=[pl.BlockSpec((1, nb, KV_DIM), lambda i: (i, 0, 0)), pl.BlockSpec((1, KV_DIM, nb), lambda i: (i, 0, 0))],
        out_shape=[jax.ShapeDtypeStruct((b, nb, KV_DIM), F32), jax.ShapeDtypeStruct((b, KV_DIM, nb), F32)],
        scratch_shapes=[pltpu.VMEM((nb + PE_ROWS, CMP_K), BF16)],
        compiler_params=pltpu.CompilerParams(dimension_semantics=("arbitrary",), vmem_limit_bytes=VMEM_LIMIT),
        name="nsa_cmp_prompt",
    )(kvc, kvc, wbd, pe_rows, b1, w2)


COLS = GQA * Q_BLOCK
TINY = float(np.finfo(np.float32).tiny)
LOG2E = float(np.log2(np.e))


def _softmax_rows(s):
    m = jnp.max(s, axis=-1, keepdims=True)
    m = jnp.where(m > 0.5 * NEG_BIG, m, 0.0)
    e = jnp.exp(s - m)
    return e / jnp.maximum(jnp.sum(e, axis=-1, keepdims=True), TINY)


def _select_blocks(imp_t, t_lane):
    jj = lax.broadcasted_iota(jnp.int32, imp_t.shape, 0)
    cb = t_lane >> SEL_SHIFT
    forced = (jj == 0) | (jj == cb) | (jj == cb - 1)
    causal = (jj << SEL_SHIFT) <= t_lane
    x = jnp.where(forced, IMP_FORCED, imp_t)
    x = jnp.where(causal, x, IMP_BLOCKED)
    sel, _ = _take_top(x, SEL_TOPK)
    return sel


def _take_top(x, k):
    nj = x.shape[0]
    jf = lax.broadcasted_iota(jnp.int32, x.shape, 0).astype(F32)
    sel = jnp.zeros_like(x)
    picks = []
    for _ in range(k):
        m = jnp.max(x, axis=0, keepdims=True)
        first = jnp.min(jnp.where(x == m, jf, float(nj)), axis=0, keepdims=True)
        hit = jf == first
        sel = jnp.where(hit, 1.0, sel)
        x = jnp.where(hit, IMP_TAKEN, x)
        picks.append(first)
    return sel, picks


def _softmax_cols(s):
    m = jnp.max(s, axis=0, keepdims=True)
    m = jnp.where(m > 0.5 * NEG_BIG, m, 0.0)
    e = jnp.exp2(s - m)
    return e, 1.0 / jnp.maximum(jnp.sum(e, axis=0, keepdims=True), TINY)


def _dot_tn(a, b):
    return lax.dot_general(a, b, (((0,), (0,)), ((), ())), preferred_element_type=F32)


def _nsa_prompt_kernel(q_ref, small_ref, kcv_ref, kcvt_ref, kaug_ref, vaugt_ref, kwin_ref, mselt_ref, out_ref,
                       lt_ref, acc_ref, ot_ref, sa_ref, sb_ref):
    n = pl.program_id(1)
    t0 = n * Q_BLOCK
    nb = kcv_ref.shape[1]
    col_t = t0 + (lax.broadcasted_iota(jnp.int32, (1, COLS), 1) & (Q_BLOCK - 1))
    lane_t = t0 + lax.broadcasted_iota(jnp.int32, (1, Q_BLOCK), 1)
    gates_t = jax.nn.sigmoid(small_ref[0]).T
    q_t = (q_ref[0] * (HEAD_DIM ** -0.5 * LOG2E)).T.astype(BF16)
    c_end = lax.broadcasted_iota(jnp.int32, (nb, 1), 0) * CMP_STRIDE + (CMP_LEN - 1)
    w0 = pl.multiple_of(jnp.maximum(t0 - WINDOW, 0), Q_BLOCK)
    kpos = w0 + lax.broadcasted_iota(jnp.int32, (WINDOW + Q_BLOCK, 1), 0)
    dpos = col_t - kpos
    win_ok = (dpos >= 0) & (dpos < WINDOW)

    for h in range(N_KV_HEADS):
        for g in range(GQA):
            lo = (h * GQA + g) * HEAD_DIM
            lt_ref[h, SEL_LANES:SEL_LANES + HEAD_DIM, g * Q_BLOCK:(g + 1) * Q_BLOCK] = q_t[lo:lo + HEAD_DIM, :]
        lt_ref[h, SEL_LANES + HEAD_DIM:, :] = jnp.zeros((HEAD_DIM, COLS), BF16)
        qh_t = lt_ref[h, SEL_LANES:SEL_LANES + HEAD_DIM, :]

        kc = kcv_ref[0, :, h * HEAD_DIM:(h + 1) * HEAD_DIM].astype(BF16)
        vc_t = kcvt_ref[0, (N_KV_HEADS + h) * HEAD_DIM:(N_KV_HEADS + h + 1) * HEAD_DIM, :].astype(BF16)
        e_t, r_t = _softmax_cols(jnp.where(c_end <= col_t, _dot(kc, qh_t), NEG_BIG))
        o_cmp_t = _dot(vc_t, e_t.astype(BF16)) * r_t

        psum = e_t[:, 0:Q_BLOCK] * r_t[:, 0:Q_BLOCK]
        for g in range(1, GQA):
            psum = psum + e_t[:, g * Q_BLOCK:(g + 1) * Q_BLOCK] * r_t[:, g * Q_BLOCK:(g + 1) * Q_BLOCK]
        p_hi = psum.astype(BF16)
        p_lo = (psum - p_hi.astype(F32)).astype(BF16)
        imp_t = _dot(mselt_ref[...], p_hi) + _dot(mselt_ref[...], p_lo)
        neg = jnp.where(_select_blocks(imp_t, lane_t) > 0.0, 0.0, NEG_BIG).astype(BF16)
        for g in range(GQA):
            lt_ref[h, 0:SEL_LANES, g * Q_BLOCK:(g + 1) * Q_BLOCK] = neg

        kw = kwin_ref[0, pl.ds(w0, WINDOW + Q_BLOCK), h * HEAD_DIM:(h + 1) * HEAD_DIM]
        vw = kwin_ref[0, pl.ds(w0, WINDOW + Q_BLOCK), (N_KV_HEADS + h) * HEAD_DIM:(N_KV_HEADS + h + 1) * HEAD_DIM]
        e_t, r_t = _softmax_cols(jnp.where(win_ok, _dot(kw, qh_t), NEG_BIG))
        o_win_t = _dot_tn(vw, e_t.astype(BF16)) * r_t
        for g in range(GQA):
            hd = h * GQA + g
            cols = slice(g * Q_BLOCK, (g + 1) * Q_BLOCK)
            ot_ref[hd * HEAD_DIM:(hd + 1) * HEAD_DIM, :] = (
                gates_t[3 * hd:3 * hd + 1, :] * o_cmp_t[:, cols] + gates_t[3 * hd + 2:3 * hd + 3, :] * o_win_t[:, cols])

    acc_ref[...] = jnp.zeros(acc_ref.shape, F32)
    n_pairs = (t0 + Q_BLOCK + 2 * SEL_TILE - 1) // (2 * SEL_TILE)

    def scores(s_ref, kt):
        k0 = pl.multiple_of(kt * SEL_TILE, SEL_TILE)
        for h in range(N_KV_HEADS):
            s_ref[h] = _dot(kaug_ref[0, h, pl.ds(k0, SEL_TILE), :], lt_ref[h])

    def consume(s_ref, kt, ms, causal_mask):
        out = []
        for h in range(N_KV_HEADS):
            s = s_ref[h]
            if causal_mask:
                kpos = kt * SEL_TILE + lax.broadcasted_iota(jnp.int32, (SEL_TILE, 1), 0)
                s = jnp.where(kpos <= col_t, s, NEG_BIG)
            m_new = jnp.maximum(ms[h], jnp.max(s, axis=0, keepdims=True))
            pe = jnp.exp2(s - m_new).astype(BF16)
            acc_ref[h] = jnp.exp2(ms[h] - m_new) * acc_ref[h] + _dot(vaugt_ref[0, h, kt], pe)
            out.append(m_new)
        return tuple(out)

    def pair(i, ms):
        scores(sb_ref, 2 * i + 1)
        ms = consume(sa_ref, 2 * i, ms, False)
        scores(sa_ref, 2 * i + 2)
        return consume(sb_ref, 2 * i + 1, ms, False)

    scores(sa_ref, 0)
    m_init = (jnp.full((1, COLS), NEG_BIG, F32),) * N_KV_HEADS
    ms = lax.fori_loop(0, n_pairs - 1, pair, m_init)
    last = 2 * (n_pairs - 1)
    two_tiles = t0 + Q_BLOCK > (last + 1) * SEL_TILE

    @pl.when(two_tiles)
    def _():
        scores(sb_ref, last + 1)
        consume(sb_ref, last + 1, consume(sa_ref, last, ms, True), True)

    @pl.when(jnp.logical_not(two_tiles))
    def _():
        consume(sa_ref, last, ms, True)

    for h in range(N_KV_HEADS):
        acc = acc_ref[h]
        o_sel_t = acc[:HEAD_DIM] / jnp.maximum(acc[HEAD_DIM:HEAD_DIM + 1], TINY)
        for g in range(GQA):
            hd = h * GQA + g
            cols = slice(g * Q_BLOCK, (g + 1) * Q_BLOCK)
            ot_ref[hd * HEAD_DIM:(hd + 1) * HEAD_DIM, :] += gates_t[3 * hd + 1:3 * hd + 2, :] * o_sel_t[:, cols]
    out_ref[0] = ot_ref[...].T


def _cmp_to_sel_t(nb, n_cmp, n_blk):
    cs = np.arange(nb)[None, :] * CMP_STRIDE
    bs = np.arange(SEL_LANES)[:, None] * SEL_BLOCK
    ov = np.clip(np.minimum(cs + CMP_LEN, bs + SEL_BLOCK) - np.maximum(cs, bs), 0, None) / CMP_LEN
    ov = ov * (np.arange(nb)[None, :] < n_cmp) * (np.arange(SEL_LANES)[:, None] < n_blk)
    return jnp.asarray(ov, BF16)


def _nsa_prompt(q, small, kcv, kcv_t, kaug, vaug_t, kwin):
    b, l, _ = q.shape
    nb = kcv.shape[1]
    n_blk = l // SEL_BLOCK
    assert l % (2 * SEL_TILE) == 0 and n_blk <= SEL_LANES and l >= WINDOW + Q_BLOCK
    mselt = _cmp_to_sel_t(nb, (l - CMP_LEN) // CMP_STRIDE + 1, n_blk)

    def per_batch(shape):
        nd = len(shape)
        return pl.BlockSpec((1,) + shape[1:], lambda i, j: (i,) + (0,) * (nd - 1), pipeline_mode=pl.Buffered(1))

    return pl.pallas_call(
        _nsa_prompt_kernel,
        grid=(b, l // Q_BLOCK),
        in_specs=[pl.BlockSpec((1, Q_BLOCK, ATT_DIM), lambda i, j: (i, j, 0)),
                  pl.BlockSpec((1, Q_BLOCK, SMALL_W), lambda i, j: (i, j, 0)),
                  per_batch(kcv.shape), per_batch(kcv_t.shape), per_batch(kaug.shape), per_batch(vaug_t.shape),
                  per_batch(kwin.shape), pl.BlockSpec(mselt.shape, lambda i, j: (0, 0), pipeline_mode=pl.Buffered(1))],
        out_specs=pl.BlockSpec((1, Q_BLOCK, ATT_DIM), lambda i, j: (i, j, 0)),
        out_shape=jax.ShapeDtypeStruct((b, l, ATT_DIM), F32),
        scratch_shapes=[pltpu.VMEM((N_KV_HEADS, AUG, COLS), BF16), pltpu.VMEM((N_KV_HEADS, VAUG_ROWS, COLS), F32),
                        pltpu.VMEM((ATT_DIM, Q_BLOCK), F32), pltpu.VMEM((N_KV_HEADS, SEL_TILE, COLS), F32),
                        pltpu.VMEM((N_KV_HEADS, SEL_TILE, COLS), F32)],
        compiler_params=pltpu.CompilerParams(dimension_semantics=("arbitrary", "arbitrary"),
                                             vmem_limit_bytes=VMEM_LIMIT),
        name="nsa_prompt",
    )(q, small, kcv, kcv_t, kaug, vaug_t, kwin, mselt)


DT_LANE = N_GATE
HEADS_PER_GROUP = SSM_HEADS // N_GROUPS
GROUP_W = D_INNER // N_GROUPS
TAIL = 8


def _expand_heads(v):
    rows = v.shape[0]
    lane = lax.broadcasted_iota(jnp.int32, (rows, LANES), 1)
    tiles = []
    for j in range(D_INNER // LANES):
        a = jnp.broadcast_to(v[:, DT_LANE + 2 * j:DT_LANE + 2 * j + 1], (rows, LANES))
        b = jnp.broadcast_to(v[:, DT_LANE + 2 * j + 1:DT_LANE + 2 * j + 2], (rows, LANES))
        tiles.append(jnp.where(lane < SSM_HEAD_DIM, a, b))
    return jnp.concatenate(tiles, axis=1)


def _cumsum_rows(x):
    n = x.shape[0]
    row = lax.broadcasted_iota(jnp.int32, x.shape, 0)
    s = 1
    while s < n:
        x = x + jnp.where(row >= s, pltpu.roll(x, s, 0), 0.0)
        s *= 2
    return x


def _grouped_norm_gate(y, z, norm):
    v = y * _silu(z)
    outs = []
    for g in range(N_GROUPS):
        vg = v[:, g * GROUP_W:(g + 1) * GROUP_W]
        outs.append(vg * lax.rsqrt(jnp.mean(vg * vg, axis=-1, keepdims=True) + EPS))
    return jnp.concatenate(outs, axis=1) * norm


def _ssd_prompt_kernel(xbc_ref, z_ref, small_ref, convw_ref, convb_ref, dtb_ref, alog_ref, dskip_ref, norm_ref,
                       y_ref, state_ref, xe_ref, st_ref, yd_ref):
    c = pl.program_id(1)
    lc = xbc_ref.shape[1]

    @pl.when(c == 0)
    def _():
        xe_ref[0:TAIL, :] = jnp.zeros((TAIL, CONV_DIM), F32)
        st_ref[...] = jnp.zeros(st_ref.shape, F32)

    xe_ref[TAIL:TAIL + lc, :] = xbc_ref[0]
    conv = convb_ref[...] + xe_ref[TAIL:TAIL + lc, :] * convw_ref[CONV_W - 1:CONV_W, :]
    for k in range(CONV_W - 1):
        conv = conv + xe_ref[pl.ds(TAIL - (CONV_W - 1) + k, lc), :] * convw_ref[k:k + 1, :]
    xe_ref[0:TAIL, :] = xe_ref[lc:lc + TAIL, :]
    xc = _silu(conv)
    xs = xc[:, :D_INNER]

    dt = jax.nn.softplus(small_ref[0] + dtb_ref[...])
    ad = dt * (-jnp.exp(alog_ref[...]))
    acs = _cumsum_rows(ad)
    acs_t = acs.T
    dt_e = _expand_heads(dt)
    acs_e = _expand_heads(acs)
    last_e = acs_e[lc - 1:lc, :]
    xd = xs * dt_e
    xd_bf = xd.astype(BF16)
    xdd_bf = (xd * jnp.exp(last_e - acs_e)).astype(BF16)
    grow = jnp.exp(acs_e)
    li = lax.broadcasted_iota(jnp.int32, (lc, lc), 0)
    si = lax.broadcasted_iota(jnp.int32, (lc, lc), 1)

    y_off = []
    for g in range(N_GROUPS):
        bm = xc[:, D_INNER + g * D_STATE:D_INNER + (g + 1) * D_STATE]
        cm = xc[:, D_INNER + N_GROUPS * D_STATE + g * D_STATE:D_INNER + N_GROUPS * D_STATE + (g + 1) * D_STATE]
        bm_bf = bm.astype(BF16)
        cm_bf = cm.astype(BF16)
        cb = _dot_nt(cm_bf, bm_bf)
        cols = slice(g * GROUP_W, (g + 1) * GROUP_W)
        st_g = st_ref[:, cols]
        y_off.append(_dot(cm_bf, st_g.astype(BF16)))
        for hh in range(HEADS_PER_GROUP):
            h = g * HEADS_PER_GROUP + hh
            seg = acs[:, DT_LANE + h:DT_LANE + h + 1] - acs_t[DT_LANE + h:DT_LANE + h + 1, :]
            m = jnp.where(li >= si, cb * jnp.exp(seg), 0.0).astype(BF16)
            yd_ref[:, h * SSM_HEAD_DIM:(h + 1) * SSM_HEAD_DIM] = _dot(m, xd_bf[:, h * SSM_HEAD_DIM:(h + 1) * SSM_HEAD_DIM])
        st_ref[:, cols] = st_g * jnp.exp(last_e[:, cols]) + _dot(bm.T.astype(BF16), xdd_bf[:, cols])

    y = yd_ref[...] + jnp.concatenate(y_off, axis=1) * grow + dskip_ref[...] * xs
    y_ref[0] = _grouped_norm_gate(y, z_ref[0], norm_ref[...])

    @pl.when(c == pl.num_programs(1) - 1)
    def _():
        state_ref[0] = st_ref[...].T


def _pad_small(v):
    return jnp.zeros((1, SMALL_W), F32).at[0, DT_LANE:DT_LANE + SSM_HEADS].set(v.astype(F32))


def _ssd_prompt(xbc, z, small, conv_w, conv_b, dt_bias, a_log, d_skip, ssm_norm):
    b, l, _ = xbc.shape
    lc = min(SSD_CHUNK, l)
    params = (conv_w, conv_b.reshape(1, CONV_DIM), _pad_small(dt_bias), _pad_small(a_log),
              jnp.repeat(d_skip.astype(F32), SSM_HEAD_DIM).reshape(1, D_INNER), ssm_norm.reshape(1, D_INNER))
    tile = lambda w: pl.BlockSpec((1, lc, w), lambda i, j: (i, j, 0))
    return pl.pallas_call(
        _ssd_prompt_kernel,
        grid=(b, l // lc),
        in_specs=[tile(CONV_DIM), tile(D_INNER), tile(SMALL_W)] + [_const_spec(p.shape) for p in params],
        out_specs=[tile(D_INNER), pl.BlockSpec((1, D_INNER, D_STATE), lambda i, j: (i, 0, 0))],
        out_shape=[jax.ShapeDtypeStruct((b, l, D_INNER), F32), jax.ShapeDtypeStruct((b, D_INNER, D_STATE), F32)],
        scratch_shapes=[pltpu.VMEM((lc + TAIL, CONV_DIM), F32), pltpu.VMEM((D_STATE, D_INNER), F32),
                        pltpu.VMEM((lc, D_INNER), F32)],
        compiler_params=pltpu.CompilerParams(dimension_semantics=("arbitrary", "arbitrary"),
                                             vmem_limit_bytes=VMEM_LIMIT),
        name="ssd_prompt",
    )(xbc, z, small, *params)


def _out_ffn_kernel(h_ref, att_ref, ssm_ref, p_ref, woa_ref, wos_ref, g2_ref, wg_ref, wu_ref, wd_ref,
                    gp_ref, wpg_ref, wple_ref, gf_ref, y_ref, acc_ref):
    h = h_ref[...] + _dot(att_ref[...].astype(BF16), woa_ref[...]) + _dot(ssm_ref[...].astype(BF16), wos_ref[...])
    h = _swiglu_half_step(h, g2_ref, wg_ref, wu_ref, wd_ref, acc_ref)
    gate = jax.nn.sigmoid(_dot(_rms(h, gp_ref[...]).astype(BF16), wpg_ref[...]))
    h = h + gate * _dot(p_ref[...].astype(BF16), wple_ref[...])
    y_ref[...] = _rms(h, gf_ref[...])


def _out_ffn(h, att, ssm, p, weights, tm):
    n = h.shape[0]
    return pl.pallas_call(
        _out_ffn_kernel,
        grid=(n // tm,),
        in_specs=[_row_spec(tm, D_MODEL), _row_spec(tm, ATT_DIM), _row_spec(tm, D_INNER), _row_spec(tm, PLE_DIM)]
        + [_const_spec(w.shape) for w in weights],
        out_specs=_row_spec(tm, D_MODEL),
        out_shape=jax.ShapeDtypeStruct((n, D_MODEL), F32),
        scratch_shapes=[pltpu.VMEM((tm, D_MODEL), F32)],
        compiler_params=pltpu.CompilerParams(dimension_semantics=("arbitrary",), vmem_limit_bytes=VMEM_LIMIT),
        name="outproj_ffn2_ple",
    )(h, att, ssm, p, *weights)


BLOCKS_PER_PAGE = PAGE_SIZE // CMP_STRIDE
HEAD_ROWS = 8


def _head_rows_of(h):
    row = lax.broadcasted_iota(jnp.int32, (HEAD_ROWS, 1), 0)
    return (row >= h * GQA) & (row < (h + 1) * GQA)


def _cmp_sample_kernel(pt_ref, q_ref, cache_ref, wbd_ref, pe_ref, b1_ref, w2_ref, mselt_ref, perm_ref,
                       ocmp_ref, imp_ref, xbuf_ref, xrow_ref, lhs_ref, sem_ref, *, n_cmp):
    b = pl.program_id(0)
    nb_total = pl.num_programs(0)
    n_pages = pt_ref.shape[1]
    nbp = n_pages * BLOCKS_PER_PAGE

    def page_copy(bb, slot, i):
        return pltpu.make_async_copy(cache_ref.at[pt_ref[bb, i], 0], xbuf_ref.at[slot, i], sem_ref.at[slot])

    def start_all(bb, slot):
        def body(i2, carry):
            for prio in range(2):
                page_copy(bb, slot, 2 * i2 + prio).start(priority=prio)
            return carry
        lax.fori_loop(0, n_pages // 2, body, 0)

    def wait_all(bb, slot):
        def body(i, carry):
            page_copy(bb, slot, i).wait()
            return carry
        lax.fori_loop(0, n_pages, body, 0)

    slot = b % 2

    @pl.when(b == 0)
    def _():
        start_all(0, 0)

    @pl.when(b + 1 < nb_total)
    def _():
        start_all(b + 1, 1 - slot)

    wait_all(b, slot)

    def load_rows(c, s):
        if c == 0 and s == 0:
            for i in range(n_pages):
                page = xbuf_ref[slot, i].reshape(KV_DIM, PAGE_SIZE).astype(BF16)
                rows = _dot_nt(perm_ref[...], page)
                for cc in range(2):
                    xrow_ref[cc, i] = rows[:, cc * HD2:(cc + 1) * HD2].reshape(CMP_STRIDE, BLOCKS_PER_PAGE, HD2)
        return xrow_ref[c, :, s].reshape(nbp, HD2)

    kcv = _cmp_tokens(load_rows, nbp, lhs_ref, wbd_ref, pe_ref, b1_ref, w2_ref)
    q8 = (q_ref[0] * (HEAD_DIM ** -0.5)).astype(BF16)
    valid = lax.broadcasted_iota(jnp.int32, (1, nbp), 1) < n_cmp
    o_cmp = jnp.zeros((HEAD_ROWS, HEAD_DIM), F32)
    psum = jnp.zeros((HEAD_ROWS, nbp), F32)
    row = lax.broadcasted_iota(jnp.int32, (HEAD_ROWS, 1), 0)
    for h in range(N_KV_HEADS):
        kc = kcv[:, h * HEAD_DIM:(h + 1) * HEAD_DIM].astype(BF16)
        vc = kcv[:, (N_KV_HEADS + h) * HEAD_DIM:(N_KV_HEADS + h + 1) * HEAD_DIM].astype(BF16)
        p = _softmax_rows(jnp.where(valid, _dot_nt(q8, kc), NEG_BIG))
        mine = _head_rows_of(h)
        o_cmp = jnp.where(mine, _dot(p.astype(BF16), vc), o_cmp)
        ph = jnp.sum(jnp.where(mine, p, 0.0), axis=0, keepdims=True)
        psum = jnp.where(row == h, ph, psum)
    ocmp_ref[0] = o_cmp
    p_hi = psum.astype(BF16)
    p_lo = (psum - p_hi.astype(F32)).astype(BF16)
    imp_ref[0] = _dot_nt(p_hi, mselt_ref[...]) + _dot_nt(p_lo, mselt_ref[...])


def _cmp_sample(page_table, q8, cache_t, wbd, pe_rows, b1, w2, mselt, n_cmp):
    db, n_pages = page_table.shape
    assert n_pages % 2 == 0
    r = np.arange(PAGE_SIZE)
    perm = jnp.asarray(r[:, None] == (r % CMP_STRIDE * BLOCKS_PER_PAGE + r // CMP_STRIDE)[None, :], BF16)
    consts = (wbd, pe_rows, b1, w2, mselt, perm)
    grid_spec = pltpu.PrefetchScalarGridSpec(
        num_scalar_prefetch=1,
        grid=(db,),
        in_specs=[pl.BlockSpec((1, HEAD_ROWS, HEAD_DIM), lambda i, pt: (i, 0, 0)),
                  pl.BlockSpec(memory_space=pl.ANY)]
        + [pl.BlockSpec(c.shape, lambda i, pt, nd=c.ndim: (0,) * nd, pipeline_mode=pl.Buffered(1)) for c in consts],
        out_specs=[pl.BlockSpec((1, HEAD_ROWS, HEAD_DIM), lambda i, pt: (i, 0, 0)),
                   pl.BlockSpec((1, HEAD_ROWS, SEL_LANES), lambda i, pt: (i, 0, 0))],
        scratch_shapes=[pltpu.VMEM((2, n_pages, 2, N_KV_HEADS, HEAD_DIM, PAGE_SIZE), F32),
                        pltpu.VMEM((2, n_pages, CMP_STRIDE, BLOCKS_PER_PAGE, HD2), F32),
                        pltpu.VMEM((n_pages * BLOCKS_PER_PAGE + PE_ROWS, CMP_K), BF16), pltpu.SemaphoreType.DMA((2,))],
    )
    return pl.pallas_call(
        functools.partial(_cmp_sample_kernel, n_cmp=n_cmp),
        grid_spec=grid_spec,
        out_shape=[jax.ShapeDtypeStruct((db, HEAD_ROWS, HEAD_DIM), F32),
                   jax.ShapeDtypeStruct((db, HEAD_ROWS, SEL_LANES), F32)],
        compiler_params=pltpu.CompilerParams(dimension_semantics=("arbitrary",), vmem_limit_bytes=VMEM_LIMIT),
        name="nsa_cmp_sample",
    )(page_table, q8, cache_t, *consts)


N_PAST_PICKS = SEL_TOPK - 1


def _topk_sample_kernel(imp_ref, idx_ref, *, past_blk):
    x = imp_ref[...].T
    jj = lax.broadcasted_iota(jnp.int32, x.shape, 0)
    x = jnp.where((jj == 0) | (jj == past_blk - 1), IMP_FORCED, x)
    x = jnp.where(jj < past_blk, x, IMP_BLOCKED)
    _, picks = _take_top(x, N_PAST_PICKS)
    picks = picks + [jnp.zeros_like(picks[0])] * (idx_ref.shape[0] - N_PAST_PICKS)
    idx_ref[...] = jnp.concatenate(picks, axis=0).astype(jnp.int32)


def _topk_sample(imp2, past_blk):
    nq = imp2.shape[0]
    assert N_PAST_PICKS <= past_blk <= SEL_LANES
    return pl.pallas_call(
        functools.partial(_topk_sample_kernel, past_blk=past_blk),
        out_shape=jax.ShapeDtypeStruct((SEL_TOPK, nq), jnp.int32),
        name="nsa_topk_sample",
    )(imp2)


SUB_PER_PAGE = PAGE_SIZE // SEL_BLOCK
SUB_SHIFT = SUB_PER_PAGE.bit_length() - 1
SEL_SHIFT_IN_PAGE = SEL_BLOCK.bit_length() - 1
N_SEL_COLS = N_PAST_PICKS * PAGE_SIZE


def _attend_one_token(q8, q8f, kt, vt, valid, k_new, v_new):
    s = jnp.where(valid, _dot(q8, kt.astype(BF16)), NEG_BIG)
    s_new = jnp.sum(q8f * k_new, axis=-1, keepdims=True)
    m = jnp.maximum(jnp.max(s, axis=-1, keepdims=True), s_new)
    e = jnp.exp(s - m)
    e_new = jnp.exp(s_new - m)
    norm = jnp.maximum(jnp.sum(e, axis=-1, keepdims=True) + e_new, TINY)
    return (_dot_nt(e.astype(BF16), vt.astype(BF16)) + e_new * v_new) / norm


def _sel_win_sample_kernel(idx_ref, pt_ref, q_ref, gate_ref, ocmp_ref, kvs_ref, kvw_ref, win_ref, slc_ref,
                           att_ref, newwin_ref, gbuf_ref, sem_ref):
    b = pl.program_id(0)
    nb_total = pl.num_programs(0)
    w_buf = win_ref.shape[-1]

    def block_copies(bb, slot, h, k):
        j = idx_ref[k, bb * N_KV_HEADS + h]
        page = pt_ref[bb, j >> SUB_SHIFT]
        return [pltpu.make_async_copy(slc_ref.at[page, 0, c, h],
                                      gbuf_ref.at[slot, h, c, :, pl.ds(k * PAGE_SIZE, PAGE_SIZE)],
                                      sem_ref.at[slot]) for c in range(2)]

    def start_all(bb, slot):
        for h in range(N_KV_HEADS):
            for k in range(N_PAST_PICKS):
                for cp in block_copies(bb, slot, h, k):
                    cp.start()

    def wait_all(bb, slot):
        for h in range(N_KV_HEADS):
            for k in range(N_PAST_PICKS):
                for cp in block_copies(bb, slot, h, k):
                    cp.wait()

    slot = b % 2

    @pl.when(b == 0)
    def _():
        start_all(0, 0)

    @pl.when(b + 1 < nb_total)
    def _():
        start_all(b + 1, 1 - slot)

    wait_all(b, slot)

    q8f = q_ref[0] * (HEAD_DIM ** -0.5)
    q8 = q8f.astype(BF16)
    page_lane = lax.broadcasted_iota(jnp.int32, (1, PAGE_SIZE), 1)
    win_lane = lax.broadcasted_iota(jnp.int32, (1, w_buf), 1)
    win_valid = w_buf - win_lane < WINDOW
    o_sel = jnp.zeros((HEAD_ROWS, HEAD_DIM), F32)
    o_win = jnp.zeros((HEAD_ROWS, HEAD_DIM), F32)
    for h in range(N_KV_HEADS):
        kcols = slice(h * HEAD_DIM, (h + 1) * HEAD_DIM)
        vcols = slice((N_KV_HEADS + h) * HEAD_DIM, (N_KV_HEADS + h + 1) * HEAD_DIM)
        mine = _head_rows_of(h)
        sel_valid = jnp.concatenate(
            [(page_lane >> SEL_SHIFT_IN_PAGE) == (idx_ref[k, b * N_KV_HEADS + h] & (SUB_PER_PAGE - 1))
             for k in range(N_PAST_PICKS)], axis=1)
        o_sel = jnp.where(mine, _attend_one_token(q8, q8f, gbuf_ref[slot, h, 0], gbuf_ref[slot, h, 1], sel_valid,
                                                  kvs_ref[0][:, kcols], kvs_ref[0][:, vcols]), o_sel)
        o_win = jnp.where(mine, _attend_one_token(q8, q8f, win_ref[0, 0, 0, h], win_ref[0, 0, 1, h], win_valid,
                                                  kvw_ref[0][:, kcols], kvw_ref[0][:, vcols]), o_win)
    gates = jax.nn.sigmoid(gate_ref[0])
    att_ref[0] = gates[:, 0:1] * ocmp_ref[0] + gates[:, 1:2] * o_sel + gates[:, 2:3] * o_win

    new_col = jnp.concatenate([kvw_ref[0], jnp.zeros((LANES - 1, KV_DIM), F32)], axis=0).T
    tile_lane = lax.broadcasted_iota(jnp.int32, (HEAD_DIM, w_buf), 1)
    for c in range(2):
        for h in range(N_KV_HEADS):
            lo = (c * N_KV_HEADS + h) * HEAD_DIM
            newwin_ref[0, 0, c, h] = jnp.where(tile_lane == w_buf - 1, new_col[lo:lo + HEAD_DIM, 0:1],
                                               pltpu.roll(win_ref[0, 0, c, h], w_buf - 1, 1))


def _sel_win_sample(idx, page_table, q8, gate8, o_cmp, kvs_new, kvw_new, win_t, slc_t):
    db = page_table.shape[0]
    w_buf = win_t.shape[-1]
    assert w_buf == WINDOW and win_t.shape[1] == 1
    per_tok = lambda shape: pl.BlockSpec((1,) + shape[1:], lambda i, idx, pt, nd=len(shape): (i,) + (0,) * (nd - 1))
    grid_spec = pltpu.PrefetchScalarGridSpec(
        num_scalar_prefetch=2,
        grid=(db,),
        in_specs=[per_tok(q8.shape), per_tok(gate8.shape), per_tok(o_cmp.shape), per_tok(kvs_new.shape),
                  per_tok(kvw_new.shape), per_tok(win_t.shape), pl.BlockSpec(memory_space=pl.ANY)],
        out_specs=[per_tok(o_cmp.shape), per_tok(win_t.shape)],
        scratch_shapes=[pltpu.VMEM((2, N_KV_HEADS, 2, HEAD_DIM, N_SEL_COLS), F32), pltpu.SemaphoreType.DMA((2,))],
    )
    return pl.pallas_call(
        _sel_win_sample_kernel,
        grid_spec=grid_spec,
        out_shape=[jax.ShapeDtypeStruct(o_cmp.shape, F32), jax.ShapeDtypeStruct(win_t.shape, F32)],
        compiler_params=pltpu.CompilerParams(dimension_semantics=("arbitrary",), vmem_limit_bytes=VMEM_LIMIT),
        name="nsa_sel_win_sample",
    )(idx, page_table, q8, gate8, o_cmp, kvs_new, kvw_new, win_t, slc_t)


SSD_BT = 8


def _ssd_sample_kernel(xbc_ref, cst_ref, z_ref, small_ref, h0_ref, convw_ref, convb_ref, dtb_ref, alog_ref,
                       dskip_ref, norm_ref, y_ref, hnew_ref, ys_ref):
    conv = convb_ref[...] + xbc_ref[...] * convw_ref[CONV_W - 1:CONV_W, :]
    for k in range(CONV_W - 1):
        conv = conv + cst_ref[k] * convw_ref[k:k + 1, :]
    xc = _silu(conv)
    xs = xc[:, :D_INNER]
    dt = jax.nn.softplus(small_ref[...] + dtb_ref[...])
    dt_e = _expand_heads(dt)
    decay_e = jnp.exp(dt_e * _expand_heads(-jnp.exp(alog_ref[...])))
    xd = xs * dt_e
    fill = jnp.zeros((LANES - SSD_BT, D_INNER), F32)
    xd_t = jnp.concatenate([xd, fill], axis=0).T
    decay_t = jnp.concatenate([decay_e, fill], axis=0).T
    lane = lax.broadcasted_iota(jnp.int32, (1, D_INNER), 1)
    for i in range(SSD_BT):
        bsel = jnp.concatenate(
            [jnp.broadcast_to(xc[i:i + 1, D_INNER + g * D_STATE:D_INNER + (g + 1) * D_STATE], (GROUP_W, D_STATE))
             for g in range(N_GROUPS)], axis=0)
        h0 = h0_ref[i].reshape(D_INNER, D_STATE)
        hn = decay_t[:, i:i + 1] * h0 + xd_t[:, i:i + 1] * bsel
        hnew_ref[i] = hn.reshape(SSM_HEADS, SSM_HEAD_DIM, D_STATE)
        c8 = jnp.concatenate(
            [xc[i:i + 1, D_INNER + (N_GROUPS + g) * D_STATE:D_INNER + (N_GROUPS + g + 1) * D_STATE]
             for g in range(N_GROUPS)] + [jnp.zeros((HEAD_ROWS - N_GROUPS, D_STATE), F32)], axis=0)
        y8 = _dot_nt(c8.astype(BF16), hn.astype(BF16))
        ys_ref[i:i + 1, :] = jnp.where(lane < GROUP_W, y8[0:1], y8[1:2])
    y = ys_ref[...] + dskip_ref[...] * xs
    y_ref[...] = _grouped_norm_gate(y, z_ref[...], norm_ref[...])


def _ssd_sample(xbc, conv_state_t, z, small, h0, conv_w, conv_b, dt_bias, a_log, d_skip, ssm_norm):
    db = xbc.shape[0]
    assert db % SSD_BT == 0 and N_GROUPS == 2
    params = (conv_w, conv_b.reshape(1, CONV_DIM), _pad_small(dt_bias), _pad_small(a_log),
              jnp.repeat(d_skip.astype(F32), SSM_HEAD_DIM).reshape(1, D_INNER), ssm_norm.reshape(1, D_INNER))
    rows = lambda w: pl.BlockSpec((SSD_BT, w), lambda i: (i, 0))
    state_spec = pl.BlockSpec((SSD_BT, SSM_HEADS, SSM_HEAD_DIM, D_STATE), lambda i: (i, 0, 0, 0))
    return pl.pallas_call(
        _ssd_sample_kernel,
        grid=(db // SSD_BT,),
        in_specs=[rows(CONV_DIM), pl.BlockSpec((CONV_W - 1, SSD_BT, CONV_DIM), lambda i: (0, i, 0)), rows(D_INNER),
                  rows(SMALL_W), state_spec] + [_const_spec(p.shape) for p in params],
        out_specs=[rows(D_INNER), state_spec],
        out_shape=[jax.ShapeDtypeStruct((db, D_INNER), F32), jax.ShapeDtypeStruct(h0.shape, F32)],
        scratch_shapes=[pltpu.VMEM((SSD_BT, D_INNER), F32)],
        compiler_params=pltpu.CompilerParams(dimension_semantics=("arbitrary",), vmem_limit_bytes=VMEM_LIMIT),
        name="ssd_sample",
    )(xbc, conv_state_t, z, small, h0, *params)


PROMPT_TM = 512


def _ffn_weights(w_gate, w_up, w_down):
    chunked = lambda w: jnp.transpose(w.astype(BF16).reshape(D_MODEL, N_FF_CHUNKS, FF_CHUNK), (1, 0, 2))
    return chunked(w_gate), chunked(w_up), w_down.astype(BF16).reshape(N_FF_CHUNKS, FF_CHUNK, D_MODEL)


def _in_proj_weight(w_in):
    cuts = np.cumsum([ATT_DIM, KV_DIM, KV_DIM, KV_DIM, N_GATE, D_INNER, CONV_DIM]).tolist()
    q, kvc, kvs, kvw, g, z, xbc, dt = jnp.split(w_in, cuts, axis=-1)
    pad = jnp.zeros((D_MODEL, SMALL_W - N_GATE - SSM_HEADS), w_in.dtype)
    return jnp.concatenate([q, kvc, kvs, kvw, z, xbc, g, dt, pad], axis=-1).astype(BF16)


def kernel(x_prompt, x_sample, cache_cmp_kv, cache_slc_kv, cache_win_kv, state_conv, state_ssm, page_table,
           p_prompt, p_sample, ffn1_norm, ffn1_w_gate, ffn1_w_up, ffn1_w_down, mix_norm, w_in,
           cmp_w1, cmp_pe, cmp_b1, cmp_w2, conv_w, conv_b, dt_bias, a_log, d_skip, ssm_norm, w_out,
           ffn2_norm, ffn2_w_gate, ffn2_w_up, ffn2_w_down, ple_norm, w_ple_gate, w_ple, final_norm):
    b, l, _ = x_prompt.shape
    db, t_new, _ = x_sample.shape
    depth = ffn1_norm.shape[0]
    n_pool = cache_cmp_kv.shape[0]
    n_pages = page_table.shape[1]
    past = n_pages * PAGE_SIZE
    assert depth == 1 and t_new == 1 and past >= WINDOW and (b * l) % PROMPT_TM == 0
    i = 0
    row = lambda v: v.reshape(1, -1).astype(F32)

    ffn1 = _ffn_weights(ffn1_w_gate[i], ffn1_w_up[i], ffn1_w_down[i])
    stage1_w = (row(ffn1_norm[i]),) + ffn1 + (row(mix_norm[i]), _in_proj_weight(w_in[i]))
    cmp_w = _cmp_weights(cmp_w1[i], cmp_pe[i], cmp_b1[i], cmp_w2[i])
    ssm_w = (conv_w[i], conv_b[i], dt_bias[i], a_log[i], d_skip[i], ssm_norm[i])
    w_o = w_out[i].astype(BF16)
    stage3_w = (w_o[:ATT_DIM], w_o[ATT_DIM:], row(ffn2_norm[i])) + _ffn_weights(ffn2_w_gate[i], ffn2_w_up[i], ffn2_w_down[i]) + (
        row(ple_norm[i]), w_ple_gate[i].astype(BF16), w_ple[i].astype(BF16), row(final_norm))
    kv6 = lambda a, lead, rows: a.reshape(lead, 1, rows, 2, N_KV_HEADS, HEAD_DIM)

    h, q, kvc, kvs, kvw, z, xbc, small, kvc_t, kvs_t, kaug, vaug_t, kwin = _ffn_inproj(
        x_prompt.reshape(b * l, D_MODEL), *stage1_w, PROMPT_TM, cache_batch=b)
    per_b = lambda a: a.reshape(b, l, a.shape[-1])
    kcv, kcv_t = _cmp_prompt(per_b(kvc), *cmp_w)
    att = _nsa_prompt(per_b(q), per_b(small), kcv, kcv_t, kaug, vaug_t, per_b(kwin))
    ssm, ssm_state = _ssd_prompt(per_b(xbc), per_b(z), per_b(small), *ssm_w)
    y_prompt = _out_ffn(h, att.reshape(b * l, ATT_DIM), ssm.reshape(b * l, D_INNER),
                        p_prompt[i].reshape(b * l, PLE_DIM), stage3_w, PROMPT_TM).reshape(b, l, D_MODEL)
    keep = min(WINDOW, l)
    rows_major = lambda a: jnp.transpose(a, (0, 1, 5, 2, 3, 4))
    new_cmp_p = rows_major(kvc_t)
    new_slc_p = rows_major(kvs_t)
    new_win_p = kv6(per_b(kvw)[:, l - keep:], b, keep)
    new_conv_p = per_b(xbc)[:, l - (CONV_W - 1):].reshape(b, 1, CONV_W - 1, CONV_DIM)
    new_ssm_p = ssm_state.reshape(b, 1, SSM_HEADS, SSM_HEAD_DIM, D_STATE)

    hs, qs, kvc_s, kvs_s, kvw_s, z_s, xbc_s, small_s = _ffn_inproj(x_sample.reshape(db, D_MODEL), *stage1_w, db)
    q8 = qs.reshape(db, N_HEADS, HEAD_DIM)
    n_cmp_s = (past + t_new - CMP_LEN) // CMP_STRIDE + 1
    past_blk = past // SEL_BLOCK
    mselt_s = _cmp_to_sel_t(n_pages * BLOCKS_PER_PAGE, n_cmp_s, past_blk)
    row_minor = lambda a: jnp.transpose(a, (0, 1, 3, 4, 5, 2))
    o_cmp, imp = _cmp_sample(page_table, q8, row_minor(cache_cmp_kv), *cmp_w, mselt_s, n_cmp_s)
    n_query = db * N_KV_HEADS
    imp2 = jnp.pad(imp[:, :N_KV_HEADS].reshape(n_query, SEL_LANES), ((0, -n_query % LANES), (0, 0)))
    idx = _topk_sample(imp2, past_blk)
    att_s, new_win_t = _sel_win_sample(
        idx, page_table, q8, small_s[:, :N_GATE].reshape(db, N_HEADS, 3), o_cmp,
        kvs_s.reshape(db, 1, KV_DIM), kvw_s.reshape(db, 1, KV_DIM),
        row_minor(cache_win_kv), row_minor(cache_slc_kv))
    new_win_s = rows_major(new_win_t)
    ssm_s, ssm_state_s = _ssd_sample(xbc_s, jnp.transpose(state_conv[:, i], (1, 0, 2)), z_s, small_s,
                                     state_ssm[:, i], *ssm_w)
    y_sample = _out_ffn(hs, att_s.reshape(db, ATT_DIM), ssm_s, p_sample[i].reshape(db, PLE_DIM),
                        stage3_w, db).reshape(db, 1, D_MODEL)
    new_conv_s = jnp.concatenate([state_conv[:, i, 1:], xbc_s[:, None]], axis=1)[:, None]

    return (y_prompt, y_sample, new_cmp_p, new_slc_p, new_win_p, new_conv_p, new_ssm_p,
            kv6(kvc_s, db, 1), kv6(kvs_s, db, 1), new_win_s, new_conv_s, ssm_state_s[:, None])
```
